```python
import math
import jax, jax.numpy as jnp
from jax import lax
import numpy as np

D_MODEL = 1024
BATCH = 8
SEQ = 4096
DEPTH = 2

CTX_LEN = 256
GRID_W = 64
HEAD_DIM = 64
ROPE_THETA = 10000.0
EPS = 1e-6
BLOCK = 128
A_HEADS = 8
A_KV_HEADS = 2
A_GROUPS = A_HEADS // A_KV_HEADS
WINDOW = 128
B_HEADS = 4
B_VDIM = 2 * HEAD_DIM
C_WIDTH = 512
C_BLOCKS = 8
C_BW = C_WIDTH // C_BLOCKS
CONV_W = 4
LRU_C = 8.0
N_BRANCH = 3
BRANCH_W = 512
FFN_HIDDEN = -(-8 * D_MODEL // (3 * 256)) * 256
IN_SIZES = (A_HEADS * HEAD_DIM, A_KV_HEADS * HEAD_DIM, A_KV_HEADS * HEAD_DIM,
            B_HEADS * 2 * HEAD_DIM, B_HEADS * 2 * HEAD_DIM, B_HEADS * B_VDIM,
            C_WIDTH, C_WIDTH, N_BRANCH * D_MODEL)
IN_WIDTH = sum(IN_SIZES)
IN_SPLITS = tuple(int(s) for s in np.cumsum(IN_SIZES)[:-1])

kernel_name = "hybrid_gated_swa_diffattn_rglru_ctxprefix"


def rms_norm(x, g):
    xf = x.astype(jnp.float32)
    y = xf * lax.rsqrt(jnp.mean(xf * xf, axis=-1, keepdims=True) + EPS)
    return (y * g.astype(jnp.float32)).astype(x.dtype)


def modulate(h, shift, scale):
    return h * (1 + scale) + shift


def axial_rope_tables(n_tokens, dtype):
    rows = n_tokens // GRID_W
    row, col = jnp.meshgrid(jnp.arange(rows), jnp.arange(GRID_W), indexing="ij")
    pos = jnp.stack([row.reshape(-1), col.reshape(-1)], axis=-1).astype(jnp.float32)
    rd = HEAD_DIM // 4
    inv = 1.0 / (ROPE_THETA ** (jnp.arange(rd, dtype=jnp.float32) * 2.0 / (HEAD_DIM // 2)))
    ang = pos[:, :, None] * inv
    return jnp.cos(ang).astype(dtype), jnp.sin(ang).astype(dtype)


def apply_rope(x, cos, sin):
    shp = x.shape
    rd = HEAD_DIM // 4
    xs = x.reshape(shp[:-1] + (2, 2, rd))
    bshape = (1, shp[1]) + (1,) * (len(shp) - 3) + (2, rd)
    cb = cos.reshape(bshape)
    sb = sin.reshape(bshape)
    x1 = xs[..., 0, :]
    x2 = xs[..., 1, :]
    out = jnp.stack([x1 * cb - x2 * sb, x2 * cb + x1 * sb], axis=-2)
    return out.reshape(shp)


def window_attention(q, k, v, kc, vc, sink):
    bsz, n_tok = q.shape[:2]
    n_ctx = kc.shape[1]
    nb = n_tok // BLOCK
    span = BLOCK + 2 * WINDOW
    scale = HEAD_DIM ** -0.5
    qb = q.reshape(bsz, nb, BLOCK, A_KV_HEADS, A_GROUPS, HEAD_DIM).transpose(1, 0, 2, 3, 4, 5)
    pad = ((0, 0), (WINDOW, WINDOW), (0, 0), (0, 0))
    kp = jnp.pad(k, pad)
    vp = jnp.pad(v, pad)
    qi = jnp.arange(BLOCK)[:, None]
    kj = jnp.arange(span)[None, :]
    rel = kj - WINDOW - qi
    sink_l = sink.astype(jnp.float32).reshape(A_KV_HEADS, A_GROUPS)[None, :, :, None, None]

    def one_block(args):
        n, qblk = args
        start = n * BLOCK
        kw = lax.dynamic_slice_in_dim(kp, start, span, axis=1)
        vw = lax.dynamic_slice_in_dim(vp, start, span, axis=1)
        kpos = start - WINDOW + kj
        valid = (jnp.abs(rel) <= WINDOW) & (kpos >= 0) & (kpos < n_tok)
        s_loc = jnp.einsum("bqhgd,bkhd->bhgqk", qblk, kw).astype(jnp.float32) * scale
        s_loc = jnp.where(valid, s_loc, -jnp.inf)
        s_ctx = jnp.einsum("bqhgd,bchd->bhgqc", qblk, kc).astype(jnp.float32) * scale
        s_sink = jnp.broadcast_to(sink_l, s_ctx.shape[:-1] + (1,))
        p = jax.nn.softmax(jnp.concatenate([s_loc, s_ctx, s_sink], axis=-1), axis=-1).astype(q.dtype)
        return (jnp.einsum("bhgqk,bkhd->bqhgd", p[..., :span], vw)
                + jnp.einsum("bhgqc,bchd->bqhgd", p[..., span:span + n_ctx], vc))

    out = lax.map(one_block, (jnp.arange(nb), qb))
    return out.transpose(1, 0, 2, 3, 4, 5).reshape(bsz, n_tok, A_HEADS * HEAD_DIM)


def sink_attention_dense(q, k, v, sink):
    bsz, n = q.shape[:2]
    qg = q.reshape(bsz, n, A_KV_HEADS, A_GROUPS, HEAD_DIM)
    s = jnp.einsum("bqhgd,bkhd->bhgqk", qg, k).astype(jnp.float32) * HEAD_DIM ** -0.5
    sink_l = sink.astype(jnp.float32).reshape(A_KV_HEADS, A_GROUPS)[None, :, :, None, None]
    s_sink = jnp.broadcast_to(sink_l, s.shape[:-1] + (1,))
    p = jax.nn.softmax(jnp.concatenate([s, s_sink], axis=-1), axis=-1)[..., :-1].astype(q.dtype)
    return jnp.einsum("bhgqk,bkhd->bqhgd", p, v).reshape(bsz, n, A_HEADS * HEAD_DIM)


def diff_attend(q, k, v, lam):
    s = jnp.einsum("bqhmd,bkhmd->bhmqk", q, k).astype(jnp.float32) * HEAD_DIM ** -0.5
    p = jax.nn.softmax(s, axis=-1)
    w = (p[:, :, 0] - lam * p[:, :, 1]).astype(v.dtype)
    return jnp.einsum("bhqk,bkhe->bqhe", w, v)


def diff_attention_latent(q, k, v, kc, vc, lam):
    bsz, n_tok = q.shape[:2]
    nb = n_tok // BLOCK
    k_all = jnp.concatenate([k, kc], axis=1)
    v_all = jnp.concatenate([v, vc], axis=1)
    qb = q.reshape(bsz, nb, BLOCK, B_HEADS, 2, HEAD_DIM).swapaxes(0, 1)
    out = lax.map(lambda qblk: diff_attend(qblk, k_all, v_all, lam), qb)
    return out.swapaxes(0, 1).reshape(bsz, n_tok, B_HEADS, B_VDIM)


def short_conv(u, w, b):
    left = CONV_W // 2
    y = lax.conv_general_dilated(u, w[:, None, :].astype(u.dtype), window_strides=(1,),
                                 padding=[(left, CONV_W - 1 - left)],
                                 dimension_numbers=("NWC", "WIO", "NWC"),
                                 feature_group_count=C_WIDTH)
    return y + b


def lru_coeffs(u, w_gate, b_gate, lam):
    bsz, n = u.shape[:2]
    ub = u.reshape(bsz, n, C_BLOCKS, C_BW)
    g = jnp.einsum("blnc,dgncf->dgblnf", ub, w_gate).reshape(2, 2, bsz, n, C_WIDTH)
    g = jax.nn.sigmoid((g + b_gate[:, :, None, None, :]).astype(jnp.float32))
    r, i = g[:, 0], g[:, 1]
    log_a = -LRU_C * jax.nn.softplus(-lam.astype(jnp.float32))[:, None, None, :] * r
    a = jnp.exp(log_a)
    bx = jnp.sqrt(-jnp.expm1(2.0 * log_a)) * i * u.astype(jnp.float32)
    return a, bx


def linear_scan(a, b, h0):
    b = b.at[:, 0].add(a[:, 0] * h0)

    def combine(e1, e2):
        a1, b1 = e1
        a2, b2 = e2
        return a1 * a2, a2 * b1 + b2

    _, h = lax.associative_scan(combine, (a, b), axis=1)
    return h


def bidir_lru(a, bx, h0_fwd, h0_bwd):
    h_f = linear_scan(a[0], bx[0], h0_fwd)
    h_b = jnp.flip(linear_scan(jnp.flip(a[1], 1), jnp.flip(bx[1], 1), h0_bwd), 1)
    return h_f, h_b


def gated_merge(ys, gates, w_branch, w_out):
    g = jax.nn.sigmoid(gates.reshape(gates.shape[:-1] + (N_BRANCH, D_MODEL)))
    return jnp.sum(g * jnp.einsum("blnw,nwd->blnd", ys, w_branch), axis=2) @ w_out


def swiglu(h, w_in, w_out):
    gate, up = jnp.split(h @ w_in, 2, axis=-1)
    return (jax.nn.silu(gate) * up) @ w_out


def trunk_layer(l, x, ctx, c, c_ctx, cos, sin, w_mod, b_mod, norm_g, w_in, attn_sink,
                diff_lambda, diff_subln, conv_w, conv_b, lru_w, lru_b, lru_lambda,
                w_branch, w_out, w_ffn_in, w_ffn_out, need_ctx):
    bsz, n_tok = x.shape[:2]
    n_ctx = ctx.shape[1]
    mx = (jax.nn.silu(c) @ w_mod + b_mod)[:, None, :]
    mc = (jax.nn.silu(c_ctx) @ w_mod + b_mod)[None, None, :]
    shx1, scx1, gx1, shx2, scx2, gx2 = jnp.split(mx, 6, axis=-1)
    shc1, scc1, gc1, shc2, scc2, gc2 = jnp.split(mc, 6, axis=-1)

    px = modulate(rms_norm(x, norm_g[0]), shx1, scx1) @ w_in
    pc = modulate(rms_norm(ctx, norm_g[0]), shc1, scc1) @ w_in
    aq, ak, av, bq, bk, bv, cu, cg, gt = jnp.split(px, IN_SPLITS, axis=-1)
    aq_c, ak_c, av_c, bq_c, bk_c, bv_c, cu_c, cg_c, gt_c = jnp.split(pc, IN_SPLITS, axis=-1)

    qA = apply_rope(aq.reshape(bsz, n_tok, A_HEADS, HEAD_DIM), cos, sin)
    kA = apply_rope(ak.reshape(bsz, n_tok, A_KV_HEADS, HEAD_DIM), cos, sin)
    vA = av.reshape(bsz, n_tok, A_KV_HEADS, HEAD_DIM)
    kA_c = ak_c.reshape(bsz, n_ctx, A_KV_HEADS, HEAD_DIM)
    vA_c = av_c.reshape(bsz, n_ctx, A_KV_HEADS, HEAD_DIM)
    yA = window_attention(qA, kA, vA, kA_c, vA_c, attn_sink)

    lam_init = 0.8 - 0.6 * math.exp(-0.3 * l)
    dl = diff_lambda.astype(jnp.float32)
    lam = jnp.exp(jnp.sum(dl[0] * dl[1])) - jnp.exp(jnp.sum(dl[2] * dl[3])) + lam_init
    qB = apply_rope(bq.reshape(bsz, n_tok, B_HEADS, 2, HEAD_DIM), cos, sin)
    kB = apply_rope(bk.reshape(bsz, n_tok, B_HEADS, 2, HEAD_DIM), cos, sin)
    vB = bv.reshape(bsz, n_tok, B_HEADS, B_VDIM)
    kB_c = bk_c.reshape(bsz, n_ctx, B_HEADS, 2, HEAD_DIM)
    vB_c = bv_c.reshape(bsz, n_ctx, B_HEADS, B_VDIM)
    yB = diff_attention_latent(qB, kB, vB, kB_c, vB_c, lam)
    yB = (rms_norm(yB, diff_subln) * (1.0 - lam_init)).reshape(bsz, n_tok, BRANCH_W)

    u_c = short_conv(cu_c, conv_w, conv_b)
    a_c, b_c = lru_coeffs(u_c, lru_w, lru_b, lru_lambda)
    h0 = jnp.zeros((bsz, C_WIDTH), jnp.float32)
    hf_c, hb_c = bidir_lru(a_c, b_c, h0, h0)
    u_l = short_conv(cu, conv_w, conv_b)
    a_l, b_l = lru_coeffs(u_l, lru_w, lru_b, lru_lambda)
    hf_l, hb_l = bidir_lru(a_l, b_l, hf_c[:, -1], hb_c[:, 0])
    yC = ((hf_l + hb_l) * jax.nn.gelu(cg.astype(jnp.float32))).astype(x.dtype)

    mix = gated_merge(jnp.stack([yA, yB, yC], axis=2), gt, w_branch, w_out)
    x = x + gx1 * rms_norm(mix, norm_g[1])
    f = swiglu(modulate(rms_norm(x, norm_g[2]), shx2, scx2), w_ffn_in, w_ffn_out)
    x = x + gx2 * rms_norm(f, norm_g[3])
    if not need_ctx:
        return x, None

    yA_c = sink_attention_dense(aq_c.reshape(bsz, n_ctx, A_HEADS, HEAD_DIM), kA_c, vA_c, attn_sink)
    yB_c = diff_attend(bq_c.reshape(bsz, n_ctx, B_HEADS, 2, HEAD_DIM), kB_c, vB_c, lam)
    yB_c = (rms_norm(yB_c, diff_subln) * (1.0 - lam_init)).reshape(bsz, n_ctx, BRANCH_W)
    yC_c = ((hf_c + hb_c) * jax.nn.gelu(cg_c.astype(jnp.float32))).astype(ctx.dtype)
    mix_c = gated_merge(jnp.stack([yA_c, yB_c, yC_c], axis=2), gt_c, w_branch, w_out)
    ctx = ctx + gc1 * rms_norm(mix_c, norm_g[1])
    f_c = swiglu(modulate(rms_norm(ctx, norm_g[2]), shc2, scc2), w_ffn_in, w_ffn_out)
    ctx = ctx + gc2 * rms_norm(f_c, norm_g[3])
    return x, ctx


def setup_inputs(seed: int = 0) -> dict:
    key = jax.random.key(seed)
    ks = jax.random.split(key, 24)
    f32 = jnp.float32

    def nrm(k, shape, scale):
        return jax.random.normal(k, shape, f32) * scale

    u = jax.random.uniform(ks[20], (DEPTH, 2, C_WIDTH), f32, minval=0.9, maxval=0.999)
    a0 = u ** (1.0 / LRU_C)
    lru_lambda = jnp.log(a0) - jnp.log1p(-a0)
    return {
        "x": nrm(ks[0], (BATCH, SEQ, D_MODEL), 1.0),
        "c": nrm(ks[1], (BATCH, D_MODEL), 1.0),
        "ctx": nrm(ks[2], (BATCH, CTX_LEN, D_MODEL), 1.0),
        "c_ctx": nrm(ks[3], (D_MODEL,), 1.0),
        "w_mod": nrm(ks[4], (DEPTH, D_MODEL, 6 * D_MODEL), 0.5 * D_MODEL ** -0.5),
        "b_mod": nrm(ks[5], (DEPTH, 6 * D_MODEL), 0.02),
        "norm_g": 1.0 + nrm(ks[6], (DEPTH, 4, D_MODEL), 0.02),
        "w_in": nrm(ks[7], (DEPTH, D_MODEL, IN_WIDTH), D_MODEL ** -0.5),
        "attn_sink": nrm(ks[8], (DEPTH, A_HEADS), 0.5),
        "diff_lambda": nrm(ks[9], (DEPTH, 4, HEAD_DIM), 0.1),
        "diff_subln": 1.0 + nrm(ks[10], (DEPTH, B_VDIM), 0.02),
        "conv_w": nrm(ks[11], (DEPTH, CONV_W, C_WIDTH), CONV_W ** -0.5),
        "conv_b": nrm(ks[12], (DEPTH, C_WIDTH), 0.02),
        "lru_w": nrm(ks[13], (DEPTH, 2, 2, C_BLOCKS, C_BW, C_BW), C_BW ** -0.5),
        "lru_b": nrm(ks[14], (DEPTH, 2, 2, C_WIDTH), 0.02),
        "lru_lambda": lru_lambda,
        "w_branch": nrm(ks[15], (DEPTH, N_BRANCH, BRANCH_W, D_MODEL), BRANCH_W ** -0.5),
        "w_out": nrm(ks[16], (DEPTH, D_MODEL, D_MODEL), D_MODEL ** -0.5),
        "w_ffn_in": nrm(ks[17], (DEPTH, D_MODEL, 2 * FFN_HIDDEN), D_MODEL ** -0.5),
        "w_ffn_out": nrm(ks[18], (DEPTH, FFN_HIDDEN, D_MODEL), FFN_HIDDEN ** -0.5),
    }


def reference(x, c, ctx, c_ctx, w_mod, b_mod, norm_g, w_in, attn_sink, diff_lambda, diff_subln,
              conv_w, conv_b, lru_w, lru_b, lru_lambda, w_branch, w_out, w_ffn_in, w_ffn_out):
    cos, sin = axial_rope_tables(x.shape[1], x.dtype)
    for l in range(DEPTH):
        x, ctx = trunk_layer(l, x, ctx, c, c_ctx, cos, sin, w_mod[l], b_mod[l], norm_g[l], w_in[l],
                             attn_sink[l], diff_lambda[l], diff_subln[l], conv_w[l], conv_b[l],
                             lru_w[l], lru_b[l], lru_lambda[l], w_branch[l], w_out[l],
                             w_ffn_in[l], w_ffn_out[l], need_ctx=(l < DEPTH - 1))
    return x
```

```python
import functools
import math

import jax
import jax.numpy as jnp
from jax import lax
from jax.experimental import pallas as pl
from jax.experimental.pallas import tpu as pltpu

F32 = jnp.float32
BF16 = jnp.bfloat16

D_MODEL = 1024
GRID_W = 64
HEAD_DIM = 64
ROPE_THETA = 10000.0
EPS = 1e-6
A_HEADS = 8
A_KV_HEADS = 2
A_GROUPS = A_HEADS // A_KV_HEADS
WINDOW = 128
B_HEADS = 4
B_VDIM = 2 * HEAD_DIM
C_WIDTH = 512
C_BLOCKS = 8
C_BW = C_WIDTH // C_BLOCKS
CONV_W = 4
LRU_C = 8.0
N_BRANCH = 3
BRANCH_W = 512
FFN_HIDDEN = -(-8 * D_MODEL // (3 * 256)) * 256

O_AQ = 0
O_AK = O_AQ + A_HEADS * HEAD_DIM
O_AV = O_AK + A_KV_HEADS * HEAD_DIM
O_BQ = O_AV + A_KV_HEADS * HEAD_DIM
O_BK = O_BQ + B_HEADS * 2 * HEAD_DIM
O_BV = O_BK + B_HEADS * 2 * HEAD_DIM
O_CU = O_BV + B_HEADS * B_VDIM
O_CG = O_CU + C_WIDTH
O_GT = O_CG + C_WIDTH
IN_WIDTH = O_GT + N_BRANCH * D_MODEL

LANES = 128
SUBLANES = 8
VMEM_CAP_BYTES = 60 * 1024 * 1024
NEG_BIG = -1e30
Q_BLOCK = 128
DIFF_TQ = 256
DIFF_TK = 512
LRU_ROWS = 256
LRU_PAD = 8


def _params(semantics, vmem_bytes):
    return pltpu.CompilerParams(dimension_semantics=semantics,
                                vmem_limit_bytes=min(int(vmem_bytes), VMEM_CAP_BYTES))


def _resident(shape):
    nd = len(shape)
    return pl.BlockSpec(shape, lambda *_: (0,) * nd, pipeline_mode=pl.Buffered(1))


def _rms(x, g):
    return x * lax.rsqrt(jnp.mean(x * x, axis=-1, keepdims=True) + EPS) * g


def _mod_kernel(c_ref, w_ref, b_ref, o_ref):
    c = c_ref[...]
    s = c * jax.nn.sigmoid(c)
    o_ref[...] = jnp.dot(s, w_ref[...], preferred_element_type=F32,
                         precision=lax.Precision.HIGHEST) + b_ref[...]


def _modulation(c_rows, w_mod, b_mod):
    depth, d, n = w_mod.shape
    rows = c_rows.shape[0]
    tn = 1536
    return pl.pallas_call(
        _mod_kernel,
        grid=(depth, n // tn),
        in_specs=[pl.BlockSpec((rows, d), lambda l, j: (0, 0)),
                  pl.BlockSpec((None, d, tn), lambda l, j: (l, 0, j)),
                  pl.BlockSpec((None, 1, tn), lambda l, j: (l, 0, j))],
        out_specs=pl.BlockSpec((None, rows, tn), lambda l, j: (l, 0, j)),
        out_shape=jax.ShapeDtypeStruct((depth, rows, n), F32),
        compiler_params=_params(("parallel", "parallel"), 32 << 20),
        name="modulation",
    )(c_rows, w_mod, b_mod.reshape(depth, 1, n))


def _in_proj_kernel(x_ref, sh_ref, sc_ref, g_ref, cos_ref, sa_ref, sb_ref, w_ref,
                    qa_ref, ka_ref, va_ref, qb_ref, kb_ref, vb_ref, cu_ref, cg_ref, gt_ref):
    h = _rms(x_ref[...], g_ref[...])
    h = (h * (1.0 + sc_ref[...]) + sh_ref[...]).astype(BF16)

    def proj(lo, width):
        return jnp.dot(h, w_ref[:, lo:lo + width], preferred_element_type=F32)

    cos = cos_ref[...]
    sin_hi = sa_ref[...]
    sin_lo = sb_ref[...]

    def rope(a):
        outs = []
        for k in range(a.shape[1] // LANES):
            blk = a[:, k * LANES:(k + 1) * LANES]
            outs.append(blk * cos + pltpu.roll(blk, HEAD_DIM // 4, 1) * sin_hi
                        + pltpu.roll(blk, LANES - HEAD_DIM // 4, 1) * sin_lo)
        return jnp.concatenate(outs, axis=1)

    qa_ref[...] = rope(proj(O_AQ, O_AK - O_AQ)).T.astype(BF16)
    ka_ref[...] = rope(proj(O_AK, O_AV - O_AK)).astype(BF16)
    va_ref[...] = proj(O_AV, O_BQ - O_AV).T.astype(BF16)
    qb_ref[...] = rope(proj(O_BQ, O_BK - O_BQ)).T.astype(BF16)
    kb_ref[...] = rope(proj(O_BK, O_BV - O_BK)).astype(BF16)
    vb_ref[...] = proj(O_BV, O_CU - O_BV).T.astype(BF16)
    cu_ref[...] = proj(O_CU, C_WIDTH).astype(BF16)
    cg_ref[...] = proj(O_CG, C_WIDTH).astype(BF16)
    for k in range(N_BRANCH):
        gt_ref[:, k * D_MODEL:(k + 1) * D_MODEL] = proj(O_GT + k * D_MODEL, D_MODEL).astype(BF16)


def _in_proj(x2, shift, scale, g0, cos, sin_hi, sin_lo, w_bf, tm):
    t, d = x2.shape
    nt = t // tm
    nb = shift.shape[0]
    npos = cos.shape[0] // tm
    tok = lambda w: pl.BlockSpec((tm, w), lambda i: (i, 0))
    feat = lambda w: pl.BlockSpec((w, tm), lambda i: (0, i))
    modspec = pl.BlockSpec((None, 1, d), lambda i: (i // (nt // nb), 0, 0))
    tabspec = pl.BlockSpec((tm, LANES), lambda i: (i % npos, 0))
    wq, wkv = A_HEADS * HEAD_DIM, A_KV_HEADS * HEAD_DIM
    wb = B_HEADS * B_VDIM
    out_shape = (
        jax.ShapeDtypeStruct((wq, t), BF16), jax.ShapeDtypeStruct((t, wkv), BF16),
        jax.ShapeDtypeStruct((wkv, t), BF16), jax.ShapeDtypeStruct((wb, t), BF16),
        jax.ShapeDtypeStruct((t, wb), BF16), jax.ShapeDtypeStruct((wb, t), BF16),
        jax.ShapeDtypeStruct((t, C_WIDTH), BF16), jax.ShapeDtypeStruct((t, C_WIDTH), BF16),
        jax.ShapeDtypeStruct((t, N_BRANCH * D_MODEL), BF16))
    out_specs = (feat(wq), tok(wkv), feat(wkv), feat(wb), tok(wb), feat(wb),
                 tok(C_WIDTH), tok(C_WIDTH), tok(N_BRANCH * D_MODEL))
    vmem = d * IN_WIDTH * 2 + 2 * tm * d * 4 + 2 * tm * IN_WIDTH * 2 + 6 * tm * d * 4 + (8 << 20)
    return pl.pallas_call(
        _in_proj_kernel,
        grid=(nt,),
        in_specs=[tok(d), modspec, modspec, _resident((1, d)), tabspec, tabspec, tabspec,
                  _resident((d, IN_WIDTH))],
        out_specs=out_specs,
        out_shape=out_shape,
        compiler_params=_params(("parallel",), vmem),
        name="in_proj",
    )(x2, shift, scale, g0, cos, sin_hi, sin_lo, w_bf)


def _sink_attention(k_all, vt_all, qt, sink_row, bias, n_win):
    tq = qt.shape[1]
    zero = jnp.zeros((HEAD_DIM, tq), qt.dtype)
    outs = []
    for j in range(A_KV_HEADS):
        cols = []
        for g in range(A_GROUPS):
            hd = j * A_GROUPS + g
            qh = qt[hd * HEAD_DIM:(hd + 1) * HEAD_DIM, :]
            cols.append(jnp.concatenate([qh, zero] if j == 0 else [zero, qh], axis=0))
        qz = jnp.concatenate(cols, axis=1)
        s = jnp.dot(k_all, qz, preferred_element_type=F32)
        sink = sink_row[:, j * A_GROUPS * tq:(j + 1) * A_GROUPS * tq]
        if n_win:
            sw = s[:n_win] + jnp.concatenate([bias] * A_GROUPS, axis=1)
            sc = s[n_win:]
            m = jnp.maximum(jnp.maximum(jnp.max(sw, axis=0, keepdims=True),
                                        jnp.max(sc, axis=0, keepdims=True)), sink)
            ew = jnp.exp(sw - m)
            ec = jnp.exp(sc - m)
            l = (jnp.sum(ew, axis=0, keepdims=True) + jnp.sum(ec, axis=0, keepdims=True)
                 + jnp.exp(sink - m))
            p = jnp.concatenate([ew, ec], axis=0).astype(BF16)
        else:
            m = jnp.maximum(jnp.max(s, axis=0, keepdims=True), sink)
            e = jnp.exp(s - m)
            l = jnp.sum(e, axis=0, keepdims=True) + jnp.exp(sink - m)
            p = e.astype(BF16)
        o = jnp.dot(vt_all[j * HEAD_DIM:(j + 1) * HEAD_DIM, :], p, preferred_element_type=F32)
        o = o * (1.0 / l)
        for g in range(A_GROUPS):
            outs.append(o[:, g * tq:(g + 1) * tq])
    return jnp.concatenate(outs, axis=0)


def _win_attn_kernel(q_ref, kp_ref, kc_ref, kn_ref, vp_ref, vc_ref, vn_ref, kx_ref, vx_ref,
                     sink_ref, o_ref, *, n_tok):
    n = pl.program_id(1)
    tq = q_ref.shape[1]
    k_all = jnp.concatenate([kp_ref[...], kc_ref[...], kn_ref[...], kx_ref[...]], axis=0)
    vt_all = jnp.concatenate([vp_ref[...], vc_ref[...], vn_ref[...], vx_ref[...]], axis=1)
    n_win = 3 * tq
    r = lax.broadcasted_iota(jnp.int32, (n_win, tq), 0)
    c = lax.broadcasted_iota(jnp.int32, (n_win, tq), 1)
    rel = r - tq - c
    kpos = (n - 1) * tq + r
    bias = (jnp.where(jnp.abs(rel) <= WINDOW, 0.0, NEG_BIG) + jnp.where(kpos >= 0, 0.0, NEG_BIG)
            + jnp.where(kpos < n_tok, 0.0, NEG_BIG)).astype(F32)
    yt = _sink_attention(k_all, vt_all, q_ref[...], sink_ref[...], bias, n_win)
    o_ref[...] = yt.T.astype(o_ref.dtype)


def _ctx_attn_kernel(q_ref, kx_ref, vx_ref, sink_ref, o_ref):
    yt = _sink_attention(kx_ref[...], vx_ref[...], q_ref[...], sink_ref[...], None, 0)
    o_ref[...] = yt.T.astype(o_ref.dtype)


def _sink_row(sink, tq):
    return jnp.repeat(sink.astype(F32), tq).reshape(1, A_HEADS * tq)


def _window_attention(qt, k, vt, kx, vxt, sink, bsz, n_tok, n_ctx):
    tq = Q_BLOCK
    nq = n_tok // tq
    wq, wkv = A_HEADS * HEAD_DIM, A_KV_HEADS * HEAD_DIM
    prev = lambda b, n: b * nq + jnp.maximum(n - 1, 0)
    cur = lambda b, n: b * nq + n
    nxt = lambda b, n: b * nq + jnp.minimum(n + 1, nq - 1)
    kspec = lambda f: pl.BlockSpec((tq, wkv), lambda b, n: (f(b, n), 0))
    vspec = lambda f: pl.BlockSpec((wkv, tq), lambda b, n: (0, f(b, n)))
    return pl.pallas_call(
        functools.partial(_win_attn_kernel, n_tok=n_tok),
        grid=(bsz, nq),
        in_specs=[pl.BlockSpec((wq, tq), lambda b, n: (0, cur(b, n))),
                  kspec(prev), kspec(cur), kspec(nxt), vspec(prev), vspec(cur), vspec(nxt),
                  pl.BlockSpec((n_ctx, wkv), lambda b, n: (b, 0)),
                  pl.BlockSpec((wkv, n_ctx), lambda b, n: (0, b)),
                  pl.BlockSpec((1, A_HEADS * tq), lambda b, n: (0, 0))],
        out_specs=pl.BlockSpec((tq, wq), lambda b, n: (cur(b, n), 0)),
        out_shape=jax.ShapeDtypeStruct((bsz * n_tok, wq), BF16),
        compiler_params=_params(("parallel", "parallel"), 32 << 20),
        name="window_attention",
    )(qt, k, k, k, vt, vt, vt, kx, vxt, _sink_row(sink, tq))


def _context_attention(qt, kx, vxt, sink, bsz, n_ctx):
    tq = Q_BLOCK
    nq = n_ctx // tq
    wq, wkv = A_HEADS * HEAD_DIM, A_KV_HEADS * HEAD_DIM
    return pl.pallas_call(
        _ctx_attn_kernel,
        grid=(bsz, nq),
        in_specs=[pl.BlockSpec((wq, tq), lambda b, n: (0, b * nq + n)),
                  pl.BlockSpec((n_ctx, wkv), lambda b, n: (b, 0)),
                  pl.BlockSpec((wkv, n_ctx), lambda b, n: (0, b)),
                  pl.BlockSpec((1, A_HEADS * tq), lambda b, n: (0, 0))],
        out_specs=pl.BlockSpec((tq, wq), lambda b, n: (b * nq + n, 0)),
        out_shape=jax.ShapeDtypeStruct((bsz * n_ctx, wq), BF16),
        compiler_params=_params(("parallel", "parallel"), 32 << 20),
        name="context_attention",
    )(qt, kx, vxt, _sink_row(sink, tq))


def _diff_attn_kernel(*refs, n_lat_chunks, tk, lam_init):
    if n_lat_chunks:
        q_ref, k_ref, v_ref, kx_ref, vx_ref, dl_ref, g_ref, o_ref, m_ref, l_ref, acc_ref = refs
    else:
        q_ref, kx_ref, vx_ref, dl_ref, g_ref, o_ref, m_ref, l_ref, acc_ref = refs
    qt = q_ref[...]
    tq = qt.shape[1]
    row = lax.broadcasted_iota(jnp.int32, qt.shape, 0)
    zero = jnp.zeros_like(qt)
    qz = jnp.concatenate([jnp.where(row < HEAD_DIM, qt, zero),
                          jnp.where(row >= HEAD_DIM, qt, zero)], axis=1)
    m_ref[...] = jnp.full(m_ref.shape, NEG_BIG, F32)
    l_ref[...] = jnp.zeros(l_ref.shape, F32)
    acc_ref[...] = jnp.zeros(acc_ref.shape, F32)

    def step(kc, vtc):
        s = jnp.dot(kc, qz, preferred_element_type=F32)
        m_old = m_ref[...]
        m_new = jnp.maximum(m_old, jnp.max(s, axis=0, keepdims=True))
        alpha = jnp.exp(m_old - m_new)
        e = jnp.exp(s - m_new)
        l_ref[...] = alpha * l_ref[...] + jnp.sum(e, axis=0, keepdims=True)
        acc_ref[...] = alpha * acc_ref[...] + jnp.dot(vtc, e.astype(BF16),
                                                      preferred_element_type=F32)
        m_ref[...] = m_new

    if n_lat_chunks:
        def body(ci, carry):
            start = pl.multiple_of(ci * tk, tk)
            step(k_ref[pl.ds(start, tk), :], v_ref[:, pl.ds(start, tk)])
            return carry
        lax.fori_loop(0, n_lat_chunks, body, 0)
    step(kx_ref[...], vx_ref[...])

    dl = dl_ref[...]
    lam = (jnp.exp(jnp.sum(dl[0:1] * dl[1:2], axis=1, keepdims=True))
           - jnp.exp(jnp.sum(dl[2:3] * dl[3:4], axis=1, keepdims=True)) + lam_init)
    o = acc_ref[...] * (1.0 / l_ref[...])
    y = (o[:, :tq] - lam * o[:, tq:]).T
    o_ref[...] = (_rms(y, g_ref[...]) * (1.0 - lam_init)).astype(o_ref.dtype)


def _diff_attention(qt, k, vt, kx, vxt, diff_lambda, subln, lam_init, bsz, n_tok, n_ctx):
    nq_tok = n_tok if n_tok else n_ctx
    tq = min(DIFF_TQ, nq_tok)
    nqt = nq_tok // tq
    tk = min(DIFF_TK, n_tok) if n_tok else 0
    hw = B_VDIM
    in_specs = [pl.BlockSpec((hw, tq), lambda b, h, i: (h, b * nqt + i))]
    args = [qt]
    if n_tok:
        in_specs += [pl.BlockSpec((n_tok, hw), lambda b, h, i: (b, h)),
                     pl.BlockSpec((hw, n_tok), lambda b, h, i: (h, b))]
        args += [k, vt]
    in_specs += [pl.BlockSpec((n_ctx, hw), lambda b, h, i: (b, h)),
                 pl.BlockSpec((hw, n_ctx), lambda b, h, i: (h, b)),
                 pl.BlockSpec((4, HEAD_DIM), lambda b, h, i: (0, 0)),
                 pl.BlockSpec((1, hw), lambda b, h, i: (0, 0))]
    args += [kx, vxt, diff_lambda.astype(F32), subln.astype(F32).reshape(1, hw)]
    return pl.pallas_call(
        functools.partial(_diff_attn_kernel, n_lat_chunks=(n_tok // tk if n_tok else 0), tk=tk,
                          lam_init=lam_init),
        grid=(bsz, B_HEADS, nqt),
        in_specs=in_specs,
        out_specs=pl.BlockSpec((tq, hw), lambda b, h, i: (b * nqt + i, h)),
        out_shape=jax.ShapeDtypeStruct((bsz * nq_tok, B_HEADS * hw), BF16),
        scratch_shapes=[pltpu.VMEM((1, 2 * tq), F32), pltpu.VMEM((1, 2 * tq), F32),
                        pltpu.VMEM((hw, 2 * tq), F32)],
        compiler_params=_params(("parallel", "parallel", "parallel"), 40 << 20),
        name="diff_attention",
    )(*args)


def _scan_rows(a, b, h_in, rowm, reverse):
    rows = a.shape[0]
    for d in (1, 2, 4):
        if reverse:
            keep = rowm < SUBLANES - d
            a_s = jnp.where(keep, pltpu.roll(a, rows - d, 0), 1.0)
            b_s = jnp.where(keep, pltpu.roll(b, rows - d, 0), 0.0)
        else:
            keep = rowm >= d
            a_s = jnp.where(keep, pltpu.roll(a, d, 0), 1.0)
            b_s = jnp.where(keep, pltpu.roll(b, d, 0), 0.0)
        b = b + a * b_s
        a = a * a_s
    groups = rows // SUBLANES
    outs = [None] * groups
    h = h_in
    for g in (range(groups - 1, -1, -1) if reverse else range(groups)):
        hg = b[g * SUBLANES:(g + 1) * SUBLANES] + a[g * SUBLANES:(g + 1) * SUBLANES] * h
        outs[g] = hg
        h = hg[0:1] if reverse else hg[SUBLANES - 1:SUBLANES]
    return jnp.concatenate(outs, axis=0), h


def _lru_kernel(cu_ref, cg_ref, h0f_ref, h0b_ref, cw_ref, cb_ref, wg_ref, bg_ref, lam_ref,
                y_ref, hf_ref, hb_ref, xpad_ref, hbs_ref, *, rows):
    n = cu_ref.shape[0]
    nchunks = n // rows
    w = C_WIDTH
    xpad_ref[0:LRU_PAD, :] = jnp.zeros((LRU_PAD, w), F32)
    xpad_ref[LRU_PAD + n:2 * LRU_PAD + n, :] = jnp.zeros((LRU_PAD, w), F32)

    def fill(ci, carry):
        s = pl.multiple_of(ci * rows, rows)
        xpad_ref[pl.ds(s + LRU_PAD, rows), :] = cu_ref[pl.ds(s, rows), :].astype(F32)
        return carry
    lax.fori_loop(0, nchunks, fill, 0)

    lam = lam_ref[...]
    decay = LRU_C * (jnp.maximum(-lam, 0.0) + jnp.log1p(jnp.exp(-jnp.abs(lam))))
    cw = cw_ref[...]
    cb = cb_ref[...]
    rowm = lax.broadcasted_iota(jnp.int32, (rows, w), 0) % SUBLANES

    def coeffs(s, d):
        xw = xpad_ref[pl.ds(s, rows + 2 * LRU_PAD), :]
        base = LRU_PAD - CONV_W // 2
        u = cb
        for k in range(CONV_W):
            u = u + cw[k:k + 1] * xw[base + k:base + k + rows]
        g = jnp.dot(u.astype(BF16), wg_ref[:, d * 2 * w:(d + 1) * 2 * w],
                    preferred_element_type=F32) + bg_ref[:, d * 2 * w:(d + 1) * 2 * w]
        r = jax.nn.sigmoid(g[:, :w])
        i = jax.nn.sigmoid(g[:, w:])
        z = decay[d:d + 1] * r
        a = jnp.exp(-z)
        bx = jnp.sqrt((1.0 + a * a) * jnp.tanh(z)) * i * u
        return a, bx

    def bwd(ci, h):
        s = pl.multiple_of((nchunks - 1 - ci) * rows, rows)
        a, bx = coeffs(s, 1)
        hs, h = _scan_rows(a, bx, h, rowm, True)
        hbs_ref[pl.ds(s, rows), :] = hs
        return h
    hb_ref[...] = lax.fori_loop(0, nchunks, bwd, h0b_ref[...])

    def fwd(ci, h):
        s = pl.multiple_of(ci * rows, rows)
        a, bx = coeffs(s, 0)
        hs, h = _scan_rows(a, bx, h, rowm, False)
        gate = jax.nn.gelu(cg_ref[pl.ds(s, rows), :].astype(F32))
        y_ref[pl.ds(s, rows), :] = ((hs + hbs_ref[pl.ds(s, rows), :]) * gate).astype(y_ref.dtype)
        return h
    hf_ref[...] = lax.fori_loop(0, nchunks, fwd, h0f_ref[...])


def _bidir_lru(cu, cg, h0f, h0b, conv_w, conv_b, wg_bf, bg, lam, bsz, n):
    w = C_WIDTH
    rows = min(LRU_ROWS, n)
    seq = pl.BlockSpec((n, w), lambda b: (b, 0))
    st = pl.BlockSpec((None, 1, w), lambda b: (b, 0, 0))
    vmem = 6 * n * w * 2 + n * w * 4 + (n + 2 * LRU_PAD) * w * 4 + (16 << 20)
    return pl.pallas_call(
        functools.partial(_lru_kernel, rows=rows),
        grid=(bsz,),
        in_specs=[seq, seq, st, st, _resident((CONV_W, w)), _resident((1, w)),
                  _resident((w, 4 * w)), _resident((1, 4 * w)), _resident((2, w))],
        out_specs=(seq, st, st),
        out_shape=(jax.ShapeDtypeStruct((bsz * n, w), BF16),
                   jax.ShapeDtypeStruct((bsz, 1, w), F32), jax.ShapeDtypeStruct((bsz, 1, w), F32)),
        scratch_shapes=[pltpu.VMEM((n + 2 * LRU_PAD, w), F32), pltpu.VMEM((n, w), F32)],
        compiler_params=_params(("parallel",), vmem),
        name="bidir_lru",
    )(cu, cg, h0f, h0b, conv_w, conv_b, wg_bf, bg, lam)


def _merge_kernel(ya_ref, yb_ref, yc_ref, gt_ref, x_ref, gx_ref, g_ref, wbr_ref, wout_ref, o_ref):
    acc = None
    for n, y_ref in enumerate((ya_ref, yb_ref, yc_ref)):
        z = jnp.dot(y_ref[...], wbr_ref[n], preferred_element_type=F32)
        gate = jax.nn.sigmoid(gt_ref[:, n * D_MODEL:(n + 1) * D_MODEL].astype(F32))
        acc = gate * z if acc is None else acc + gate * z
    mix = jnp.dot(acc.astype(BF16), wout_ref[...], preferred_element_type=F32)
    o_ref[...] = x_ref[...] + gx_ref[...] * _rms(mix, g_ref[...])


def _merge(ya, yb, yc, gt, x2, gate_x, g1, wbr_bf, wout_bf, tm):
    t, d = x2.shape
    nt = t // tm
    nb = gate_x.shape[0]
    tok = lambda w: pl.BlockSpec((tm, w), lambda i: (i, 0))
    modspec = pl.BlockSpec((None, 1, d), lambda i: (i // (nt // nb), 0, 0))
    vmem = (N_BRANCH * BRANCH_W * d + d * d) * 2 + 2 * tm * (3 * BRANCH_W + 3 * d) * 2 \
        + 4 * tm * d * 4 + 6 * tm * d * 4 + (8 << 20)
    return pl.pallas_call(
        _merge_kernel,
        grid=(nt,),
        in_specs=[tok(BRANCH_W), tok(BRANCH_W), tok(BRANCH_W), tok(N_BRANCH * d), tok(d), modspec,
                  _resident((1, d)), _resident((N_BRANCH, BRANCH_W, d)), _resident((d, d))],
        out_specs=tok(d),
        out_shape=jax.ShapeDtypeStruct((t, d), F32),
        compiler_params=_params(("parallel",), vmem),
        name="gated_merge",
    )(ya, yb, yc, gt, x2, gate_x, g1, wbr_bf, wout_bf)


def _ffn_kernel(x_ref, sh_ref, sc_ref, gx_ref, g2_ref, g3_ref, w1_ref, w2_ref, o_ref, *, n_chunks):
    x = x_ref[...]
    h = _rms(x, g2_ref[...])
    h = (h * (1.0 + sc_ref[...]) + sh_ref[...]).astype(BF16)
    hc = FFN_HIDDEN // n_chunks
    f = None
    for c in range(n_chunks):
        gate = jnp.dot(h, w1_ref[:, c * hc:(c + 1) * hc], preferred_element_type=F32)
        up = jnp.dot(h, w1_ref[:, FFN_HIDDEN + c * hc:FFN_HIDDEN + (c + 1) * hc],
                     preferred_element_type=F32)
        act = (gate * jax.nn.sigmoid(gate) * up).astype(BF16)
        part = jnp.dot(act, w2_ref[c * hc:(c + 1) * hc, :], preferred_element_type=F32)
        f = part if f is None else f + part
    o_ref[...] = x + gx_ref[...] * _rms(f, g3_ref[...])


def _ffn(x2, shift, scale, gate_x, g2, g3, w1_bf, w2_bf, tm):
    t, d = x2.shape
    nt = t // tm
    nb = shift.shape[0]
    tok = pl.BlockSpec((tm, d), lambda i: (i, 0))
    modspec = pl.BlockSpec((None, 1, d), lambda i: (i // (nt // nb), 0, 0))
    n_chunks = 2
    vmem = 3 * d * FFN_HIDDEN * 2 + 4 * tm * d * 4 + 3 * tm * (FFN_HIDDEN // n_chunks) * 4 \
        + 4 * tm * d * 4 + (8 << 20)
    return pl.pallas_call(
        functools.partial(_ffn_kernel, n_chunks=n_chunks),
        grid=(nt,),
        in_specs=[tok, modspec, modspec, modspec, _resident((1, d)), _resident((1, d)),
                  _resident((d, 2 * FFN_HIDDEN)), _resident((FFN_HIDDEN, d))],
        out_specs=tok,
        out_shape=jax.ShapeDtypeStruct((t, d), F32),
        compiler_params=_params(("parallel",), vmem),
        name="swiglu_ffn",
    )(x2, shift, scale, gate_x, g2, g3, w1_bf, w2_bf)


def _rope_tables(n_tokens):
    rd = HEAD_DIM // 4
    t = jnp.arange(n_tokens)
    pos = jnp.stack([t // GRID_W, t % GRID_W], axis=-1).astype(F32)
    inv = 1.0 / (ROPE_THETA ** (jnp.arange(rd, dtype=F32) * 2.0 / (HEAD_DIM // 2)))
    ang = pos[:, :, None] * inv
    cos = jnp.cos(ang)[:, :, None, :]
    sin = jnp.sin(ang)[:, :, None, :]
    zeros = jnp.zeros_like(sin)
    cos64 = jnp.concatenate([cos, cos], axis=2).reshape(n_tokens, HEAD_DIM)
    hi64 = jnp.concatenate([zeros, sin], axis=2).reshape(n_tokens, HEAD_DIM)
    lo64 = jnp.concatenate([-sin, zeros], axis=2).reshape(n_tokens, HEAD_DIM)
    rep = LANES // HEAD_DIM
    return jnp.tile(cos64, (1, rep)), jnp.tile(hi64, (1, rep)), jnp.tile(lo64, (1, rep))


def _identity_tables(rows):
    z = jnp.zeros((rows, LANES), F32)
    return jnp.ones((rows, LANES), F32), z, z


def _gate_weights(lru_w, lru_b):
    eye = jnp.eye(C_BLOCKS, dtype=lru_w.dtype)
    dense = jnp.einsum("dgncf,nm->ncdgmf", lru_w, eye).reshape(C_WIDTH, 4 * C_WIDTH)
    return dense.astype(BF16), lru_b.astype(F32).reshape(1, 4 * C_WIDTH)


def kernel(x, c, ctx, c_ctx, w_mod, b_mod, norm_g, w_in, attn_sink, diff_lambda, diff_subln, conv_w,
           conv_b, lru_w, lru_b, lru_lambda, w_branch, w_out, w_ffn_in, w_ffn_out):
    bsz, n_tok, d = x.shape
    n_ctx = ctx.shape[1]
    depth = w_mod.shape[0]
    tm = min(512, n_tok)
    tmc = min(256, n_ctx)

    mod_rows = 2 * SUBLANES
    c_rows = jnp.zeros((mod_rows, d), F32).at[:bsz].set(c).at[bsz].set(c_ctx)
    mods = _modulation(c_rows, w_mod, b_mod)

    cos, sin_hi, sin_lo = _rope_tables(n_tok)
    cos_c, sin_hi_c, sin_lo_c = _identity_tables(tmc)
    q_scale = jnp.ones((IN_WIDTH,), F32).at[O_AQ:O_AK].set(HEAD_DIM ** -0.5)
    q_scale = q_scale.at[O_BQ:O_BK].set(HEAD_DIM ** -0.5)

    x2 = x.reshape(bsz * n_tok, d)
    cx2 = ctx.reshape(bsz * n_ctx, d)
    zero_state = jnp.zeros((bsz, 1, C_WIDTH), F32)

    for l in range(depth):
        need_ctx = l < depth - 1
        lam_init = 0.8 - 0.6 * math.exp(-0.3 * l)
        mx = [mods[l, :bsz, k * d:(k + 1) * d].reshape(bsz, 1, d) for k in range(6)]
        mc = [mods[l, bsz:bsz + 1, k * d:(k + 1) * d].reshape(1, 1, d) for k in range(6)]
        g = [norm_g[l, k].reshape(1, d).astype(F32) for k in range(4)]
        w_in_bf = (w_in[l] * q_scale).astype(BF16)
        wg_bf, bg = _gate_weights(lru_w[l], lru_b[l])
        cw = conv_w[l].astype(F32)
        cb = conv_b[l].astype(F32).reshape(1, C_WIDTH)
        lam = lru_lambda[l].astype(F32)
        wbr_bf = w_branch[l].astype(BF16)
        wout_bf = w_out[l].astype(BF16)
        w1_bf = w_ffn_in[l].astype(BF16)
        w2_bf = w_ffn_out[l].astype(BF16)

        (qa_c, ka_c, va_c, qb_c, kb_c, vb_c, cu_c, cg_c, gt_c) = _in_proj(
            cx2, mc[0], mc[1], g[0], cos_c, sin_hi_c, sin_lo_c, w_in_bf, tmc)
        yc_c, hf_c, hb_c = _bidir_lru(cu_c, cg_c, zero_state, zero_state, cw, cb, wg_bf, bg, lam,
                                      bsz, n_ctx)

        (qa, ka, va, qb, kb, vb, cu, cg, gt) = _in_proj(
            x2, mx[0], mx[1], g[0], cos, sin_hi, sin_lo, w_in_bf, tm)
        ya = _window_attention(qa, ka, va, ka_c, va_c, attn_sink[l], bsz, n_tok, n_ctx)
        yb = _diff_attention(qb, kb, vb, kb_c, vb_c, diff_lambda[l], diff_subln[l], lam_init,
                             bsz, n_tok, n_ctx)
        yc, _, _ = _bidir_lru(cu, cg, hf_c, hb_c, cw, cb, wg_bf, bg, lam, bsz, n_tok)
        x2 = _merge(ya, yb, yc, gt, x2, mx[2], g[1], wbr_bf, wout_bf, tm)
        x2 = _ffn(x2, mx[3], mx[4], mx[5], g[2], g[3], w1_bf, w2_bf, tm)

        if need_ctx:
            ya_c = _context_attention(qa_c, ka_c, va_c, attn_sink[l], bsz, n_ctx)
            yb_c = _diff_attention(qb_c, None, None, kb_c, vb_c, diff_lambda[l], diff_subln[l],
                                   lam_init, bsz, 0, n_ctx)
            cx2 = _merge(ya_c, yb_c, yc_c, gt_c, cx2, mc[2], g[1], wbr_bf, wout_bf, tmc)
            cx2 = _ffn(cx2, mc[3], mc[4], mc[5], g[2], g[3], w1_bf, w2_bf, tmc)

    return x2.reshape(bsz, n_tok, d)
```

```python
import functools
import math

import jax
import jax.numpy as jnp
from jax import lax
from jax.experimental import pallas as pl
from jax.experimental.pallas import tpu as pltpu

F32 = jnp.float32
BF16 = jnp.bfloat16

D_MODEL = 1024
GRID_W = 64
HEAD_DIM = 64
ROPE_THETA = 10000.0
EPS = 1e-6
A_HEADS = 8
A_KV_HEADS = 2
A_GROUPS = A_HEADS // A_KV_HEADS
WINDOW = 128
B_HEADS = 4
B_VDIM = 2 * HEAD_DIM
C_WIDTH = 512
C_BLOCKS = 8
C_BW = C_WIDTH // C_BLOCKS
CONV_W = 4
LRU_C = 8.0
N_BRANCH = 3
BRANCH_W = 512
FFN_HIDDEN = -(-8 * D_MODEL // (3 * 256)) * 256

O_AQ = 0
O_AK = O_AQ + A_HEADS * HEAD_DIM
O_AV = O_AK + A_KV_HEADS * HEAD_DIM
O_BQ = O_AV + A_KV_HEADS * HEAD_DIM
O_BK = O_BQ + B_HEADS * 2 * HEAD_DIM
O_BV = O_BK + B_HEADS * 2 * HEAD_DIM
O_CU = O_BV + B_HEADS * B_VDIM
O_CG = O_CU + C_WIDTH
O_GT = O_CG + C_WIDTH
IN_WIDTH = O_GT + N_BRANCH * D_MODEL

LANES = 128
SUBLANES = 8
VMEM_CAP_BYTES = 60 * 1024 * 1024
NEG_BIG = -1e30
Q_BLOCK = 128
DIFF_TQ = 256
DIFF_TK = 512
LRU_ROWS = 256
LRU_PAD = 8
LOG2E = math.log2(math.e)
Q_SCALE = HEAD_DIM ** -0.5 * LOG2E


def _params(semantics, vmem_bytes):
    return pltpu.CompilerParams(dimension_semantics=semantics,
                                vmem_limit_bytes=min(int(vmem_bytes), VMEM_CAP_BYTES))


def _resident(shape):
    nd = len(shape)
    return pl.BlockSpec(shape, lambda *_: (0,) * nd, pipeline_mode=pl.Buffered(1))


def _rms(x, g):
    return x * lax.rsqrt(jnp.mean(x * x, axis=-1, keepdims=True) + EPS) * g


def _mod_kernel(c_ref, w_ref, b_ref, o_ref):
    c = c_ref[...]
    s = c * jax.nn.sigmoid(c)
    o_ref[...] = jnp.dot(s, w_ref[...], preferred_element_type=F32,
                         precision=lax.Precision.HIGHEST) + b_ref[...]


def _modulation(c_rows, w_mod, b_mod):
    depth, d, n = w_mod.shape
    rows = c_rows.shape[0]
    tn = 1536
    return pl.pallas_call(
        _mod_kernel,
        grid=(depth, n // tn),
        in_specs=[pl.BlockSpec((rows, d), lambda l, j: (0, 0)),
                  pl.BlockSpec((None, d, tn), lambda l, j: (l, 0, j)),
                  pl.BlockSpec((None, 1, tn), lambda l, j: (l, 0, j))],
        out_specs=pl.BlockSpec((None, rows, tn), lambda l, j: (l, 0, j)),
        out_shape=jax.ShapeDtypeStruct((depth, rows, n), F32),
        compiler_params=_params(("parallel", "parallel"), 32 << 20),
        name="modulation",
    )(c_rows, w_mod, b_mod.reshape(depth, 1, n))


def _in_proj_kernel(x_ref, sh_ref, sc_ref, g_ref, cos_ref, sa_ref, sb_ref, w_ref,
                    qa_ref, ka_ref, va_ref, qb_ref, kb_ref, vb_ref, cu_ref, cg_ref, gt_ref):
    h = _rms(x_ref[...], g_ref[...])
    h = (h * (1.0 + sc_ref[...]) + sh_ref[...]).astype(BF16)

    def proj(lo, width):
        return jnp.dot(h, w_ref[:, lo:lo + width], preferred_element_type=F32)

    cos = cos_ref[...]
    sin_hi = sa_ref[...]
    sin_lo = sb_ref[...]

    def rope(a):
        outs = []
        for k in range(a.shape[1] // LANES):
            blk = a[:, k * LANES:(k + 1) * LANES]
            outs.append(blk * cos + pltpu.roll(blk, HEAD_DIM // 4, 1) * sin_hi
                        + pltpu.roll(blk, LANES - HEAD_DIM // 4, 1) * sin_lo)
        return jnp.concatenate(outs, axis=1)

    qa_ref[...] = (rope(proj(O_AQ, O_AK - O_AQ)) * Q_SCALE).T.astype(BF16)
    ka_ref[...] = rope(proj(O_AK, O_AV - O_AK)).astype(BF16)
    va_ref[...] = proj(O_AV, O_BQ - O_AV).T.astype(BF16)
    qb_ref[...] = (rope(proj(O_BQ, O_BK - O_BQ)) * Q_SCALE).T.astype(BF16)
    kb_ref[...] = rope(proj(O_BK, O_BV - O_BK)).astype(BF16)
    vb_ref[...] = proj(O_BV, O_CU - O_BV).T.astype(BF16)
    cu_ref[...] = proj(O_CU, C_WIDTH).astype(BF16)
    cg_ref[...] = proj(O_CG, C_WIDTH).astype(BF16)
    for k in range(N_BRANCH):
        gt_ref[:, k * D_MODEL:(k + 1) * D_MODEL] = proj(O_GT + k * D_MODEL, D_MODEL).astype(BF16)


def _in_proj(x2, shift, scale, g0, cos, sin_hi, sin_lo, w_bf, tm):
    t, d = x2.shape
    nt = t // tm
    nb = shift.shape[0]
    npos = cos.shape[0] // tm
    tok = lambda w: pl.BlockSpec((tm, w), lambda i: (i, 0))
    feat = lambda w: pl.BlockSpec((w, tm), lambda i: (0, i))
    modspec = pl.BlockSpec((None, 1, d), lambda i: (i // (nt // nb), 0, 0))
    tabspec = pl.BlockSpec((tm, LANES), lambda i: (i % npos, 0))
    wq, wkv = A_HEADS * HEAD_DIM, A_KV_HEADS * HEAD_DIM
    wb = B_HEADS * B_VDIM
    out_shape = (
        jax.ShapeDtypeStruct((wq, t), BF16), jax.ShapeDtypeStruct((t, wkv), BF16),
        jax.ShapeDtypeStruct((wkv, t), BF16), jax.ShapeDtypeStruct((wb, t), BF16),
        jax.ShapeDtypeStruct((t, wb), BF16), jax.ShapeDtypeStruct((wb, t), BF16),
        jax.ShapeDtypeStruct((t, C_WIDTH), BF16), jax.ShapeDtypeStruct((t, C_WIDTH), BF16),
        jax.ShapeDtypeStruct((t, N_BRANCH * D_MODEL), BF16))
    out_specs = (feat(wq), tok(wkv), feat(wkv), feat(wb), tok(wb), feat(wb),
                 tok(C_WIDTH), tok(C_WIDTH), tok(N_BRANCH * D_MODEL))
    vmem = d * IN_WIDTH * 2 + 2 * tm * d * 4 + 2 * tm * IN_WIDTH * 2 + 6 * tm * d * 4 + (8 << 20)
    return pl.pallas_call(
        _in_proj_kernel,
        grid=(nt,),
        in_specs=[tok(d), modspec, modspec, _resident((1, d)), tabspec, tabspec, tabspec,
                  _resident((d, IN_WIDTH))],
        out_specs=out_specs,
        out_shape=out_shape,
        compiler_params=_params(("parallel",), vmem),
        name="in_proj",
    )(x2, shift, scale, g0, cos, sin_hi, sin_lo, w_bf)


def _sink_attention(k_all, vt_all, qt, sink_row, bias, n_win):
    tq = qt.shape[1]
    zero = jnp.zeros((HEAD_DIM, tq), qt.dtype)
    outs = []
    scores = []
    for j in range(A_KV_HEADS):
        cols = []
        for g in range(A_GROUPS):
            hd = j * A_GROUPS + g
            qh = qt[hd * HEAD_DIM:(hd + 1) * HEAD_DIM, :]
            cols.append(jnp.concatenate([qh, zero] if j == 0 else [zero, qh], axis=0))
        qz = jnp.concatenate(cols, axis=1)
        scores.append(jnp.dot(k_all, qz, preferred_element_type=F32))
    for j in range(A_KV_HEADS):
        s = scores[j]
        sink = sink_row[:, j * A_GROUPS * tq:(j + 1) * A_GROUPS * tq] * LOG2E
        if n_win:
            sw = s[:n_win] + jnp.concatenate([bias] * A_GROUPS, axis=1)
            sc = s[n_win:]
            m = jnp.maximum(jnp.maximum(jnp.max(sw, axis=0, keepdims=True),
                                        jnp.max(sc, axis=0, keepdims=True)), sink)
            ew = jnp.exp2(sw - m)
            ec = jnp.exp2(sc - m)
            l = (jnp.sum(ew, axis=0, keepdims=True) + jnp.sum(ec, axis=0, keepdims=True)
                 + jnp.exp2(sink - m))
            p = jnp.concatenate([ew, ec], axis=0).astype(BF16)
        else:
            m = jnp.maximum(jnp.max(s, axis=0, keepdims=True), sink)
            e = jnp.exp2(s - m)
            l = jnp.sum(e, axis=0, keepdims=True) + jnp.exp2(sink - m)
            p = e.astype(BF16)
        o = jnp.dot(vt_all[j * HEAD_DIM:(j + 1) * HEAD_DIM, :], p, preferred_element_type=F32)
        o = o * (1.0 / l)
        for g in range(A_GROUPS):
            outs.append(o[:, g * tq:(g + 1) * tq])
    return jnp.concatenate(outs, axis=0)


def _win_attn_kernel(q_ref, kp_ref, kc_ref, kn_ref, vp_ref, vc_ref, vn_ref, kx_ref, vx_ref,
                     sink_ref, o_ref, *, n_tok):
    n = pl.program_id(1)
    tq = q_ref.shape[1]
    k_all = jnp.concatenate([kp_ref[...], kc_ref[...], kn_ref[...], kx_ref[...]], axis=0)
    vt_all = jnp.concatenate([vp_ref[...], vc_ref[...], vn_ref[...], vx_ref[...]], axis=1)
    n_win = 3 * tq
    r = lax.broadcasted_iota(jnp.int32, (n_win, tq), 0)
    c = lax.broadcasted_iota(jnp.int32, (n_win, tq), 1)
    rel = r - tq - c
    kpos = (n - 1) * tq + r
    bias = (jnp.where(jnp.abs(rel) <= WINDOW, 0.0, NEG_BIG) + jnp.where(kpos >= 0, 0.0, NEG_BIG)
            + jnp.where(kpos < n_tok, 0.0, NEG_BIG)).astype(F32)
    yt = _sink_attention(k_all, vt_all, q_ref[...], sink_ref[...], bias, n_win)
    o_ref[...] = yt.T.astype(o_ref.dtype)


def _ctx_attn_kernel(q_ref, kx_ref, vx_ref, sink_ref, o_ref):
    yt = _sink_attention(kx_ref[...], vx_ref[...], q_ref[...], sink_ref[...], None, 0)
    o_ref[...] = yt.T.astype(o_ref.dtype)


def _sink_row(sink, tq):
    return jnp.repeat(sink.astype(F32), tq).reshape(1, A_HEADS * tq)


def _window_attention(qt, k, vt, kx, vxt, sink, bsz, n_tok, n_ctx):
    tq = Q_BLOCK
    nq = n_tok // tq
    wq, wkv = A_HEADS * HEAD_DIM, A_KV_HEADS * HEAD_DIM
    prev = lambda b, n: b * nq + jnp.maximum(n - 1, 0)
    cur = lambda b, n: b * nq + n
    nxt = lambda b, n: b * nq + jnp.minimum(n + 1, nq - 1)
    kspec = lambda f: pl.BlockSpec((tq, wkv), lambda b, n: (f(b, n), 0))
    vspec = lambda f: pl.BlockSpec((wkv, tq), lambda b, n: (0, f(b, n)))
    return pl.pallas_call(
        functools.partial(_win_attn_kernel, n_tok=n_tok),
        grid=(bsz, nq),
        in_specs=[pl.BlockSpec((wq, tq), lambda b, n: (0, cur(b, n))),
                  kspec(prev), kspec(cur), kspec(nxt), vspec(prev), vspec(cur), vspec(nxt),
                  pl.BlockSpec((n_ctx, wkv), lambda b, n: (b, 0)),
                  pl.BlockSpec((wkv, n_ctx), lambda b, n: (0, b)),
                  pl.BlockSpec((1, A_HEADS * tq), lambda b, n: (0, 0))],
        out_specs=pl.BlockSpec((tq, wq), lambda b, n: (cur(b, n), 0)),
        out_shape=jax.ShapeDtypeStruct((bsz * n_tok, wq), BF16),
        compiler_params=_params(("parallel", "parallel"), 32 << 20),
        name="window_attention",
    )(qt, k, k, k, vt, vt, vt, kx, vxt, _sink_row(sink, tq))


def _context_attention(qt, kx, vxt, sink, bsz, n_ctx):
    tq = Q_BLOCK
    nq = n_ctx // tq
    wq, wkv = A_HEADS * HEAD_DIM, A_KV_HEADS * HEAD_DIM
    return pl.pallas_call(
        _ctx_attn_kernel,
        grid=(bsz, nq),
        in_specs=[pl.BlockSpec((wq, tq), lambda b, n: (0, b * nq + n)),
                  pl.BlockSpec((n_ctx, wkv), lambda b, n: (b, 0)),
                  pl.BlockSpec((wkv, n_ctx), lambda b, n: (0, b)),
                  pl.BlockSpec((1, A_HEADS * tq), lambda b, n: (0, 0))],
        out_specs=pl.BlockSpec((tq, wq), lambda b, n: (b * nq + n, 0)),
        out_shape=jax.ShapeDtypeStruct((bsz * n_ctx, wq), BF16),
        compiler_params=_params(("parallel", "parallel"), 32 << 20),
        name="context_attention",
    )(qt, kx, vxt, _sink_row(sink, tq))


def _diff_attn_kernel(*refs, n_lat_chunks, tk, lam_init):
    if n_lat_chunks:
        q_ref, k_ref, v_ref, kx_ref, vx_ref, dl_ref, g_ref, o_ref = refs
    else:
        q_ref, kx_ref, vx_ref, dl_ref, g_ref, o_ref = refs
    qt = q_ref[...]
    tq = qt.shape[1]
    row = lax.broadcasted_iota(jnp.int32, qt.shape, 0)
    zero = jnp.zeros_like(qt)
    qz = jnp.concatenate([jnp.where(row < HEAD_DIM, qt, zero),
                          jnp.where(row >= HEAD_DIM, qt, zero)], axis=1)

    keys = [lambda ci=ci: k_ref[ci * tk:(ci + 1) * tk, :] for ci in range(n_lat_chunks)]
    vals = [lambda ci=ci: v_ref[:, ci * tk:(ci + 1) * tk] for ci in range(n_lat_chunks)]
    keys.append(lambda: kx_ref[...])
    vals.append(lambda: vx_ref[...])
    n = len(keys)

    def scores(i):
        return jnp.dot(keys[i](), qz, preferred_element_type=F32)

    m = jnp.full((1, 2 * tq), NEG_BIG, F32)
    l = jnp.zeros((1, 2 * tq), F32)
    acc = jnp.zeros((B_VDIM, 2 * tq), F32)
    s_next = scores(0)
    e_prev = alpha_prev = None
    for c in range(n):
        s = s_next
        if c + 1 < n:
            s_next = scores(c + 1)
        if c:
            acc = alpha_prev * acc + jnp.dot(vals[c - 1](), e_prev, preferred_element_type=F32)
        m_new = jnp.maximum(m, jnp.max(s, axis=0, keepdims=True))
        alpha_prev = jnp.exp2(m - m_new)
        e = jnp.exp2(s - m_new)
        l = alpha_prev * l + jnp.sum(e, axis=0, keepdims=True)
        e_prev = e.astype(BF16)
        m = m_new
    acc = alpha_prev * acc + jnp.dot(vals[n - 1](), e_prev, preferred_element_type=F32)

    dl = dl_ref[...]
    lam = (jnp.exp(jnp.sum(dl[0:1] * dl[1:2], axis=1, keepdims=True))
           - jnp.exp(jnp.sum(dl[2:3] * dl[3:4], axis=1, keepdims=True)) + lam_init)
    o = acc * (1.0 / l)
    y = (o[:, :tq] - lam * o[:, tq:]).T
    o_ref[...] = (_rms(y, g_ref[...]) * (1.0 - lam_init)).astype(o_ref.dtype)


def _diff_attention(qt, k, vt, kx, vxt, diff_lambda, subln, lam_init, bsz, n_tok, n_ctx):
    nq_tok = n_tok if n_tok else n_ctx
    tq = min(DIFF_TQ, nq_tok)
    nqt = nq_tok // tq
    tk = min(DIFF_TK, n_tok) if n_tok else 0
    hw = B_VDIM
    in_specs = [pl.BlockSpec((hw, tq), lambda b, h, i: (h, b * nqt + i))]
    args = [qt]
    if n_tok:
        in_specs += [pl.BlockSpec((n_tok, hw), lambda b, h, i: (b, h)),
                     pl.BlockSpec((hw, n_tok), lambda b, h, i: (h, b))]
        args += [k, vt]
    in_specs += [pl.BlockSpec((n_ctx, hw), lambda b, h, i: (b, h)),
                 pl.BlockSpec((hw, n_ctx), lambda b, h, i: (h, b)),
                 pl.BlockSpec((4, HEAD_DIM), lambda b, h, i: (0, 0)),
                 pl.BlockSpec((1, hw), lambda b, h, i: (0, 0))]
    args += [kx, vxt, diff_lambda.astype(F32), subln.astype(F32).reshape(1, hw)]
    return pl.pallas_call(
        functools.partial(_diff_attn_kernel, n_lat_chunks=(n_tok // tk if n_tok else 0), tk=tk,
                          lam_init=lam_init),
        grid=(bsz, B_HEADS, nqt),
        in_specs=in_specs,
        out_specs=pl.BlockSpec((tq, hw), lambda b, h, i: (b * nqt + i, h)),
        out_shape=jax.ShapeDtypeStruct((bsz * nq_tok, B_HEADS * hw), BF16),
        compiler_params=_params(("parallel", "parallel", "parallel"), 40 << 20),
        name="diff_attention",
    )(*args)


def _scan_rows(a, b, h_in, rowm, reverse):
    rows = a.shape[0]
    for d in (1, 2, 4):
        if reverse:
            keep = rowm < SUBLANES - d
            a_s = jnp.where(keep, pltpu.roll(a, rows - d, 0), 1.0)
            b_s = jnp.where(keep, pltpu.roll(b, rows - d, 0), 0.0)
        else:
            keep = rowm >= d
            a_s = jnp.where(keep, pltpu.roll(a, d, 0), 1.0)
            b_s = jnp.where(keep, pltpu.roll(b, d, 0), 0.0)
        b = b + a * b_s
        a = a * a_s
    groups = rows // SUBLANES
    outs = [None] * groups
    h = h_in
    for g in (range(groups - 1, -1, -1) if reverse else range(groups)):
        hg = b[g * SUBLANES:(g + 1) * SUBLANES] + a[g * SUBLANES:(g + 1) * SUBLANES] * h
        outs[g] = hg
        h = hg[0:1] if reverse else hg[SUBLANES - 1:SUBLANES]
    return jnp.concatenate(outs, axis=0), h


def _lru_kernel(cu_ref, cg_ref, h0f_ref, h0b_ref, cw_ref, cb_ref, wg_ref, bg_ref, lam_ref,
                y_ref, hf_ref, hb_ref, xpad_ref, hbs_ref, *, rows):
    n = cu_ref.shape[0]
    nchunks = n // rows
    w = C_WIDTH
    xpad_ref[0:LRU_PAD, :] = jnp.zeros((LRU_PAD, w), F32)
    xpad_ref[LRU_PAD + n:2 * LRU_PAD + n, :] = jnp.zeros((LRU_PAD, w), F32)

    def fill(ci, carry):
        s = pl.multiple_of(ci * rows, rows)
        xpad_ref[pl.ds(s + LRU_PAD, rows), :] = cu_ref[pl.ds(s, rows), :].astype(F32)
        return carry
    lax.fori_loop(0, nchunks, fill, 0)

    lam = lam_ref[...]
    decay = LRU_C * (jnp.maximum(-lam, 0.0) + jnp.log1p(jnp.exp(-jnp.abs(lam))))
    cw = cw_ref[...]
    cb = cb_ref[...]
    rowm = lax.broadcasted_iota(jnp.int32, (rows, w), 0) % SUBLANES

    def coeffs(s, d):
        xw = xpad_ref[pl.ds(s, rows + 2 * LRU_PAD), :]
        base = LRU_PAD - CONV_W // 2
        u = cb
        for k in range(CONV_W):
            u = u + cw[k:k + 1] * xw[base + k:base + k + rows]
        g = jnp.dot(u.astype(BF16), wg_ref[:, d * 2 * w:(d + 1) * 2 * w],
                    preferred_element_type=F32) + bg_ref[:, d * 2 * w:(d + 1) * 2 * w]
        r = jax.nn.sigmoid(g[:, :w])
        i = jax.nn.sigmoid(g[:, w:])
        z = decay[d:d + 1] * r
        a = jnp.exp(-z)
        bx = jnp.sqrt((1.0 + a * a) * jnp.tanh(z)) * i * u
        return a, bx

    def bwd(ci, h):
        s = pl.multiple_of((nchunks - 1 - ci) * rows, rows)
        a, bx = coeffs(s, 1)
        hs, h = _scan_rows(a, bx, h, rowm, True)
        hbs_ref[pl.ds(s, rows), :] = hs
        return h
    hb_ref[...] = lax.fori_loop(0, nchunks, bwd, h0b_ref[...])

    def fwd(ci, h):
        s = pl.multiple_of(ci * rows, rows)
        a, bx = coeffs(s, 0)
        hs, h = _scan_rows(a, bx, h, rowm, False)
        gate = jax.nn.gelu(cg_ref[pl.ds(s, rows), :].astype(F32))
        y_ref[pl.ds(s, rows), :] = ((hs + hbs_ref[pl.ds(s, rows), :]) * gate).astype(y_ref.dtype)
        return h
    hf_ref[...] = lax.fori_loop(0, nchunks, fwd, h0f_ref[...])


def _bidir_lru(cu, cg, h0f, h0b, conv_w, conv_b, wg_bf, bg, lam, bsz, n):
    w = C_WIDTH
    rows = min(LRU_ROWS, n)
    seq = pl.BlockSpec((n, w), lambda b: (b, 0))
    st = pl.BlockSpec((None, 1, w), lambda b: (b, 0, 0))
    vmem = 6 * n * w * 2 + n * w * 4 + (n + 2 * LRU_PAD) * w * 4 + (16 << 20)
    return pl.pallas_call(
        functools.partial(_lru_kernel, rows=rows),
        grid=(bsz,),
        in_specs=[seq, seq, st, st, _resident((CONV_W, w)), _resident((1, w)),
                  _resident((w, 4 * w)), _resident((1, 4 * w)), _resident((2, w))],
        out_specs=(seq, st, st),
        out_shape=(jax.ShapeDtypeStruct((bsz * n, w), BF16),
                   jax.ShapeDtypeStruct((bsz, 1, w), F32), jax.ShapeDtypeStruct((bsz, 1, w), F32)),
        scratch_shapes=[pltpu.VMEM((n + 2 * LRU_PAD, w), F32), pltpu.VMEM((n, w), F32)],
        compiler_params=_params(("parallel",), vmem),
        name="bidir_lru",
    )(cu, cg, h0f, h0b, conv_w, conv_b, wg_bf, bg, lam)


def _merge_kernel(ya_ref, yb_ref, yc_ref, gt_ref, x_ref, gx_ref, g_ref, wbr_ref, wout_ref, o_ref):
    acc = None
    for n, y_ref in enumerate((ya_ref, yb_ref, yc_ref)):
        z = jnp.dot(y_ref[...], wbr_ref[n], preferred_element_type=F32)
        gate = jax.nn.sigmoid(gt_ref[:, n * D_MODEL:(n + 1) * D_MODEL].astype(F32))
        acc = gate * z if acc is None else acc + gate * z
    mix = jnp.dot(acc.astype(BF16), wout_ref[...], preferred_element_type=F32)
    o_ref[...] = x_ref[...] + gx_ref[...] * _rms(mix, g_ref[...])


def _merge(ya, yb, yc, gt, x2, gate_x, g1, wbr_bf, wout_bf, tm):
    t, d = x2.shape
    nt = t // tm
    nb = gate_x.shape[0]
    tok = lambda w: pl.BlockSpec((tm, w), lambda i: (i, 0))
    modspec = pl.BlockSpec((None, 1, d), lambda i: (i // (nt // nb), 0, 0))
    vmem = (N_BRANCH * BRANCH_W * d + d * d) * 2 + 2 * tm * (3 * BRANCH_W + 3 * d) * 2 \
        + 4 * tm * d * 4 + 6 * tm * d * 4 + (8 << 20)
    return pl.pallas_call(
        _merge_kernel,
        grid=(nt,),
        in_specs=[tok(BRANCH_W), tok(BRANCH_W), tok(BRANCH_W), tok(N_BRANCH * d), tok(d), modspec,
                  _resident((1, d)), _resident((N_BRANCH, BRANCH_W, d)), _resident((d, d))],
        out_specs=tok(d),
        out_shape=jax.ShapeDtypeStruct((t, d), F32),
        compiler_params=_params(("parallel",), vmem),
        name="gated_merge",
    )(ya, yb, yc, gt, x2, gate_x, g1, wbr_bf, wout_bf)


def _ffn_kernel(x_ref, sh_ref, sc_ref, gx_ref, g2_ref, g3_ref, w1_ref, w2_ref, o_ref, *, n_chunks):
    x = x_ref[...]
    h = _rms(x, g2_ref[...])
    h = (h * (1.0 + sc_ref[...]) + sh_ref[...]).astype(BF16)
    hc = FFN_HIDDEN // n_chunks
    f = None
    for c in range(n_chunks):
        gate = jnp.dot(h, w1_ref[:, c * hc:(c + 1) * hc], preferred_element_type=F32)
        up = jnp.dot(h, w1_ref[:, FFN_HIDDEN + c * hc:FFN_HIDDEN + (c + 1) * hc],
                     preferred_element_type=F32)
        act = (gate * jax.nn.sigmoid(gate) * up).astype(BF16)
        part = jnp.dot(act, w2_ref[c * hc:(c + 1) * hc, :], preferred_element_type=F32)
        f = part if f is None else f + part
    o_ref[...] = x + gx_ref[...] * _rms(f, g3_ref[...])


def _ffn(x2, shift, scale, gate_x, g2, g3, w1_bf, w2_bf, tm):
    t, d = x2.shape
    nt = t // tm
    nb = shift.shape[0]
    tok = pl.BlockSpec((tm, d), lambda i: (i, 0))
    modspec = pl.BlockSpec((None, 1, d), lambda i: (i // (nt // nb), 0, 0))
    n_chunks = 2
    vmem = 3 * d * FFN_HIDDEN * 2 + 4 * tm * d * 4 + 3 * tm * (FFN_HIDDEN // n_chunks) * 4 \
        + 4 * tm * d * 4 + (8 << 20)
    return pl.pallas_call(
        functools.partial(_ffn_kernel, n_chunks=n_chunks),
        grid=(nt,),
        in_specs=[tok, modspec, modspec, modspec, _resident((1, d)), _resident((1, d)),
                  _resident((d, 2 * FFN_HIDDEN)), _resident((FFN_HIDDEN, d))],
        out_specs=tok,
        out_shape=jax.ShapeDtypeStruct((t, d), F32),
        compiler_params=_params(("parallel",), vmem),
        name="swiglu_ffn",
    )(x2, shift, scale, gate_x, g2, g3, w1_bf, w2_bf)


def _rope_tables(n_tokens):
    rd = HEAD_DIM // 4
    t = jnp.arange(n_tokens)
    pos = jnp.stack([t // GRID_W, t % GRID_W], axis=-1).astype(F32)
    inv = 1.0 / (ROPE_THETA ** (jnp.arange(rd, dtype=F32) * 2.0 / (HEAD_DIM // 2)))
    ang = pos[:, :, None] * inv
    cos = jnp.cos(ang)[:, :, None, :]
    sin = jnp.sin(ang)[:, :, None, :]
    zeros = jnp.zeros_like(sin)
    cos64 = jnp.concatenate([cos, cos], axis=2).reshape(n_tokens, HEAD_DIM)
    hi64 = jnp.concatenate([zeros, sin], axis=2).reshape(n_tokens, HEAD_DIM)
    lo64 = jnp.concatenate([-sin, zeros], axis=2).reshape(n_tokens, HEAD_DIM)
    rep = LANES // HEAD_DIM
    return jnp.tile(cos64, (1, rep)), jnp.tile(hi64, (1, rep)), jnp.tile(lo64, (1, rep))


def _identity_tables(rows):
    z = jnp.zeros((rows, LANES), F32)
    return jnp.ones((rows, LANES), F32), z, z


def _gate_weights(lru_w, lru_b):
    eye = jnp.eye(C_BLOCKS, dtype=lru_w.dtype)
    dense = jnp.einsum("dgncf,nm->ncdgmf", lru_w, eye).reshape(C_WIDTH, 4 * C_WIDTH)
    return dense.astype(BF16), lru_b.astype(F32).reshape(1, 4 * C_WIDTH)


def kernel(x, c, ctx, c_ctx, w_mod, b_mod, norm_g, w_in, attn_sink, diff_lambda, diff_subln, conv_w,
           conv_b, lru_w, lru_b, lru_lambda, w_branch, w_out, w_ffn_in, w_ffn_out):
    bsz, n_tok, d = x.shape
    n_ctx = ctx.shape[1]
    depth = w_mod.shape[0]
    tm = min(512, n_tok)
    tmc = min(256, n_ctx)

    mod_rows = 2 * SUBLANES
    c_rows = jnp.zeros((mod_rows, d), F32).at[:bsz].set(c).at[bsz].set(c_ctx)
    mods = _modulation(c_rows, w_mod, b_mod)

    cos, sin_hi, sin_lo = _rope_tables(n_tok)
    cos_c, sin_hi_c, sin_lo_c = _identity_tables(tmc)

    x2 = x.reshape(bsz * n_tok, d)
    cx2 = ctx.reshape(bsz * n_ctx, d)
    zero_state = jnp.zeros((bsz, 1, C_WIDTH), F32)

    for l in range(depth):
        need_ctx = l < depth - 1
        lam_init = 0.8 - 0.6 * math.exp(-0.3 * l)
        mx = [mods[l, :bsz, k * d:(k + 1) * d].reshape(bsz, 1, d) for k in range(6)]
        mc = [mods[l, bsz:bsz + 1, k * d:(k + 1) * d].reshape(1, 1, d) for k in range(6)]
        g = [norm_g[l, k].reshape(1, d).astype(F32) for k in range(4)]
        w_in_bf = w_in[l].astype(BF16)
        wg_bf, bg = _gate_weights(lru_w[l], lru_b[l])
        cw = conv_w[l].astype(F32)
        cb = conv_b[l].astype(F32).reshape(1, C_WIDTH)
        lam = lru_lambda[l].astype(F32)
        wbr_bf = w_branch[l].astype(BF16)
        wout_bf = w_out[l].astype(BF16)
        w1_bf = w_ffn_in[l].astype(BF16)
        w2_bf = w_ffn_out[l].astype(BF16)

        (qa_c, ka_c, va_c, qb_c, kb_c, vb_c, cu_c, cg_c, gt_c) = _in_proj(
            cx2, mc[0], mc[1], g[0], cos_c, sin_hi_c, sin_lo_c, w_in_bf, tmc)
        yc_c, hf_c, hb_c = _bidir_lru(cu_c, cg_c, zero_state, zero_state, cw, cb, wg_bf, bg, lam,
                                      bsz, n_ctx)

        (qa, ka, va, qb, kb, vb, cu, cg, gt) = _in_proj(
            x2, mx[0], mx[1], g[0], cos, sin_hi, sin_lo, w_in_bf, tm)
        ya = _window_attention(qa, ka, va, ka_c, va_c, attn_sink[l], bsz, n_tok, n_ctx)
        yb = _diff_attention(qb, kb, vb, kb_c, vb_c, diff_lambda[l], diff_subln[l], lam_init,
                             bsz, n_tok, n_ctx)
        yc, _, _ = _bidir_lru(cu, cg, hf_c, hb_c, cw, cb, wg_bf, bg, lam, bsz, n_tok)
        x2 = _merge(ya, yb, yc, gt, x2, mx[2], g[1], wbr_bf, wout_bf, tm)
        x2 = _ffn(x2, mx[3], mx[4], mx[5], g[2], g[3], w1_bf, w2_bf, tm)

        if need_ctx:
            ya_c = _context_attention(qa_c, ka_c, va_c, attn_sink[l], bsz, n_ctx)
            yb_c = _diff_attention(qb_c, None, None, kb_c, vb_c, diff_lambda[l], diff_subln[l],
                                   lam_init, bsz, 0, n_ctx)
            cx2 = _merge(ya_c, yb_c, yc_c, gt_c, cx2, mc[2], g[1], wbr_bf, wout_bf, tmc)
            cx2 = _ffn(cx2, mc[3], mc[4], mc[5], g[2], g[3], w1_bf, w2_bf, tmc)

    return x2.reshape(bsz, n_tok, d)
```

```python
import functools
import math

import jax
import jax.numpy as jnp
from jax import lax
from jax.experimental import pallas as pl
from jax.experimental.pallas import tpu as pltpu

F32 = jnp.float32
BF16 = jnp.bfloat16

D_MODEL = 1024
GRID_W = 64
HEAD_DIM = 64
ROPE_THETA = 10000.0
EPS = 1e-6
A_HEADS = 8
A_KV_HEADS = 2
A_GROUPS = A_HEADS // A_KV_HEADS
WINDOW = 128
B_HEADS = 4
B_VDIM = 2 * HEAD_DIM
C_WIDTH = 512
C_BLOCKS = 8
C_BW = C_WIDTH // C_BLOCKS
CONV_W = 4
LRU_C = 8.0
N_BRANCH = 3
BRANCH_W = 512
FFN_HIDDEN = -(-8 * D_MODEL // (3 * 256)) * 256

O_AQ = 0
O_AK = O_AQ + A_HEADS * HEAD_DIM
O_AV = O_AK + A_KV_HEADS * HEAD_DIM
O_BQ = O_AV + A_KV_HEADS * HEAD_DIM
O_BK = O_BQ + B_HEADS * 2 * HEAD_DIM
O_BV = O_BK + B_HEADS * 2 * HEAD_DIM
O_CU = O_BV + B_HEADS * B_VDIM
O_CG = O_CU + C_WIDTH
O_GT = O_CG + C_WIDTH
IN_WIDTH = O_GT + N_BRANCH * D_MODEL

LANES = 128
SUBLANES = 8
BF16_ROWS = 16
VMEM_CAP_BYTES = 60 * 1024 * 1024
NEG_BIG = -1e30
Q_BLOCK = 128
DIFF_TQ = 512
DIFF_TK = 256
LRU_ROWS = 256
LRU_PAD = 8
LOG2E = math.log2(math.e)
Q_SCALE = HEAD_DIM ** -0.5 * LOG2E


def _params(semantics, vmem_bytes):
    return pltpu.CompilerParams(dimension_semantics=semantics,
                                vmem_limit_bytes=min(int(vmem_bytes), VMEM_CAP_BYTES))


def _resident(shape):
    nd = len(shape)
    return pl.BlockSpec(shape, lambda *_: (0,) * nd, pipeline_mode=pl.Buffered(1))


def _rms(x, g):
    return x * lax.rsqrt(jnp.mean(x * x, axis=-1, keepdims=True) + EPS) * g


def _mod_kernel(c_ref, w_ref, b_ref, o_ref):
    c = c_ref[...]
    s = c * jax.nn.sigmoid(c)
    o_ref[...] = jnp.dot(s, w_ref[...], preferred_element_type=F32,
                         precision=lax.Precision.HIGHEST) + b_ref[...]


def _modulation(c_rows, w_mod, b_mod):
    depth, d, n = w_mod.shape
    rows = c_rows.shape[0]
    tn = 1536
    return pl.pallas_call(
        _mod_kernel,
        grid=(depth, n // tn),
        in_specs=[pl.BlockSpec((rows, d), lambda l, j: (0, 0)),
                  pl.BlockSpec((None, d, tn), lambda l, j: (l, 0, j)),
                  pl.BlockSpec((None, 1, tn), lambda l, j: (l, 0, j))],
        out_specs=pl.BlockSpec((None, rows, tn), lambda l, j: (l, 0, j)),
        out_shape=jax.ShapeDtypeStruct((depth, rows, n), F32),
        compiler_params=_params(("parallel", "parallel"), 32 << 20),
        name="modulation",
    )(c_rows, w_mod, b_mod.reshape(depth, 1, n))


def _in_proj_kernel(x_ref, sh_ref, sc_ref, g_ref, cos_ref, sa_ref, sb_ref, w_ref,
                    qa_ref, ka_ref, va_ref, qb_ref, kb_ref, vb_ref, cu_ref, cg_ref, gt_ref):
    h = _rms(x_ref[...], g_ref[...])
    h = (h * (1.0 + sc_ref[...]) + sh_ref[...]).astype(BF16)

    def proj(lo, width):
        return jnp.dot(h, w_ref[:, lo:lo + width], preferred_element_type=F32)

    cos = cos_ref[...]
    sin_hi = sa_ref[...]
    sin_lo = sb_ref[...]

    def rope(a):
        outs = []
        for k in range(a.shape[1] // LANES):
            blk = a[:, k * LANES:(k + 1) * LANES]
            outs.append(blk * cos + pltpu.roll(blk, HEAD_DIM // 4, 1) * sin_hi
                        + pltpu.roll(blk, LANES - HEAD_DIM // 4, 1) * sin_lo)
        return jnp.concatenate(outs, axis=1)

    qa_ref[...] = (rope(proj(O_AQ, O_AK - O_AQ)) * Q_SCALE).T.astype(BF16)
    ka_ref[...] = rope(proj(O_AK, O_AV - O_AK)).astype(BF16)
    va_ref[...] = proj(O_AV, O_BQ - O_AV).T.astype(BF16)
    qb_ref[...] = (rope(proj(O_BQ, O_BK - O_BQ)) * Q_SCALE).T.astype(BF16)
    kb_ref[...] = rope(proj(O_BK, O_BV - O_BK)).astype(BF16)
    vb_ref[...] = proj(O_BV, O_CU - O_BV).T.astype(BF16)
    cu_ref[...] = proj(O_CU, C_WIDTH).astype(BF16)
    cg_ref[...] = proj(O_CG, C_WIDTH).astype(BF16)
    for k in range(N_BRANCH):
        gt_ref[:, k * D_MODEL:(k + 1) * D_MODEL] = proj(O_GT + k * D_MODEL, D_MODEL).astype(BF16)


def _in_proj(x2, shift, scale, g0, cos, sin_hi, sin_lo, w_bf, tm):
    t, d = x2.shape
    nt = t // tm
    nb = shift.shape[0]
    npos = cos.shape[0] // tm
    tok = lambda w: pl.BlockSpec((tm, w), lambda i: (i, 0))
    feat = lambda w: pl.BlockSpec((w, tm), lambda i: (0, i))
    modspec = pl.BlockSpec((None, 1, d), lambda i: (i // (nt // nb), 0, 0))
    tabspec = pl.BlockSpec((tm, LANES), lambda i: (i % npos, 0))
    wq, wkv = A_HEADS * HEAD_DIM, A_KV_HEADS * HEAD_DIM
    wb = B_HEADS * B_VDIM
    out_shape = (
        jax.ShapeDtypeStruct((wq, t), BF16), jax.ShapeDtypeStruct((t, wkv), BF16),
        jax.ShapeDtypeStruct((wkv, t), BF16), jax.ShapeDtypeStruct((wb, t), BF16),
        jax.ShapeDtypeStruct((t, wb), BF16), jax.ShapeDtypeStruct((wb, t), BF16),
        jax.ShapeDtypeStruct((t, C_WIDTH), BF16), jax.ShapeDtypeStruct((t, C_WIDTH), BF16),
        jax.ShapeDtypeStruct((t, N_BRANCH * D_MODEL), BF16))
    out_specs = (feat(wq), tok(wkv), feat(wkv), feat(wb), tok(wb), feat(wb),
                 tok(C_WIDTH), tok(C_WIDTH), tok(N_BRANCH * D_MODEL))
    vmem = d * IN_WIDTH * 2 + 2 * tm * d * 4 + 2 * tm * IN_WIDTH * 2 + 6 * tm * d * 4 + (8 << 20)
    return pl.pallas_call(
        _in_proj_kernel,
        grid=(nt,),
        in_specs=[tok(d), modspec, modspec, _resident((1, d)), tabspec, tabspec, tabspec,
                  _resident((d, IN_WIDTH))],
        out_specs=out_specs,
        out_shape=out_shape,
        compiler_params=_params(("parallel",), vmem),
        name="in_proj",
    )(x2, shift, scale, g0, cos, sin_hi, sin_lo, w_bf)


def _sink_attention(k_all, vt_all, qt, sink_row, bias, n_win):
    tq = qt.shape[1]
    zero = jnp.zeros((HEAD_DIM, tq), qt.dtype)
    outs = []
    scores = []
    for j in range(A_KV_HEADS):
        cols = []
        for g in range(A_GROUPS):
            hd = j * A_GROUPS + g
            qh = qt[hd * HEAD_DIM:(hd + 1) * HEAD_DIM, :]
            cols.append(jnp.concatenate([qh, zero] if j == 0 else [zero, qh], axis=0))
        qz = jnp.concatenate(cols, axis=1)
        scores.append(jnp.dot(k_all, qz, preferred_element_type=F32))
    for j in range(A_KV_HEADS):
        s = scores[j]
        sink = sink_row[:, j * A_GROUPS * tq:(j + 1) * A_GROUPS * tq] * LOG2E
        if n_win:
            sw = s[:n_win] + jnp.concatenate([bias] * A_GROUPS, axis=1)
            sc = s[n_win:]
            m = jnp.maximum(jnp.maximum(jnp.max(sw, axis=0, keepdims=True),
                                        jnp.max(sc, axis=0, keepdims=True)), sink)
            ew = jnp.exp2(sw - m)
            ec = jnp.exp2(sc - m)
            l = (jnp.sum(ew, axis=0, keepdims=True) + jnp.sum(ec, axis=0, keepdims=True)
                 + jnp.exp2(sink - m))
            p = jnp.concatenate([ew, ec], axis=0).astype(BF16)
        else:
            m = jnp.maximum(jnp.max(s, axis=0, keepdims=True), sink)
            e = jnp.exp2(s - m)
            l = jnp.sum(e, axis=0, keepdims=True) + jnp.exp2(sink - m)
            p = e.astype(BF16)
        o = jnp.dot(vt_all[j * HEAD_DIM:(j + 1) * HEAD_DIM, :], p, preferred_element_type=F32)
        o = o * (1.0 / l)
        for g in range(A_GROUPS):
            outs.append(o[:, g * tq:(g + 1) * tq])
    return jnp.concatenate(outs, axis=0)


def _win_attn_kernel(q_ref, kp_ref, kc_ref, kn_ref, vp_ref, vc_ref, vn_ref, kx_ref, vx_ref,
                     sink_ref, o_ref, *, n_tok):
    n = pl.program_id(1)
    tq = q_ref.shape[1]
    k_all = jnp.concatenate([kp_ref[...], kc_ref[...], kn_ref[...], kx_ref[...]], axis=0)
    vt_all = jnp.concatenate([vp_ref[...], vc_ref[...], vn_ref[...], vx_ref[...]], axis=1)
    n_win = 3 * tq
    r = lax.broadcasted_iota(jnp.int32, (n_win, tq), 0)
    c = lax.broadcasted_iota(jnp.int32, (n_win, tq), 1)
    rel = r - tq - c
    kpos = (n - 1) * tq + r
    bias = (jnp.where(jnp.abs(rel) <= WINDOW, 0.0, NEG_BIG) + jnp.where(kpos >= 0, 0.0, NEG_BIG)
            + jnp.where(kpos < n_tok, 0.0, NEG_BIG)).astype(F32)
    yt = _sink_attention(k_all, vt_all, q_ref[...], sink_ref[...], bias, n_win)
    o_ref[...] = yt.T.astype(o_ref.dtype)


def _ctx_attn_kernel(q_ref, kx_ref, vx_ref, sink_ref, o_ref):
    yt = _sink_attention(kx_ref[...], vx_ref[...], q_ref[...], sink_ref[...], None, 0)
    o_ref[...] = yt.T.astype(o_ref.dtype)


def _sink_row(sink, tq):
    return jnp.repeat(sink.astype(F32), tq).reshape(1, A_HEADS * tq)


def _window_attention(qt, k, vt, kx, vxt, sink, bsz, n_tok, n_ctx):
    tq = Q_BLOCK
    nq = n_tok // tq
    wq, wkv = A_HEADS * HEAD_DIM, A_KV_HEADS * HEAD_DIM
    prev = lambda b, n: b * nq + jnp.maximum(n - 1, 0)
    cur = lambda b, n: b * nq + n
    nxt = lambda b, n: b * nq + jnp.minimum(n + 1, nq - 1)
    kspec = lambda f: pl.BlockSpec((tq, wkv), lambda b, n: (f(b, n), 0))
    vspec = lambda f: pl.BlockSpec((wkv, tq), lambda b, n: (0, f(b, n)))
    return pl.pallas_call(
        functools.partial(_win_attn_kernel, n_tok=n_tok),
        grid=(bsz, nq),
        in_specs=[pl.BlockSpec((wq, tq), lambda b, n: (0, cur(b, n))),
                  kspec(prev), kspec(cur), kspec(nxt), vspec(prev), vspec(cur), vspec(nxt),
                  pl.BlockSpec((n_ctx, wkv), lambda b, n: (b, 0)),
                  pl.BlockSpec((wkv, n_ctx), lambda b, n: (0, b)),
                  pl.BlockSpec((1, A_HEADS * tq), lambda b, n: (0, 0))],
        out_specs=pl.BlockSpec((tq, wq), lambda b, n: (cur(b, n), 0)),
        out_shape=jax.ShapeDtypeStruct((bsz * n_tok, wq), BF16),
        compiler_params=_params(("parallel", "parallel"), 32 << 20),
        name="window_attention",
    )(qt, k, k, k, vt, vt, vt, kx, vxt, _sink_row(sink, tq))


def _context_attention(qt, kx, vxt, sink, bsz, n_ctx):
    tq = Q_BLOCK
    nq = n_ctx // tq
    wq, wkv = A_HEADS * HEAD_DIM, A_KV_HEADS * HEAD_DIM
    return pl.pallas_call(
        _ctx_attn_kernel,
        grid=(bsz, nq),
        in_specs=[pl.BlockSpec((wq, tq), lambda b, n: (0, b * nq + n)),
                  pl.BlockSpec((n_ctx, wkv), lambda b, n: (b, 0)),
                  pl.BlockSpec((wkv, n_ctx), lambda b, n: (0, b)),
                  pl.BlockSpec((1, A_HEADS * tq), lambda b, n: (0, 0))],
        out_specs=pl.BlockSpec((tq, wq), lambda b, n: (b * nq + n, 0)),
        out_shape=jax.ShapeDtypeStruct((bsz * n_ctx, wq), BF16),
        compiler_params=_params(("parallel", "parallel"), 32 << 20),
        name="context_attention",
    )(qt, kx, vxt, _sink_row(sink, tq))


def _diff_attn_kernel(*refs, n_lat_chunks, tk, lam_init):
    if n_lat_chunks:
        q_ref, k_ref, v_ref, kx_ref, vx_ref, dl_ref, g_ref, o_ref = refs
    else:
        q_ref, kx_ref, vx_ref, dl_ref, g_ref, o_ref = refs
    qt = q_ref[...]
    tq = qt.shape[1]
    row = lax.broadcasted_iota(jnp.int32, qt.shape, 0)
    zero = jnp.zeros_like(qt)
    qz = jnp.concatenate([jnp.where(row < HEAD_DIM, qt, zero),
                          jnp.where(row >= HEAD_DIM, qt, zero)], axis=1)

    keys = [lambda ci=ci: k_ref[ci * tk:(ci + 1) * tk, :] for ci in range(n_lat_chunks)]
    vals = [lambda ci=ci: v_ref[:, ci * tk:(ci + 1) * tk] for ci in range(n_lat_chunks)]
    keys.append(lambda: kx_ref[...])
    vals.append(lambda: vx_ref[...])
    n = len(keys)

    def scores(i):
        s = jnp.dot(keys[i](), qz, preferred_element_type=F32)
        return s, jnp.max(s, axis=0, keepdims=True)

    def values(i):
        v = vals[i]()
        return jnp.concatenate([v, jnp.ones((BF16_ROWS, v.shape[1]), BF16)], axis=0)

    m = jnp.full((1, 2 * tq), NEG_BIG, F32)
    acc = jnp.zeros((B_VDIM + BF16_ROWS, 2 * tq), F32)
    s_next, smax_next = scores(0)
    e_prev = alpha_prev = None
    for c in range(n):
        s, smax = s_next, smax_next
        if c + 1 < n:
            s_next, smax_next = scores(c + 1)
        if c:
            acc = alpha_prev * acc + jnp.dot(values(c - 1), e_prev, preferred_element_type=F32)
        m_new = jnp.maximum(m, smax)
        alpha_prev = jnp.exp2(m - m_new)
        e_prev = jnp.exp2(s - m_new).astype(BF16)
        m = m_new
    acc = alpha_prev * acc + jnp.dot(values(n - 1), e_prev, preferred_element_type=F32)

    dl = dl_ref[...]
    lam = (jnp.exp(jnp.sum(dl[0:1] * dl[1:2], axis=1, keepdims=True))
           - jnp.exp(jnp.sum(dl[2:3] * dl[3:4], axis=1, keepdims=True)) + lam_init)
    o = acc[:B_VDIM] * (1.0 / acc[B_VDIM:B_VDIM + 1])
    y = (o[:, :tq] - lam * o[:, tq:]).T
    o_ref[...] = (_rms(y, g_ref[...]) * (1.0 - lam_init)).astype(o_ref.dtype)


def _diff_attention(qt, k, vt, kx, vxt, diff_lambda, subln, lam_init, bsz, n_tok, n_ctx):
    nq_tok = n_tok if n_tok else n_ctx
    tq = min(DIFF_TQ, nq_tok)
    nqt = nq_tok // tq
    tk = min(DIFF_TK, n_tok) if n_tok else 0
    hw = B_VDIM
    in_specs = [pl.BlockSpec((hw, tq), lambda b, h, i: (h, b * nqt + i))]
    args = [qt]
    if n_tok:
        in_specs += [pl.BlockSpec((n_tok, hw), lambda b, h, i: (b, h)),
                     pl.BlockSpec((hw, n_tok), lambda b, h, i: (h, b))]
        args += [k, vt]
    in_specs += [pl.BlockSpec((n_ctx, hw), lambda b, h, i: (b, h)),
                 pl.BlockSpec((hw, n_ctx), lambda b, h, i: (h, b)),
                 pl.BlockSpec((4, HEAD_DIM), lambda b, h, i: (0, 0)),
                 pl.BlockSpec((1, hw), lambda b, h, i: (0, 0))]
    args += [kx, vxt, diff_lambda.astype(F32), subln.astype(F32).reshape(1, hw)]
    return pl.pallas_call(
        functools.partial(_diff_attn_kernel, n_lat_chunks=(n_tok // tk if n_tok else 0), tk=tk,
                          lam_init=lam_init),
        grid=(bsz, B_HEADS, nqt),
        in_specs=in_specs,
        out_specs=pl.BlockSpec((tq, hw), lambda b, h, i: (b * nqt + i, h)),
        out_shape=jax.ShapeDtypeStruct((bsz * nq_tok, B_HEADS * hw), BF16),
        compiler_params=_params(("parallel", "parallel", "parallel"), 40 << 20),
        name="diff_attention",
    )(*args)


def _scan_segments(a, b, h_in, reverse):
    rows, w = a.shape
    seg = rows // SUBLANES
    order = range(seg - 1, -1, -1) if reverse else range(seg)
    h_loc = [None] * seg
    a_cum = [None] * seg
    h = prod = None
    for i in order:
        ai = a[i * SUBLANES:(i + 1) * SUBLANES]
        bi = b[i * SUBLANES:(i + 1) * SUBLANES]
        h = bi if h is None else ai * h + bi
        prod = ai if prod is None else ai * prod
        h_loc[i] = h
        a_cum[i] = prod
    sub = lax.broadcasted_iota(jnp.int32, (SUBLANES, w), 0)
    e, p = h, prod
    for d in (1, 2, 4):
        keep = (sub < SUBLANES - d) if reverse else (sub >= d)
        shift = SUBLANES - d if reverse else d
        p_s = jnp.where(keep, pltpu.roll(p, shift, 0), 1.0)
        e_s = jnp.where(keep, pltpu.roll(e, shift, 0), 0.0)
        e = e + p * e_s
        p = p * p_s
    seg_out = e + p * h_in
    if reverse:
        seg_in = jnp.where(sub < SUBLANES - 1, pltpu.roll(seg_out, SUBLANES - 1, 0), h_in)
        h_out = seg_out[0:1]
    else:
        seg_in = jnp.where(sub >= 1, pltpu.roll(seg_out, 1, 0), h_in)
        h_out = seg_out[SUBLANES - 1:SUBLANES]
    return [h_loc[i] + a_cum[i] * seg_in for i in range(seg)], h_out


def _lru_kernel(cu_ref, cg_ref, h0f_ref, h0b_ref, cw_ref, cb_ref, wg_ref, bg_ref, lam_ref,
                y_ref, hf_ref, hb_ref, xpad_ref, hs_ref, *, rows):
    n = cu_ref.shape[0]
    nchunks = n // rows
    w = C_WIDTH
    seg = rows // SUBLANES
    slabs = w // LANES

    def load(ref, idx):
        return jnp.concatenate([ref[j, idx, :] for j in range(slabs)], axis=1)

    def store(ref, idx, val):
        for j in range(slabs):
            ref[j, idx, :] = val[:, j * LANES:(j + 1) * LANES]

    store(xpad_ref, pl.ds(0, LRU_PAD), jnp.zeros((LRU_PAD, w), F32))
    store(xpad_ref, pl.ds(LRU_PAD + n, LRU_PAD), jnp.zeros((LRU_PAD, w), F32))

    def fill(ci, carry):
        s = pl.multiple_of(ci * rows, rows)
        store(xpad_ref, pl.ds(s + LRU_PAD, rows), cu_ref[pl.ds(s, rows), :].astype(F32))
        return carry
    lax.fori_loop(0, nchunks, fill, 0)

    lam = lam_ref[...]
    decay = LRU_C * (jnp.maximum(-lam, 0.0) + jnp.log1p(jnp.exp(-jnp.abs(lam))))
    cw = cw_ref[...]
    cb = cb_ref[...]

    def coeffs(s, d):
        first = LRU_PAD - CONV_W // 2
        taps = [load(xpad_ref, pl.ds(s + (first + q), SUBLANES, stride=seg))
                for q in range(seg + CONV_W - 1)]
        groups = []
        for i in range(seg):
            acc = cb
            for k in range(CONV_W):
                acc = acc + cw[k:k + 1] * taps[i + k]
            groups.append(acc)
        u = jnp.concatenate(groups, axis=0)
        g = jnp.dot(u.astype(BF16), wg_ref[:, d * 2 * w:(d + 1) * 2 * w],
                    preferred_element_type=F32) + bg_ref[:, d * 2 * w:(d + 1) * 2 * w]
        r = jax.nn.sigmoid(g[:, :w])
        i_gate = jax.nn.sigmoid(g[:, w:])
        z = decay[d:d + 1] * r
        a = jnp.exp(-z)
        v = (1.0 + a * a) * jnp.tanh(z)
        root = jnp.where(v > 0.0, v * lax.rsqrt(v), 0.0)
        return a, root * i_gate * u

    r_i = lax.broadcasted_iota(jnp.int32, (rows, rows), 0)
    c_i = lax.broadcasted_iota(jnp.int32, (rows, rows), 1)
    to_tokens = jnp.where(c_i == (r_i % seg) * SUBLANES + r_i // seg, 1.0, 0.0).astype(BF16)
    to_segments = jnp.where(r_i == (c_i % seg) * SUBLANES + c_i // seg, 1.0, 0.0).astype(BF16)

    def bwd(ci, h):
        s = pl.multiple_of((nchunks - 1 - ci) * rows, rows)
        a, bx = coeffs(s, 1)
        hs, h = _scan_segments(a, bx, h, True)
        hs_ref[pl.ds(s, rows), :] = jnp.concatenate(hs, axis=0)
        return h
    hb_ref[...] = lax.fori_loop(0, nchunks, bwd, h0b_ref[...])

    def fwd(ci, h):
        s = pl.multiple_of(ci * rows, rows)
        a, bx = coeffs(s, 0)
        hs, h = _scan_segments(a, bx, h, False)
        gate = jax.nn.gelu(jnp.dot(to_segments, cg_ref[pl.ds(s, rows), :], preferred_element_type=F32))
        y = ((jnp.concatenate(hs, axis=0) + hs_ref[pl.ds(s, rows), :]) * gate).astype(BF16)
        y_ref[pl.ds(s, rows), :] = jnp.dot(to_tokens, y, preferred_element_type=F32).astype(y_ref.dtype)
        return h
    hf_ref[...] = lax.fori_loop(0, nchunks, fwd, h0f_ref[...])


def _bidir_lru(cu, cg, h0f, h0b, conv_w, conv_b, wg_bf, bg, lam, bsz, n):
    w = C_WIDTH
    rows = min(LRU_ROWS, n)
    seq = pl.BlockSpec((n, w), lambda b: (b, 0))
    st = pl.BlockSpec((None, 1, w), lambda b: (b, 0, 0))
    vmem = 6 * n * w * 2 + n * w * 4 + (n + 2 * LRU_PAD) * w * 4 + (16 << 20)
    return pl.pallas_call(
        functools.partial(_lru_kernel, rows=rows),
        grid=(bsz,),
        in_specs=[seq, seq, st, st, _resident((CONV_W, w)), _resident((1, w)),
                  _resident((w, 4 * w)), _resident((1, 4 * w)), _resident((2, w))],
        out_specs=(seq, st, st),
        out_shape=(jax.ShapeDtypeStruct((bsz * n, w), BF16),
                   jax.ShapeDtypeStruct((bsz, 1, w), F32), jax.ShapeDtypeStruct((bsz, 1, w), F32)),
        scratch_shapes=[pltpu.VMEM((w // LANES, n + 2 * LRU_PAD, LANES), F32),
                        pltpu.VMEM((n, w), F32)],
        compiler_params=_params(("parallel",), vmem),
        name="bidir_lru",
    )(cu, cg, h0f, h0b, conv_w, conv_b, wg_bf, bg, lam)


def _merge_kernel(ya_ref, yb_ref, yc_ref, gt_ref, x_ref, gx_ref, g_ref, wbr_ref, wout_ref, o_ref):
    acc = None
    for n, y_ref in enumerate((ya_ref, yb_ref, yc_ref)):
        z = jnp.dot(y_ref[...], wbr_ref[n], preferred_element_type=F32)
        gate = jax.nn.sigmoid(gt_ref[:, n * D_MODEL:(n + 1) * D_MODEL].astype(F32))
        acc = gate * z if acc is None else acc + gate * z
    mix = jnp.dot(acc.astype(BF16), wout_ref[...], preferred_element_type=F32)
    o_ref[...] = x_ref[...] + gx_ref[...] * _rms(mix, g_ref[...])


def _merge(ya, yb, yc, gt, x2, gate_x, g1, wbr_bf, wout_bf, tm):
    t, d = x2.shape
    nt = t // tm
    nb = gate_x.shape[0]
    tok = lambda w: pl.BlockSpec((tm, w), lambda i: (i, 0))
    modspec = pl.BlockSpec((None, 1, d), lambda i: (i // (nt // nb), 0, 0))
    vmem = (N_BRANCH * BRANCH_W * d + d * d) * 2 + 2 * tm * (3 * BRANCH_W + 3 * d) * 2 \
        + 4 * tm * d * 4 + 6 * tm * d * 4 + (8 << 20)
    return pl.pallas_call(
        _merge_kernel,
        grid=(nt,),
        in_specs=[tok(BRANCH_W), tok(BRANCH_W), tok(BRANCH_W), tok(N_BRANCH * d), tok(d), modspec,
                  _resident((1, d)), _resident((N_BRANCH, BRANCH_W, d)), _resident((d, d))],
        out_specs=tok(d),
        out_shape=jax.ShapeDtypeStruct((t, d), F32),
        compiler_params=_params(("parallel",), vmem),
        name="gated_merge",
    )(ya, yb, yc, gt, x2, gate_x, g1, wbr_bf, wout_bf)


def _ffn_kernel(x_ref, sh_ref, sc_ref, gx_ref, g2_ref, g3_ref, w1_ref, w2_ref, o_ref, *, n_chunks):
    x = x_ref[...]
    h = _rms(x, g2_ref[...])
    h = (h * (1.0 + sc_ref[...]) + sh_ref[...]).astype(BF16)
    hc = FFN_HIDDEN // n_chunks
    f = None
    for c in range(n_chunks):
        gate = jnp.dot(h, w1_ref[:, c * hc:(c + 1) * hc], preferred_element_type=F32)
        up = jnp.dot(h, w1_ref[:, FFN_HIDDEN + c * hc:FFN_HIDDEN + (c + 1) * hc],
                     preferred_element_type=F32)
        act = (gate * jax.nn.sigmoid(gate) * up).astype(BF16)
        part = jnp.dot(act, w2_ref[c * hc:(c + 1) * hc, :], preferred_element_type=F32)
        f = part if f is None else f + part
    o_ref[...] = x + gx_ref[...] * _rms(f, g3_ref[...])


def _ffn(x2, shift, scale, gate_x, g2, g3, w1_bf, w2_bf, tm):
    t, d = x2.shape
    nt = t // tm
    nb = shift.shape[0]
    tok = pl.BlockSpec((tm, d), lambda i: (i, 0))
    modspec = pl.BlockSpec((None, 1, d), lambda i: (i // (nt // nb), 0, 0))
    n_chunks = 2
    vmem = 3 * d * FFN_HIDDEN * 2 + 4 * tm * d * 4 + 3 * tm * (FFN_HIDDEN // n_chunks) * 4 \
        + 4 * tm * d * 4 + (8 << 20)
    return pl.pallas_call(
        functools.partial(_ffn_kernel, n_chunks=n_chunks),
        grid=(nt,),
        in_specs=[tok, modspec, modspec, modspec, _resident((1, d)), _resident((1, d)),
                  _resident((d, 2 * FFN_HIDDEN)), _resident((FFN_HIDDEN, d))],
        out_specs=tok,
        out_shape=jax.ShapeDtypeStruct((t, d), F32),
        compiler_params=_params(("parallel",), vmem),
        name="swiglu_ffn",
    )(x2, shift, scale, gate_x, g2, g3, w1_bf, w2_bf)


def _rope_tables(n_tokens):
    rd = HEAD_DIM // 4
    t = jnp.arange(n_tokens)
    pos = jnp.stack([t // GRID_W, t % GRID_W], axis=-1).astype(F32)
    inv = 1.0 / (ROPE_THETA ** (jnp.arange(rd, dtype=F32) * 2.0 / (HEAD_DIM // 2)))
    ang = pos[:, :, None] * inv
    cos = jnp.cos(ang)[:, :, None, :]
    sin = jnp.sin(ang)[:, :, None, :]
    zeros = jnp.zeros_like(sin)
    cos64 = jnp.concatenate([cos, cos], axis=2).reshape(n_tokens, HEAD_DIM)
    hi64 = jnp.concatenate([zeros, sin], axis=2).reshape(n_tokens, HEAD_DIM)
    lo64 = jnp.concatenate([-sin, zeros], axis=2).reshape(n_tokens, HEAD_DIM)
    rep = LANES // HEAD_DIM
    return jnp.tile(cos64, (1, rep)), jnp.tile(hi64, (1, rep)), jnp.tile(lo64, (1, rep))


def _identity_tables(rows):
    z = jnp.zeros((rows, LANES), F32)
    return jnp.ones((rows, LANES), F32), z, z


def _gate_weights(lru_w, lru_b):
    eye = jnp.eye(C_BLOCKS, dtype=lru_w.dtype)
    dense = jnp.einsum("dgncf,nm->ncdgmf", lru_w, eye).reshape(C_WIDTH, 4 * C_WIDTH)
    return dense.astype(BF16), lru_b.astype(F32).reshape(1, 4 * C_WIDTH)


def kernel(x, c, ctx, c_ctx, w_mod, b_mod, norm_g, w_in, attn_sink, diff_lambda, diff_subln, conv_w,
           conv_b, lru_w, lru_b, lru_lambda, w_branch, w_out, w_ffn_in, w_ffn_out):
    bsz, n_tok, d = x.shape
    n_ctx = ctx.shape[1]
    depth = w_mod.shape[0]
    tm = min(512, n_tok)
    tmc = min(256, n_ctx)

    mod_rows = 2 * SUBLANES
    c_rows = jnp.zeros((mod_rows, d), F32).at[:bsz].set(c).at[bsz].set(c_ctx)
    mods = _modulation(c_rows, w_mod, b_mod)

    cos, sin_hi, sin_lo = _rope_tables(n_tok)
    cos_c, sin_hi_c, sin_lo_c = _identity_tables(tmc)

    x2 = x.reshape(bsz * n_tok, d)
    cx2 = ctx.reshape(bsz * n_ctx, d)
    zero_state = jnp.zeros((bsz, 1, C_WIDTH), F32)

    for l in range(depth):
        need_ctx = l < depth - 1
        lam_init = 0.8 - 0.6 * math.exp(-0.3 * l)
        mx = [mods[l, :bsz, k * d:(k + 1) * d].reshape(bsz, 1, d) for k in range(6)]
        mc = [mods[l, bsz:bsz + 1, k * d:(k + 1) * d].reshape(1, 1, d) for k in range(6)]
        g = [norm_g[l, k].reshape(1, d).astype(F32) for k in range(4)]
        w_in_bf = w_in[l].astype(BF16)
        wg_bf, bg = _gate_weights(lru_w[l], lru_b[l])
        cw = conv_w[l].astype(F32)
        cb = conv_b[l].astype(F32).reshape(1, C_WIDTH)
        lam = lru_lambda[l].astype(F32)
        wbr_bf = w_branch[l].astype(BF16)
        wout_bf = w_out[l].astype(BF16)
        w1_bf = w_ffn_in[l].astype(BF16)
        w2_bf = w_ffn_out[l].astype(BF16)

        (qa_c, ka_c, va_c, qb_c, kb_c, vb_c, cu_c, cg_c, gt_c) = _in_proj(
            cx2, mc[0], mc[1], g[0], cos_c, sin_hi_c, sin_lo_c, w_in_bf, tmc)
        yc_c, hf_c, hb_c = _bidir_lru(cu_c, cg_c, zero_state, zero_state, cw, cb, wg_bf, bg, lam,
                                      bsz, n_ctx)

        (qa, ka, va, qb, kb, vb, cu, cg, gt) = _in_proj(
            x2, mx[0], mx[1], g[0], cos, sin_hi, sin_lo, w_in_bf, tm)
        ya = _window_attention(qa, ka, va, ka_c, va_c, attn_sink[l], bsz, n_tok, n_ctx)
        yb = _diff_attention(qb, kb, vb, kb_c, vb_c, diff_lambda[l], diff_subln[l], lam_init,
                             bsz, n_tok, n_ctx)
        yc, _, _ = _bidir_lru(cu, cg, hf_c, hb_c, cw, cb, wg_bf, bg, lam, bsz, n_tok)
        x2 = _merge(ya, yb, yc, gt, x2, mx[2], g[1], wbr_bf, wout_bf, tm)
        x2 = _ffn(x2, mx[3], mx[4], mx[5], g[2], g[3], w1_bf, w2_bf, tm)

        if need_ctx:
            ya_c = _context_attention(qa_c, ka_c, va_c, attn_sink[l], bsz, n_ctx)
            yb_c = _diff_attention(qb_c, None, None, kb_c, vb_c, diff_lambda[l], diff_subln[l],
                                   lam_init, bsz, 0, n_ctx)
            cx2 = _merge(ya_c, yb_c, yc_c, gt_c, cx2, mc[2], g[1], wbr_bf, wout_bf, tmc)
            cx2 = _ffn(cx2, mc[3], mc[4], mc[5], g[2], g[3], w1_bf, w2_bf, tmc)

    return x2.reshape(bsz, n_tok, d)
```

```python
import functools
import math

import jax
import jax.numpy as jnp
from jax import lax
from jax.experimental import pallas as pl
from jax.experimental.pallas import tpu as pltpu

F32 = jnp.float32
BF16 = jnp.bfloat16

D_MODEL = 1024
GRID_W = 64
HEAD_DIM = 64
ROPE_THETA = 10000.0
EPS = 1e-6
A_HEADS = 8
A_KV_HEADS = 2
A_GROUPS = A_HEADS // A_KV_HEADS
WINDOW = 128
B_HEADS = 4
B_VDIM = 2 * HEAD_DIM
C_WIDTH = 512
C_BLOCKS = 8
C_BW = C_WIDTH // C_BLOCKS
CONV_W = 4
LRU_C = 8.0
N_BRANCH = 3
BRANCH_W = 512
FFN_HIDDEN = -(-8 * D_MODEL // (3 * 256)) * 256

O_AQ = 0
O_AK = O_AQ + A_HEADS * HEAD_DIM
O_AV = O_AK + A_KV_HEADS * HEAD_DIM
O_BQ = O_AV + A_KV_HEADS * HEAD_DIM
O_BK = O_BQ + B_HEADS * 2 * HEAD_DIM
O_BV = O_BK + B_HEADS * 2 * HEAD_DIM
O_CU = O_BV + B_HEADS * B_VDIM
O_CG = O_CU + C_WIDTH
O_GT = O_CG + C_WIDTH
IN_WIDTH = O_GT + N_BRANCH * D_MODEL

LANES = 128
SUBLANES = 8
BF16_ROWS = 16
VMEM_CAP_BYTES = 60 * 1024 * 1024
NEG_BIG = -1e30
Q_BLOCK = 128
WIN_Q_BLOCKS = 4
DIFF_TQ = 512
DIFF_TK = 512
DIFF_HEADS_PER_STEP = 2
LRU_ROWS = 256
LRU_PAD = 8
LOG2E = math.log2(math.e)
Q_SCALE = HEAD_DIM ** -0.5 * LOG2E


def _params(semantics, vmem_bytes):
    return pltpu.CompilerParams(dimension_semantics=semantics,
                                vmem_limit_bytes=min(int(vmem_bytes), VMEM_CAP_BYTES))


def _resident(shape):
    nd = len(shape)
    return pl.BlockSpec(shape, lambda *_: (0,) * nd, pipeline_mode=pl.Buffered(1))


def _rms(x, g):
    return x * lax.rsqrt(jnp.mean(x * x, axis=-1, keepdims=True) + EPS) * g


def _mod_kernel(c_ref, w_ref, b_ref, o_ref):
    c = c_ref[...]
    s = c * jax.nn.sigmoid(c)
    o_ref[...] = jnp.dot(s, w_ref[...], preferred_element_type=F32,
                         precision=lax.Precision.HIGHEST) + b_ref[...]


def _modulation(c_rows, w_mod, b_mod):
    depth, d, n = w_mod.shape
    rows = c_rows.shape[0]
    tn = 1536
    return pl.pallas_call(
        _mod_kernel,
        grid=(depth, n // tn),
        in_specs=[pl.BlockSpec((rows, d), lambda l, j: (0, 0)),
                  pl.BlockSpec((None, d, tn), lambda l, j: (l, 0, j)),
                  pl.BlockSpec((None, 1, tn), lambda l, j: (l, 0, j))],
        out_specs=pl.BlockSpec((None, rows, tn), lambda l, j: (l, 0, j)),
        out_shape=jax.ShapeDtypeStruct((depth, rows, n), F32),
        compiler_params=_params(("parallel", "parallel"), 32 << 20),
        name="modulation",
    )(c_rows, w_mod, b_mod.reshape(depth, 1, n))


def _in_proj_kernel(x_ref, sh_ref, sc_ref, g_ref, cos_ref, sa_ref, sb_ref, w_ref,
                    qa_ref, ka_ref, va_ref, qb_ref, kb_ref, vb_ref, cu_ref, cg_ref, gt_ref):
    h = _rms(x_ref[...], g_ref[...])
    h = (h * (1.0 + sc_ref[...]) + sh_ref[...]).astype(BF16)

    def proj(lo, width):
        return jnp.dot(h, w_ref[:, lo:lo + width], preferred_element_type=F32)

    cos = cos_ref[...]
    sin_hi = sa_ref[...]
    sin_lo = sb_ref[...]

    def rope(a):
        outs = []
        for k in range(a.shape[1] // LANES):
            blk = a[:, k * LANES:(k + 1) * LANES]
            outs.append(blk * cos + pltpu.roll(blk, HEAD_DIM // 4, 1) * sin_hi
                        + pltpu.roll(blk, LANES - HEAD_DIM // 4, 1) * sin_lo)
        return jnp.concatenate(outs, axis=1)

    qa_ref[...] = (rope(proj(O_AQ, O_AK - O_AQ)) * Q_SCALE).T.astype(BF16)
    ka_ref[...] = rope(proj(O_AK, O_AV - O_AK)).astype(BF16)
    va_ref[...] = proj(O_AV, O_BQ - O_AV).T.astype(BF16)
    qb_ref[...] = (rope(proj(O_BQ, O_BK - O_BQ)) * Q_SCALE).T.astype(BF16)
    kb_ref[...] = rope(proj(O_BK, O_BV - O_BK)).astype(BF16)
    vb_ref[...] = proj(O_BV, O_CU - O_BV).T.astype(BF16)
    cu_ref[...] = proj(O_CU, C_WIDTH).astype(BF16)
    cg_ref[...] = proj(O_CG, C_WIDTH).astype(BF16)
    for k in range(N_BRANCH):
        gt_ref[:, k * D_MODEL:(k + 1) * D_MODEL] = proj(O_GT + k * D_MODEL, D_MODEL).astype(BF16)


def _in_proj(x2, shift, scale, g0, cos, sin_hi, sin_lo, w_bf, tm):
    t, d = x2.shape
    nt = t // tm
    nb = shift.shape[0]
    npos = cos.shape[0] // tm
    tok = lambda w: pl.BlockSpec((tm, w), lambda i: (i, 0))
    feat = lambda w: pl.BlockSpec((w, tm), lambda i: (0, i))
    modspec = pl.BlockSpec((None, 1, d), lambda i: (i // (nt // nb), 0, 0))
    tabspec = pl.BlockSpec((tm, LANES), lambda i: (i % npos, 0))
    wq, wkv = A_HEADS * HEAD_DIM, A_KV_HEADS * HEAD_DIM
    wb = B_HEADS * B_VDIM
    out_shape = (
        jax.ShapeDtypeStruct((wq, t), BF16), jax.ShapeDtypeStruct((t, wkv), BF16),
        jax.ShapeDtypeStruct((wkv, t), BF16), jax.ShapeDtypeStruct((wb, t), BF16),
        jax.ShapeDtypeStruct((t, wb), BF16), jax.ShapeDtypeStruct((wb, t), BF16),
        jax.ShapeDtypeStruct((t, C_WIDTH), BF16), jax.ShapeDtypeStruct((t, C_WIDTH), BF16),
        jax.ShapeDtypeStruct((t, N_BRANCH * D_MODEL), BF16))
    out_specs = (feat(wq), tok(wkv), feat(wkv), feat(wb), tok(wb), feat(wb),
                 tok(C_WIDTH), tok(C_WIDTH), tok(N_BRANCH * D_MODEL))
    vmem = d * IN_WIDTH * 2 + 2 * tm * d * 4 + 2 * tm * IN_WIDTH * 2 + 6 * tm * d * 4 + (8 << 20)
    return pl.pallas_call(
        _in_proj_kernel,
        grid=(nt,),
        in_specs=[tok(d), modspec, modspec, _resident((1, d)), tabspec, tabspec, tabspec,
                  _resident((d, IN_WIDTH))],
        out_specs=out_specs,
        out_shape=out_shape,
        compiler_params=_params(("parallel",), vmem),
        name="in_proj",
    )(x2, shift, scale, g0, cos, sin_hi, sin_lo, w_bf)


def _sink_scores(k_all, qt):
    tq = qt.shape[1]
    zero = jnp.zeros((HEAD_DIM, tq), qt.dtype)
    scores = []
    for j in range(A_KV_HEADS):
        cols = []
        for g in range(A_GROUPS):
            hd = j * A_GROUPS + g
            qh = qt[hd * HEAD_DIM:(hd + 1) * HEAD_DIM, :]
            cols.append(jnp.concatenate([qh, zero] if j == 0 else [zero, qh], axis=0))
        qz = jnp.concatenate(cols, axis=1)
        scores.append(jnp.dot(k_all, qz, preferred_element_type=F32))
    return scores


def _sink_softmax_pv(scores, vt_all, sink_row, bias, n_win):
    tq = scores[0].shape[1] // A_GROUPS
    outs = []
    ones = jnp.ones((BF16_ROWS, vt_all.shape[1]), BF16)
    for j in range(A_KV_HEADS):
        s = scores[j]
        sink = sink_row[:, j * A_GROUPS * tq:(j + 1) * A_GROUPS * tq] * LOG2E
        if n_win:
            sw = s[:n_win] + jnp.concatenate([bias] * A_GROUPS, axis=1)
            sc = s[n_win:]
            m = jnp.maximum(jnp.maximum(jnp.max(sw, axis=0, keepdims=True),
                                        jnp.max(sc, axis=0, keepdims=True)), sink)
            p = jnp.concatenate([jnp.exp2(sw - m), jnp.exp2(sc - m)], axis=0).astype(BF16)
        else:
            m = jnp.maximum(jnp.max(s, axis=0, keepdims=True), sink)
            p = jnp.exp2(s - m).astype(BF16)
        vt_aug = jnp.concatenate([vt_all[j * HEAD_DIM:(j + 1) * HEAD_DIM, :], ones], axis=0)
        o = jnp.dot(vt_aug, p, preferred_element_type=F32)
        l = o[HEAD_DIM:HEAD_DIM + 1] + jnp.exp2(sink - m)
        o = o[:HEAD_DIM] * (1.0 / l)
        for g in range(A_GROUPS):
            outs.append(o[:, g * tq:(g + 1) * tq])
    return jnp.concatenate(outs, axis=0)


def _win_attn_kernel(q_ref, kp_ref, kc_ref, kn_ref, vp_ref, vc_ref, vn_ref, kx_ref, vx_ref,
                     sink_ref, o_ref, *, n_tok):
    tq = Q_BLOCK
    q_blocks = q_ref.shape[1] // tq
    first = pl.program_id(1) * q_blocks
    k_cat = jnp.concatenate([kp_ref[...], kc_ref[...], kn_ref[...]], axis=0)
    vt_cat = jnp.concatenate([vp_ref[...], vc_ref[...], vn_ref[...]], axis=1)
    kx = kx_ref[...]
    vx = vx_ref[...]
    n_win = 3 * tq
    r = lax.broadcasted_iota(jnp.int32, (n_win, tq), 0)
    c = lax.broadcasted_iota(jnp.int32, (n_win, tq), 1)
    band = jnp.where(jnp.abs(r - tq - c) <= WINDOW, 0.0, NEG_BIG).astype(F32)

    def scores(b):
        k_all = jnp.concatenate([k_cat[b * tq:(b + 3) * tq], kx], axis=0)
        return _sink_scores(k_all, q_ref[:, b * tq:(b + 1) * tq])

    s_next = scores(0)
    for b in range(q_blocks):
        s_cur = s_next
        if b + 1 < q_blocks:
            s_next = scores(b + 1)
        kpos = (first + b - 1) * tq + r
        bias = band + jnp.where(kpos >= 0, 0.0, NEG_BIG) + jnp.where(kpos < n_tok, 0.0, NEG_BIG)
        vt_all = jnp.concatenate([vt_cat[:, b * tq:(b + 3) * tq], vx], axis=1)
        yt = _sink_softmax_pv(s_cur, vt_all, sink_ref[...], bias, n_win)
        o_ref[b * tq:(b + 1) * tq, :] = yt.T.astype(o_ref.dtype)


def _ctx_attn_kernel(q_ref, kx_ref, vx_ref, sink_ref, o_ref):
    yt = _sink_softmax_pv(_sink_scores(kx_ref[...], q_ref[...]), vx_ref[...], sink_ref[...], None, 0)
    o_ref[...] = yt.T.astype(o_ref.dtype)


def _sink_row(sink, tq):
    return jnp.repeat(sink.astype(F32), tq).reshape(1, A_HEADS * tq)


def _window_attention(qt, k, vt, kx, vxt, sink, bsz, n_tok, n_ctx):
    tq = Q_BLOCK
    nq = n_tok // tq
    qb = min(WIN_Q_BLOCKS, nq)
    ns = nq // qb
    wq, wkv = A_HEADS * HEAD_DIM, A_KV_HEADS * HEAD_DIM
    prev = lambda b, n: b * nq + jnp.maximum(n * qb - 1, 0)
    cur = lambda b, n: b * ns + n
    nxt = lambda b, n: b * nq + jnp.minimum(n * qb + qb, nq - 1)
    edge_k = lambda f: pl.BlockSpec((tq, wkv), lambda b, n: (f(b, n), 0))
    edge_v = lambda f: pl.BlockSpec((wkv, tq), lambda b, n: (0, f(b, n)))
    return pl.pallas_call(
        functools.partial(_win_attn_kernel, n_tok=n_tok),
        grid=(bsz, ns),
        in_specs=[pl.BlockSpec((wq, qb * tq), lambda b, n: (0, cur(b, n))),
                  edge_k(prev), pl.BlockSpec((qb * tq, wkv), lambda b, n: (cur(b, n), 0)), edge_k(nxt),
                  edge_v(prev), pl.BlockSpec((wkv, qb * tq), lambda b, n: (0, cur(b, n))), edge_v(nxt),
                  pl.BlockSpec((n_ctx, wkv), lambda b, n: (b, 0)),
                  pl.BlockSpec((wkv, n_ctx), lambda b, n: (0, b)),
                  pl.BlockSpec((1, A_HEADS * tq), lambda b, n: (0, 0))],
        out_specs=pl.BlockSpec((qb * tq, wq), lambda b, n: (cur(b, n), 0)),
        out_shape=jax.ShapeDtypeStruct((bsz * n_tok, wq), BF16),
        compiler_params=_params(("parallel", "parallel"), 32 << 20),
        name="window_attention",
    )(qt, k, k, k, vt, vt, vt, kx, vxt, _sink_row(sink, tq))


def _context_attention(qt, kx, vxt, sink, bsz, n_ctx):
    tq = Q_BLOCK
    nq = n_ctx // tq
    wq, wkv = A_HEADS * HEAD_DIM, A_KV_HEADS * HEAD_DIM
    return pl.pallas_call(
        _ctx_attn_kernel,
        grid=(bsz, nq),
        in_specs=[pl.BlockSpec((wq, tq), lambda b, n: (0, b * nq + n)),
                  pl.BlockSpec((n_ctx, wkv), lambda b, n: (b, 0)),
                  pl.BlockSpec((wkv, n_ctx), lambda b, n: (0, b)),
                  pl.BlockSpec((1, A_HEADS * tq), lambda b, n: (0, 0))],
        out_specs=pl.BlockSpec((tq, wq), lambda b, n: (b * nq + n, 0)),
        out_shape=jax.ShapeDtypeStruct((bsz * n_ctx, wq), BF16),
        compiler_params=_params(("parallel", "parallel"), 32 << 20),
        name="context_attention",
    )(qt, kx, vxt, _sink_row(sink, tq))


def _diff_attn_kernel(*refs, n_lat_chunks, tk, lam_init, heads):
    if n_lat_chunks:
        q_ref, k_ref, v_ref, kx_ref, vx_ref, dl_ref, g_ref, o_ref = refs
    else:
        q_ref, kx_ref, vx_ref, dl_ref, g_ref, o_ref = refs
    tq = q_ref.shape[1]
    hw = B_VDIM
    row = lax.broadcasted_iota(jnp.int32, (hw, tq), 0)
    zero = jnp.zeros((hw, tq), q_ref.dtype)
    n = n_lat_chunks + 1

    def head_slice(hd):
        return slice(hd * hw, (hd + 1) * hw)

    qz = []
    for hd in range(heads):
        qt = q_ref[head_slice(hd), :]
        qz.append(jnp.concatenate([jnp.where(row < HEAD_DIM, qt, zero),
                                   jnp.where(row >= HEAD_DIM, qt, zero)], axis=1))

    def keys(hd, i):
        if i < n_lat_chunks:
            return k_ref[i * tk:(i + 1) * tk, head_slice(hd)]
        return kx_ref[:, head_slice(hd)]

    def values(hd, i):
        v = v_ref[head_slice(hd), i * tk:(i + 1) * tk] if i < n_lat_chunks else vx_ref[head_slice(hd), :]
        return jnp.concatenate([v, jnp.ones((BF16_ROWS, v.shape[1]), BF16)], axis=0)

    def scores(hd, i):
        s = jnp.dot(keys(hd, i), qz[hd], preferred_element_type=F32)
        return s, jnp.max(s, axis=0, keepdims=True)

    m = [jnp.full((1, 2 * tq), NEG_BIG, F32) for _ in range(heads)]
    acc = [jnp.zeros((hw + BF16_ROWS, 2 * tq), F32) for _ in range(heads)]
    nxt = [scores(hd, 0) for hd in range(heads)]
    e_prev = [None] * heads
    alpha = [None] * heads
    for c in range(n):
        cur = list(nxt)
        for hd in range(heads):
            if c + 1 < n:
                nxt[hd] = scores(hd, c + 1)
            if c:
                acc[hd] = alpha[hd] * acc[hd] + jnp.dot(values(hd, c - 1), e_prev[hd],
                                                        preferred_element_type=F32)
        for hd in range(heads):
            s, smax = cur[hd]
            m_new = jnp.maximum(m[hd], smax)
            alpha[hd] = jnp.exp2(m[hd] - m_new)
            e_prev[hd] = jnp.exp2(s - m_new).astype(BF16)
            m[hd] = m_new

    dl = dl_ref[...]
    lam = (jnp.exp(jnp.sum(dl[0:1] * dl[1:2], axis=1, keepdims=True))
           - jnp.exp(jnp.sum(dl[2:3] * dl[3:4], axis=1, keepdims=True)) + lam_init)
    for hd in range(heads):
        a = alpha[hd] * acc[hd] + jnp.dot(values(hd, n - 1), e_prev[hd], preferred_element_type=F32)
        o = a[:hw] * (1.0 / a[hw:hw + 1])
        y = (o[:, :tq] - lam * o[:, tq:]).T
        o_ref[:, head_slice(hd)] = (_rms(y, g_ref[...]) * (1.0 - lam_init)).astype(o_ref.dtype)


def _diff_attention(qt, k, vt, kx, vxt, diff_lambda, subln, lam_init, bsz, n_tok, n_ctx):
    nq_tok = n_tok if n_tok else n_ctx
    tq = min(DIFF_TQ, nq_tok)
    nqt = nq_tok // tq
    tk = min(DIFF_TK, n_tok) if n_tok else 0
    heads = DIFF_HEADS_PER_STEP
    hw = heads * B_VDIM
    in_specs = [pl.BlockSpec((hw, tq), lambda b, h, i: (h, b * nqt + i))]
    args = [qt]
    if n_tok:
        in_specs += [pl.BlockSpec((n_tok, hw), lambda b, h, i: (b, h)),
                     pl.BlockSpec((hw, n_tok), lambda b, h, i: (h, b))]
        args += [k, vt]
    in_specs += [pl.BlockSpec((n_ctx, hw), lambda b, h, i: (b, h)),
                 pl.BlockSpec((hw, n_ctx), lambda b, h, i: (h, b)),
                 pl.BlockSpec((4, HEAD_DIM), lambda b, h, i: (0, 0)),
                 pl.BlockSpec((1, B_VDIM), lambda b, h, i: (0, 0))]
    args += [kx, vxt, diff_lambda.astype(F32), subln.astype(F32).reshape(1, B_VDIM)]
    return pl.pallas_call(
        functools.partial(_diff_attn_kernel, n_lat_chunks=(n_tok // tk if n_tok else 0), tk=tk,
                          lam_init=lam_init, heads=heads),
        grid=(bsz, B_HEADS // heads, nqt),
        in_specs=in_specs,
        out_specs=pl.BlockSpec((tq, hw), lambda b, h, i: (b * nqt + i, h)),
        out_shape=jax.ShapeDtypeStruct((bsz * nq_tok, B_HEADS * B_VDIM), BF16),
        compiler_params=_params(("parallel", "parallel", "parallel"), 48 << 20),
        name="diff_attention",
    )(*args)


def _scan_segments(a, b, h_in, reverse):
    rows, w = a.shape
    seg = rows // SUBLANES
    order = range(seg - 1, -1, -1) if reverse else range(seg)
    h_loc = [None] * seg
    a_cum = [None] * seg
    h = prod = None
    for i in order:
        ai = a[i * SUBLANES:(i + 1) * SUBLANES]
        bi = b[i * SUBLANES:(i + 1) * SUBLANES]
        h = bi if h is None else ai * h + bi
        prod = ai if prod is None else ai * prod
        h_loc[i] = h
        a_cum[i] = prod
    sub = lax.broadcasted_iota(jnp.int32, (SUBLANES, w), 0)
    e, p = h, prod
    for d in (1, 2, 4):
        keep = (sub < SUBLANES - d) if reverse else (sub >= d)
        shift = SUBLANES - d if reverse else d
        p_s = jnp.where(keep, pltpu.roll(p, shift, 0), 1.0)
        e_s = jnp.where(keep, pltpu.roll(e, shift, 0), 0.0)
        e = e + p * e_s
        p = p * p_s
    seg_out = e + p * h_in
    if reverse:
        seg_in = jnp.where(sub < SUBLANES - 1, pltpu.roll(seg_out, SUBLANES - 1, 0), h_in)
        h_out = seg_out[0:1]
    else:
        seg_in = jnp.where(sub >= 1, pltpu.roll(seg_out, 1, 0), h_in)
        h_out = seg_out[SUBLANES - 1:SUBLANES]
    return [h_loc[i] + a_cum[i] * seg_in for i in range(seg)], h_out


def _lru_kernel(cu_ref, cg_ref, h0f_ref, h0b_ref, cw_ref, cb_ref, wg_ref, bg_ref, lam_ref,
                y_ref, hf_ref, hb_ref, xpad_ref, hs_ref, *, rows):
    n = cu_ref.shape[0]
    nchunks = n // rows
    w = C_WIDTH
    seg = rows // SUBLANES
    slabs = w // LANES

    def load(ref, idx):
        return jnp.concatenate([ref[j, idx, :] for j in range(slabs)], axis=1)

    def store(ref, idx, val):
        for j in range(slabs):
            ref[j, idx, :] = val[:, j * LANES:(j + 1) * LANES]

    store(xpad_ref, pl.ds(0, LRU_PAD), jnp.zeros((LRU_PAD, w), F32))
    store(xpad_ref, pl.ds(LRU_PAD + n, LRU_PAD), jnp.zeros((LRU_PAD, w), F32))

    def fill(ci, carry):
        s = pl.multiple_of(ci * rows, rows)
        store(xpad_ref, pl.ds(s + LRU_PAD, rows), cu_ref[pl.ds(s, rows), :].astype(F32))
        return carry
    lax.fori_loop(0, nchunks, fill, 0)

    lam = lam_ref[...]
    decay = LRU_C * (jnp.maximum(-lam, 0.0) + jnp.log1p(jnp.exp(-jnp.abs(lam))))
    cw = cw_ref[...]
    cb = cb_ref[...]

    def coeffs(s, d):
        first = LRU_PAD - CONV_W // 2
        taps = [load(xpad_ref, pl.ds(s + (first + q), SUBLANES, stride=seg))
                for q in range(seg + CONV_W - 1)]
        groups = []
        for i in range(seg):
            acc = cb
            for k in range(CONV_W):
                acc = acc + cw[k:k + 1] * taps[i + k]
            groups.append(acc)
        u = jnp.concatenate(groups, axis=0)
        g = jnp.dot(u.astype(BF16), wg_ref[:, d * 2 * w:(d + 1) * 2 * w],
                    preferred_element_type=F32) + bg_ref[:, d * 2 * w:(d + 1) * 2 * w]
        r = jax.nn.sigmoid(g[:, :w])
        i_gate = jax.nn.sigmoid(g[:, w:])
        z = decay[d:d + 1] * r
        a = jnp.exp(-z)
        v = (1.0 + a * a) * jnp.tanh(z)
        root = jnp.where(v > 0.0, v * lax.rsqrt(v), 0.0)
        return a, root * i_gate * u

    r_i = lax.broadcasted_iota(jnp.int32, (rows, rows), 0)
    c_i = lax.broadcasted_iota(jnp.int32, (rows, rows), 1)
    to_tokens = jnp.where(c_i == (r_i % seg) * SUBLANES + r_i // seg, 1.0, 0.0).astype(BF16)
    to_segments = jnp.where(r_i == (c_i % seg) * SUBLANES + c_i // seg, 1.0, 0.0).astype(BF16)

    def bwd(ci, h):
        s = pl.multiple_of((nchunks - 1 - ci) * rows, rows)
        a, bx = coeffs(s, 1)
        hs, h = _scan_segments(a, bx, h, True)
        hs_ref[pl.ds(s, rows), :] = jnp.concatenate(hs, axis=0)
        return h
    hb_ref[...] = lax.fori_loop(0, nchunks, bwd, h0b_ref[...])

    def fwd(ci, h):
        s = pl.multiple_of(ci * rows, rows)
        a, bx = coeffs(s, 0)
        hs, h = _scan_segments(a, bx, h, False)
        gate = jax.nn.gelu(jnp.dot(to_segments, cg_ref[pl.ds(s, rows), :], preferred_element_type=F32))
        y = ((jnp.concatenate(hs, axis=0) + hs_ref[pl.ds(s, rows), :]) * gate).astype(BF16)
        y_ref[pl.ds(s, rows), :] = jnp.dot(to_tokens, y, preferred_element_type=F32).astype(y_ref.dtype)
        return h
    hf_ref[...] = lax.fori_loop(0, nchunks, fwd, h0f_ref[...])


def _bidir_lru(cu, cg, h0f, h0b, conv_w, conv_b, wg_bf, bg, lam, bsz, n):
    w = C_WIDTH
    rows = min(LRU_ROWS, n)
    seq = pl.BlockSpec((n, w), lambda b: (b, 0))
    st = pl.BlockSpec((None, 1, w), lambda b: (b, 0, 0))
    vmem = 6 * n * w * 2 + n * w * 4 + (n + 2 * LRU_PAD) * w * 4 + (16 << 20)
    return pl.pallas_call(
        functools.partial(_lru_kernel, rows=rows),
        grid=(bsz,),
        in_specs=[seq, seq, st, st, _resident((CONV_W, w)), _resident((1, w)),
                  _resident((w, 4 * w)), _resident((1, 4 * w)), _resident((2, w))],
        out_specs=(seq, st, st),
        out_shape=(jax.ShapeDtypeStruct((bsz * n, w), BF16),
                   jax.ShapeDtypeStruct((bsz, 1, w), F32), jax.ShapeDtypeStruct((bsz, 1, w), F32)),
        scratch_shapes=[pltpu.VMEM((w // LANES, n + 2 * LRU_PAD, LANES), F32),
                        pltpu.VMEM((n, w), F32)],
        compiler_params=_params(("parallel",), vmem),
        name="bidir_lru",
    )(cu, cg, h0f, h0b, conv_w, conv_b, wg_bf, bg, lam)


def _merge_kernel(ya_ref, yb_ref, yc_ref, gt_ref, x_ref, gx_ref, g_ref, wbr_ref, wout_ref, o_ref):
    acc = None
    for n, y_ref in enumerate((ya_ref, yb_ref, yc_ref)):
        z = jnp.dot(y_ref[...], wbr_ref[n], preferred_element_type=F32)
        gate = jax.nn.sigmoid(gt_ref[:, n * D_MODEL:(n + 1) * D_MODEL].astype(F32))
        acc = gate * z if acc is None else acc + gate * z
    mix = jnp.dot(acc.astype(BF16), wout_ref[...], preferred_element_type=F32)
    o_ref[...] = x_ref[...] + gx_ref[...] * _rms(mix, g_ref[...])


def _merge(ya, yb, yc, gt, x2, gate_x, g1, wbr_bf, wout_bf, tm):
    t, d = x2.shape
    nt = t // tm
    nb = gate_x.shape[0]
    tok = lambda w: pl.BlockSpec((tm, w), lambda i: (i, 0))
    modspec = pl.BlockSpec((None, 1, d), lambda i: (i // (nt // nb), 0, 0))
    vmem = (N_BRANCH * BRANCH_W * d + d * d) * 2 + 2 * tm * (3 * BRANCH_W + 3 * d) * 2 \
        + 4 * tm * d * 4 + 6 * tm * d * 4 + (8 << 20)
    return pl.pallas_call(
        _merge_kernel,
        grid=(nt,),
        in_specs=[tok(BRANCH_W), tok(BRANCH_W), tok(BRANCH_W), tok(N_BRANCH * d), tok(d), modspec,
                  _resident((1, d)), _resident((N_BRANCH, BRANCH_W, d)), _resident((d, d))],
        out_specs=tok(d),
        out_shape=jax.ShapeDtypeStruct((t, d), F32),
        compiler_params=_params(("parallel",), vmem),
        name="gated_merge",
    )(ya, yb, yc, gt, x2, gate_x, g1, wbr_bf, wout_bf)


def _ffn_kernel(x_ref, sh_ref, sc_ref, gx_ref, g2_ref, g3_ref, w1_ref, w2_ref, o_ref, *, n_chunks):
    x = x_ref[...]
    h = _rms(x, g2_ref[...])
    h = (h * (1.0 + sc_ref[...]) + sh_ref[...]).astype(BF16)
    hc = FFN_HIDDEN // n_chunks
    f = None
    for c in range(n_chunks):
        gate = jnp.dot(h, w1_ref[:, c * hc:(c + 1) * hc], preferred_element_type=F32)
        up = jnp.dot(h, w1_ref[:, FFN_HIDDEN + c * hc:FFN_HIDDEN + (c + 1) * hc],
                     preferred_element_type=F32)
        act = (gate * jax.nn.sigmoid(gate) * up).astype(BF16)
        part = jnp.dot(act, w2_ref[c * hc:(c + 1) * hc, :], preferred_element_type=F32)
        f = part if f is None else f + part
    o_ref[...] = x + gx_ref[...] * _rms(f, g3_ref[...])


def _ffn(x2, shift, scale, gate_x, g2, g3, w1_bf, w2_bf, tm):
    t, d = x2.shape
    nt = t // tm
    nb = shift.shape[0]
    tok = pl.BlockSpec((tm, d), lambda i: (i, 0))
    modspec = pl.BlockSpec((None, 1, d), lambda i: (i // (nt // nb), 0, 0))
    n_chunks = 2
    vmem = 3 * d * FFN_HIDDEN * 2 + 4 * tm * d * 4 + 3 * tm * (FFN_HIDDEN // n_chunks) * 4 \
        + 4 * tm * d * 4 + (8 << 20)
    return pl.pallas_call(
        functools.partial(_ffn_kernel, n_chunks=n_chunks),
        grid=(nt,),
        in_specs=[tok, modspec, modspec, modspec, _resident((1, d)), _resident((1, d)),
                  _resident((d, 2 * FFN_HIDDEN)), _resident((FFN_HIDDEN, d))],
        out_specs=tok,
        out_shape=jax.ShapeDtypeStruct((t, d), F32),
        compiler_params=_params(("parallel",), vmem),
        name="swiglu_ffn",
    )(x2, shift, scale, gate_x, g2, g3, w1_bf, w2_bf)


def _rope_tables(n_tokens):
    rd = HEAD_DIM // 4
    t = jnp.arange(n_tokens)
    pos = jnp.stack([t // GRID_W, t % GRID_W], axis=-1).astype(F32)
    inv = 1.0 / (ROPE_THETA ** (jnp.arange(rd, dtype=F32) * 2.0 / (HEAD_DIM // 2)))
    ang = pos[:, :, None] * inv
    cos = jnp.cos(ang)[:, :, None, :]
    sin = jnp.sin(ang)[:, :, None, :]
    zeros = jnp.zeros_like(sin)
    cos64 = jnp.concatenate([cos, cos], axis=2).reshape(n_tokens, HEAD_DIM)
    hi64 = jnp.concatenate([zeros, sin], axis=2).reshape(n_tokens, HEAD_DIM)
    lo64 = jnp.concatenate([-sin, zeros], axis=2).reshape(n_tokens, HEAD_DIM)
    rep = LANES // HEAD_DIM
    return jnp.tile(cos64, (1, rep)), jnp.tile(hi64, (1, rep)), jnp.tile(lo64, (1, rep))


def _identity_tables(rows):
    z = jnp.zeros((rows, LANES), F32)
    return jnp.ones((rows, LANES), F32), z, z


def _gate_weights(lru_w, lru_b):
    eye = jnp.eye(C_BLOCKS, dtype=lru_w.dtype)
    dense = jnp.einsum("dgncf,nm->ncdgmf", lru_w, eye).reshape(C_WIDTH, 4 * C_WIDTH)
    return dense.astype(BF16), lru_b.astype(F32).reshape(1, 4 * C_WIDTH)


def kernel(x, c, ctx, c_ctx, w_mod, b_mod, norm_g, w_in, attn_sink, diff_lambda, diff_subln, conv_w,
           conv_b, lru_w, lru_b, lru_lambda, w_branch, w_out, w_ffn_in, w_ffn_out):
    bsz, n_tok, d = x.shape
    n_ctx = ctx.shape[1]
    depth = w_mod.shape[0]
    tm = min(512, n_tok)
    tmc = min(256, n_ctx)

    mod_rows = 2 * SUBLANES
    c_rows = jnp.zeros((mod_rows, d), F32).at[:bsz].set(c).at[bsz].set(c_ctx)
    mods = _modulation(c_rows, w_mod, b_mod)

    cos, sin_hi, sin_lo = _rope_tables(n_tok)
    cos_c, sin_hi_c, sin_lo_c = _identity_tables(tmc)

    x2 = x.reshape(bsz * n_tok, d)
    cx2 = ctx.reshape(bsz * n_ctx, d)
    zero_state = jnp.zeros((bsz, 1, C_WIDTH), F32)

    for l in range(depth):
        need_ctx = l < depth - 1
        lam_init = 0.8 - 0.6 * math.exp(-0.3 * l)
        mx = [mods[l, :bsz, k * d:(k + 1) * d].reshape(bsz, 1, d) for k in range(6)]
        mc = [mods[l, bsz:bsz + 1, k * d:(k + 1) * d].reshape(1, 1, d) for k in range(6)]
        g = [norm_g[l, k].reshape(1, d).astype(F32) for k in range(4)]
        w_in_bf = w_in[l].astype(BF16)
        wg_bf, bg = _gate_weights(lru_w[l], lru_b[l])
        cw = conv_w[l].astype(F32)
        cb = conv_b[l].astype(F32).reshape(1, C_WIDTH)
        lam = lru_lambda[l].astype(F32)
        wbr_bf = w_branch[l].astype(BF16)
        wout_bf = w_out[l].astype(BF16)
        w1_bf = w_ffn_in[l].astype(BF16)
        w2_bf = w_ffn_out[l].astype(BF16)

        (qa_c, ka_c, va_c, qb_c, kb_c, vb_c, cu_c, cg_c, gt_c) = _in_proj(
            cx2, mc[0], mc[1], g[0], cos_c, sin_hi_c, sin_lo_c, w_in_bf, tmc)
        yc_c, hf_c, hb_c = _bidir_lru(cu_c, cg_c, zero_state, zero_state, cw, cb, wg_bf, bg, lam,
                                      bsz, n_ctx)

        (qa, ka, va, qb, kb, vb, cu, cg, gt) = _in_proj(
            x2, mx[0], mx[1], g[0], cos, sin_hi, sin_lo, w_in_bf, tm)
        ya = _window_attention(qa, ka, va, ka_c, va_c, attn_sink[l], bsz, n_tok, n_ctx)
        yb = _diff_attention(qb, kb, vb, kb_c, vb_c, diff_lambda[l], diff_subln[l], lam_init,
                             bsz, n_tok, n_ctx)
        yc, _, _ = _bidir_lru(cu, cg, hf_c, hb_c, cw, cb, wg_bf, bg, lam, bsz, n_tok)
        x2 = _merge(ya, yb, yc, gt, x2, mx[2], g[1], wbr_bf, wout_bf, tm)
        x2 = _ffn(x2, mx[3], mx[4], mx[5], g[2], g[3], w1_bf, w2_bf, tm)

        if need_ctx:
            ya_c = _context_attention(qa_c, ka_c, va_c, attn_sink[l], bsz, n_ctx)
            yb_c = _diff_attention(qb_c, None, None, kb_c, vb_c, diff_lambda[l], diff_subln[l],
                                   lam_init, bsz, 0, n_ctx)
            cx2 = _merge(ya_c, yb_c, yc_c, gt_c, cx2, mc[2], g[1], wbr_bf, wout_bf, tmc)
            cx2 = _ffn(cx2, mc[3], mc[4], mc[5], g[2], g[3], w1_bf, w2_bf, tmc)

    return x2.reshape(bsz, n_tok, d)
```

```python
import functools
import math

import jax
import jax.numpy as jnp
from jax import lax
from jax.experimental import pallas as pl
from jax.experimental.pallas import tpu as pltpu

F32 = jnp.float32
BF16 = jnp.bfloat16

D_MODEL = 1024
GRID_W = 64
HEAD_DIM = 64
ROPE_THETA = 10000.0
EPS = 1e-6
A_HEADS = 8
A_KV_HEADS = 2
A_GROUPS = A_HEADS // A_KV_HEADS
WINDOW = 128
B_HEADS = 4
B_VDIM = 2 * HEAD_DIM
C_WIDTH = 512
C_BLOCKS = 8
C_BW = C_WIDTH // C_BLOCKS
CONV_W = 4
LRU_C = 8.0
N_BRANCH = 3
BRANCH_W = 512
FFN_HIDDEN = -(-8 * D_MODEL // (3 * 256)) * 256

O_AQ = 0
O_AK = O_AQ + A_HEADS * HEAD_DIM
O_AV = O_AK + A_KV_HEADS * HEAD_DIM
O_BQ = O_AV + A_KV_HEADS * HEAD_DIM
O_BK = O_BQ + B_HEADS * 2 * HEAD_DIM
O_BV = O_BK + B_HEADS * 2 * HEAD_DIM
O_CU = O_BV + B_HEADS * B_VDIM
O_CG = O_CU + C_WIDTH
O_GT = O_CG + C_WIDTH
IN_WIDTH = O_GT + N_BRANCH * D_MODEL

LANES = 128
SUBLANES = 8
BF16_ROWS = 16
MXU_TILE = 256
VMEM_CAP_BYTES = 60 * 1024 * 1024
NEG_BIG = -1e30
Q_BLOCK = 128
WIN_Q_BLOCKS = 4
DIFF_TQ = 512
DIFF_TK = 512
DIFF_HEADS_PER_STEP = 2
LRU_ROWS = 256
LRU_PAD = 8
LOG2E = math.log2(math.e)
Q_SCALE = HEAD_DIM ** -0.5 * LOG2E


def _params(semantics, vmem_bytes):
    return pltpu.CompilerParams(dimension_semantics=semantics,
                                vmem_limit_bytes=min(int(vmem_bytes), VMEM_CAP_BYTES))


def _resident(shape):
    nd = len(shape)
    return pl.BlockSpec(shape, lambda *_: (0,) * nd, pipeline_mode=pl.Buffered(1))


def _layer_resident(shape, layer):
    nd = len(shape)
    return pl.BlockSpec((None,) + tuple(shape), lambda *_: (layer,) + (0,) * nd,
                        pipeline_mode=pl.Buffered(1))


def _rms(x, g):
    return x * lax.rsqrt(jnp.mean(x * x, axis=-1, keepdims=True) + EPS) * g


def _mod_kernel(c_ref, w_ref, b_ref, o_ref):
    c = c_ref[...]
    s = c * jax.nn.sigmoid(c)
    o_ref[...] = jnp.dot(s, w_ref[...], preferred_element_type=F32,
                         precision=lax.Precision.HIGHEST) + b_ref[...]


def _modulation(c_rows, w_mod, b_mod):
    depth, d, n = w_mod.shape
    rows = c_rows.shape[0]
    tn = 1536
    return pl.pallas_call(
        _mod_kernel,
        grid=(depth, n // tn),
        in_specs=[pl.BlockSpec((rows, d), lambda l, j: (0, 0)),
                  pl.BlockSpec((None, d, tn), lambda l, j: (l, 0, j)),
                  pl.BlockSpec((None, 1, tn), lambda l, j: (l, 0, j))],
        out_specs=pl.BlockSpec((None, rows, tn), lambda l, j: (l, 0, j)),
        out_shape=jax.ShapeDtypeStruct((depth, rows, n), F32),
        compiler_params=_params(("parallel", "parallel"), 32 << 20),
        name="modulation",
    )(c_rows, w_mod, b_mod.reshape(depth, 1, n))


def _in_proj_kernel(x_ref, sh_ref, sc_ref, g_ref, cos_ref, sa_ref, sb_ref, w_ref,
                    qa_ref, ka_ref, va_ref, qb_ref, kb_ref, vb_ref, cu_ref, cg_ref, gt_ref):
    h = _rms(x_ref[...], g_ref[...])
    h = (h * (1.0 + sc_ref[...]) + sh_ref[...]).astype(BF16)

    def proj(lo, width):
        return jnp.dot(h, w_ref[:, lo:lo + width], preferred_element_type=F32)

    cos = cos_ref[...]
    sin_hi = sa_ref[...]
    sin_lo = sb_ref[...]

    def rope(a):
        outs = []
        for k in range(a.shape[1] // LANES):
            blk = a[:, k * LANES:(k + 1) * LANES]
            outs.append(blk * cos + pltpu.roll(blk, HEAD_DIM // 4, 1) * sin_hi
                        + pltpu.roll(blk, LANES - HEAD_DIM // 4, 1) * sin_lo)
        return jnp.concatenate(outs, axis=1)

    qa_ref[...] = (rope(proj(O_AQ, O_AK - O_AQ)) * Q_SCALE).T.astype(BF16)
    kv_a = proj(O_AK, O_BQ - O_AK)
    ka_ref[...] = rope(kv_a[:, :O_AV - O_AK]).astype(BF16)
    va_ref[...] = kv_a[:, O_AV - O_AK:].T.astype(BF16)
    qb_ref[...] = (rope(proj(O_BQ, O_BK - O_BQ)) * Q_SCALE).T.astype(BF16)
    kb_ref[...] = rope(proj(O_BK, O_BV - O_BK)).astype(BF16)
    vb_ref[...] = proj(O_BV, O_CU - O_BV).T.astype(BF16)
    cu_ref[...] = proj(O_CU, C_WIDTH).astype(BF16)
    cg_ref[...] = proj(O_CG, C_WIDTH).astype(BF16)
    for k in range(N_BRANCH):
        gt_ref[:, k * D_MODEL:(k + 1) * D_MODEL] = proj(O_GT + k * D_MODEL, D_MODEL).astype(BF16)


def _in_proj(x2, shift, scale, g0, cos, sin_hi, sin_lo, w_bf, layer, tm):
    t, d = x2.shape
    nt = t // tm
    nb = shift.shape[0]
    npos = cos.shape[0] // tm
    tok = lambda w: pl.BlockSpec((tm, w), lambda i: (i, 0))
    feat = lambda w: pl.BlockSpec((w, tm), lambda i: (0, i))
    modspec = pl.BlockSpec((None, 1, d), lambda i: (i // (nt // nb), 0, 0))
    tabspec = pl.BlockSpec((tm, LANES), lambda i: (i % npos, 0))
    wq, wkv = A_HEADS * HEAD_DIM, A_KV_HEADS * HEAD_DIM
    wb = B_HEADS * B_VDIM
    out_shape = (
        jax.ShapeDtypeStruct((wq, t), BF16), jax.ShapeDtypeStruct((t, wkv), BF16),
        jax.ShapeDtypeStruct((wkv, t), BF16), jax.ShapeDtypeStruct((wb, t), BF16),
        jax.ShapeDtypeStruct((t, wb), BF16), jax.ShapeDtypeStruct((wb, t), BF16),
        jax.ShapeDtypeStruct((t, C_WIDTH), BF16), jax.ShapeDtypeStruct((t, C_WIDTH), BF16),
        jax.ShapeDtypeStruct((t, N_BRANCH * D_MODEL), BF16))
    out_specs = (feat(wq), tok(wkv), feat(wkv), feat(wb), tok(wb), feat(wb),
                 tok(C_WIDTH), tok(C_WIDTH), tok(N_BRANCH * D_MODEL))
    vmem = d * IN_WIDTH * 2 + 2 * tm * d * 4 + 2 * tm * IN_WIDTH * 2 + 6 * tm * d * 4 + (8 << 20)
    return pl.pallas_call(
        _in_proj_kernel,
        grid=(nt,),
        in_specs=[tok(d), modspec, modspec, _resident((1, d)), tabspec, tabspec, tabspec,
                  _layer_resident((d, IN_WIDTH), layer)],
        out_specs=out_specs,
        out_shape=out_shape,
        compiler_params=_params(("parallel",), vmem),
        name="in_proj",
    )(x2, shift, scale, g0, cos, sin_hi, sin_lo, w_bf)


def _sink_scores(k_all, qt):
    tq = qt.shape[1]
    zero = jnp.zeros((HEAD_DIM, tq), qt.dtype)
    scores = []
    for j in range(A_KV_HEADS):
        cols = []
        for g in range(A_GROUPS):
            hd = j * A_GROUPS + g
            qh = qt[hd * HEAD_DIM:(hd + 1) * HEAD_DIM, :]
            cols.append(jnp.concatenate([qh, zero] if j == 0 else [zero, qh], axis=0))
        qz = jnp.concatenate(cols, axis=1)
        scores.append(jnp.dot(k_all, qz, preferred_element_type=F32))
    return scores


def _sink_softmax_pv(scores, vt_all, sink_row, bias, n_win):
    tq = scores[0].shape[1] // A_GROUPS
    outs = []
    ones = jnp.ones((BF16_ROWS, vt_all.shape[1]), BF16)
    for j in range(A_KV_HEADS):
        s = scores[j]
        sink = sink_row[:, j * A_GROUPS * tq:(j + 1) * A_GROUPS * tq] * LOG2E
        if n_win:
            sw = s[:n_win] + jnp.concatenate([bias] * A_GROUPS, axis=1)
            sc = s[n_win:]
            m = jnp.maximum(jnp.maximum(jnp.max(sw, axis=0, keepdims=True),
                                        jnp.max(sc, axis=0, keepdims=True)), sink)
            p = jnp.concatenate([jnp.exp2(sw - m), jnp.exp2(sc - m)], axis=0).astype(BF16)
        else:
            m = jnp.maximum(jnp.max(s, axis=0, keepdims=True), sink)
            p = jnp.exp2(s - m).astype(BF16)
        vt_aug = jnp.concatenate([vt_all[j * HEAD_DIM:(j + 1) * HEAD_DIM, :], ones], axis=0)
        o = jnp.dot(vt_aug, p, preferred_element_type=F32)
        l = o[HEAD_DIM:HEAD_DIM + 1] + jnp.exp2(sink - m)
        o = o[:HEAD_DIM] * (1.0 / l)
        for g in range(A_GROUPS):
            outs.append(o[:, g * tq:(g + 1) * tq])
    return jnp.concatenate(outs, axis=0)


def _win_attn_kernel(q_ref, kp_ref, kc_ref, kn_ref, vp_ref, vc_ref, vn_ref, kx_ref, vx_ref,
                     sink_ref, o_ref, *, n_tok):
    tq = Q_BLOCK
    q_blocks = q_ref.shape[1] // tq
    first = pl.program_id(1) * q_blocks
    k_cat = jnp.concatenate([kp_ref[...], kc_ref[...], kn_ref[...]], axis=0)
    vt_cat = jnp.concatenate([vp_ref[...], vc_ref[...], vn_ref[...]], axis=1)
    kx = kx_ref[...]
    vx = vx_ref[...]
    n_win = 3 * tq
    r = lax.broadcasted_iota(jnp.int32, (n_win, tq), 0)
    c = lax.broadcasted_iota(jnp.int32, (n_win, tq), 1)
    band = jnp.where(jnp.abs(r - tq - c) <= WINDOW, 0.0, NEG_BIG).astype(F32)

    def scores(b):
        k_all = jnp.concatenate([k_cat[b * tq:(b + 3) * tq], kx], axis=0)
        return _sink_scores(k_all, q_ref[:, b * tq:(b + 1) * tq])

    s_next = scores(0)
    for b in range(q_blocks):
        s_cur = s_next
        if b + 1 < q_blocks:
            s_next = scores(b + 1)
        kpos = (first + b - 1) * tq + r
        bias = band + jnp.where(kpos >= 0, 0.0, NEG_BIG) + jnp.where(kpos < n_tok, 0.0, NEG_BIG)
        vt_all = jnp.concatenate([vt_cat[:, b * tq:(b + 3) * tq], vx], axis=1)
        yt = _sink_softmax_pv(s_cur, vt_all, sink_ref[...], bias, n_win)
        o_ref[b * tq:(b + 1) * tq, :] = yt.T.astype(o_ref.dtype)


def _ctx_attn_kernel(q_ref, kx_ref, vx_ref, sink_ref, o_ref):
    yt = _sink_softmax_pv(_sink_scores(kx_ref[...], q_ref[...]), vx_ref[...], sink_ref[...], None, 0)
    o_ref[...] = yt.T.astype(o_ref.dtype)


def _sink_row(sink, tq):
    return jnp.repeat(sink.astype(F32), tq).reshape(1, A_HEADS * tq)


def _window_attention(qt, k, vt, kx, vxt, sink, bsz, n_tok, n_ctx):
    tq = Q_BLOCK
    nq = n_tok // tq
    qb = min(WIN_Q_BLOCKS, nq)
    ns = nq // qb
    wq, wkv = A_HEADS * HEAD_DIM, A_KV_HEADS * HEAD_DIM
    prev = lambda b, n: b * nq + jnp.maximum(n * qb - 1, 0)
    cur = lambda b, n: b * ns + n
    nxt = lambda b, n: b * nq + jnp.minimum(n * qb + qb, nq - 1)
    edge_k = lambda f: pl.BlockSpec((tq, wkv), lambda b, n: (f(b, n), 0))
    edge_v = lambda f: pl.BlockSpec((wkv, tq), lambda b, n: (0, f(b, n)))
    return pl.pallas_call(
        functools.partial(_win_attn_kernel, n_tok=n_tok),
        grid=(bsz, ns),
        in_specs=[pl.BlockSpec((wq, qb * tq), lambda b, n: (0, cur(b, n))),
                  edge_k(prev), pl.BlockSpec((qb * tq, wkv), lambda b, n: (cur(b, n), 0)), edge_k(nxt),
                  edge_v(prev), pl.BlockSpec((wkv, qb * tq), lambda b, n: (0, cur(b, n))), edge_v(nxt),
                  pl.BlockSpec((n_ctx, wkv), lambda b, n: (b, 0)),
                  pl.BlockSpec((wkv, n_ctx), lambda b, n: (0, b)),
                  pl.BlockSpec((1, A_HEADS * tq), lambda b, n: (0, 0))],
        out_specs=pl.BlockSpec((qb * tq, wq), lambda b, n: (cur(b, n), 0)),
        out_shape=jax.ShapeDtypeStruct((bsz * n_tok, wq), BF16),
        compiler_params=_params(("parallel", "parallel"), 32 << 20),
        name="window_attention",
    )(qt, k, k, k, vt, vt, vt, kx, vxt, _sink_row(sink, tq))


def _context_attention(qt, kx, vxt, sink, bsz, n_ctx):
    tq = Q_BLOCK
    nq = n_ctx // tq
    wq, wkv = A_HEADS * HEAD_DIM, A_KV_HEADS * HEAD_DIM
    return pl.pallas_call(
        _ctx_attn_kernel,
        grid=(bsz, nq),
        in_specs=[pl.BlockSpec((wq, tq), lambda b, n: (0, b * nq + n)),
                  pl.BlockSpec((n_ctx, wkv), lambda b, n: (b, 0)),
                  pl.BlockSpec((wkv, n_ctx), lambda b, n: (0, b)),
                  pl.BlockSpec((1, A_HEADS * tq), lambda b, n: (0, 0))],
        out_specs=pl.BlockSpec((tq, wq), lambda b, n: (b * nq + n, 0)),
        out_shape=jax.ShapeDtypeStruct((bsz * n_ctx, wq), BF16),
        compiler_params=_params(("parallel", "parallel"), 32 << 20),
        name="context_attention",
    )(qt, kx, vxt, _sink_row(sink, tq))


def _diff_attn_kernel(*refs, n_lat_chunks, tk, lam_init, heads):
    if n_lat_chunks:
        q_ref, k_ref, v_ref, kx_ref, vx_ref, dl_ref, g_ref, o_ref = refs
    else:
        q_ref, kx_ref, vx_ref, dl_ref, g_ref, o_ref = refs
    tq = q_ref.shape[1]
    hw = B_VDIM
    row = lax.broadcasted_iota(jnp.int32, (hw, tq), 0)
    zero = jnp.zeros((hw, tq), q_ref.dtype)
    n = n_lat_chunks + 1

    def head_slice(hd):
        return slice(hd * hw, (hd + 1) * hw)

    qz = []
    for hd in range(heads):
        qt = q_ref[head_slice(hd), :]
        qz.append(jnp.concatenate([jnp.where(row < HEAD_DIM, qt, zero),
                                   jnp.where(row >= HEAD_DIM, qt, zero)], axis=1))

    def keys(hd, i):
        if i < n_lat_chunks:
            return k_ref[i * tk:(i + 1) * tk, head_slice(hd)]
        return kx_ref[:, head_slice(hd)]

    def values(hd, i):
        v = v_ref[head_slice(hd), i * tk:(i + 1) * tk] if i < n_lat_chunks else vx_ref[head_slice(hd), :]
        return jnp.concatenate([v, jnp.ones((BF16_ROWS, v.shape[1]), BF16)], axis=0)

    def scores(hd, i):
        s = jnp.dot(keys(hd, i), qz[hd], preferred_element_type=F32)
        return s, jnp.max(s, axis=0, keepdims=True)

    m = [jnp.full((1, 2 * tq), NEG_BIG, F32) for _ in range(heads)]
    acc = [jnp.zeros((hw + BF16_ROWS, 2 * tq), F32) for _ in range(heads)]
    nxt = [scores(hd, 0) for hd in range(heads)]
    e_prev = [None] * heads
    alpha = [None] * heads
    for c in range(n):
        cur = list(nxt)
        for hd in range(heads):
            if c + 1 < n:
                nxt[hd] = scores(hd, c + 1)
            if c:
                acc[hd] = alpha[hd] * acc[hd] + jnp.dot(values(hd, c - 1), e_prev[hd],
                                                        preferred_element_type=F32)
        for hd in range(heads):
            s, smax = cur[hd]
            m_new = jnp.maximum(m[hd], smax)
            alpha[hd] = jnp.exp2(m[hd] - m_new)
            e_prev[hd] = jnp.exp2(s - m_new).astype(BF16)
            m[hd] = m_new

    dl = dl_ref[...]
    lam = (jnp.exp(jnp.sum(dl[0:1] * dl[1:2], axis=1, keepdims=True))
           - jnp.exp(jnp.sum(dl[2:3] * dl[3:4], axis=1, keepdims=True)) + lam_init)
    for hd in range(heads):
        a = alpha[hd] * acc[hd] + jnp.dot(values(hd, n - 1), e_prev[hd], preferred_element_type=F32)
        o = a[:hw] * (1.0 / a[hw:hw + 1])
        y = (o[:, :tq] - lam * o[:, tq:]).T
        o_ref[:, head_slice(hd)] = (_rms(y, g_ref[...]) * (1.0 - lam_init)).astype(o_ref.dtype)


def _diff_attention(qt, k, vt, kx, vxt, diff_lambda, subln, lam_init, bsz, n_tok, n_ctx):
    nq_tok = n_tok if n_tok else n_ctx
    tq = min(DIFF_TQ, nq_tok)
    nqt = nq_tok // tq
    tk = min(DIFF_TK, n_tok) if n_tok else 0
    heads = DIFF_HEADS_PER_STEP
    hw = heads * B_VDIM
    in_specs = [pl.BlockSpec((hw, tq), lambda b, h, i: (h, b * nqt + i))]
    args = [qt]
    if n_tok:
        in_specs += [pl.BlockSpec((n_tok, hw), lambda b, h, i: (b, h)),
                     pl.BlockSpec((hw, n_tok), lambda b, h, i: (h, b))]
        args += [k, vt]
    in_specs += [pl.BlockSpec((n_ctx, hw), lambda b, h, i: (b, h)),
                 pl.BlockSpec((hw, n_ctx), lambda b, h, i: (h, b)),
                 pl.BlockSpec((4, HEAD_DIM), lambda b, h, i: (0, 0)),
                 pl.BlockSpec((1, B_VDIM), lambda b, h, i: (0, 0))]
    args += [kx, vxt, diff_lambda.astype(F32), subln.astype(F32).reshape(1, B_VDIM)]
    return pl.pallas_call(
        functools.partial(_diff_attn_kernel, n_lat_chunks=(n_tok // tk if n_tok else 0), tk=tk,
                          lam_init=lam_init, heads=heads),
        grid=(bsz, B_HEADS // heads, nqt),
        in_specs=in_specs,
        out_specs=pl.BlockSpec((tq, hw), lambda b, h, i: (b * nqt + i, h)),
        out_shape=jax.ShapeDtypeStruct((bsz * nq_tok, B_HEADS * B_VDIM), BF16),
        compiler_params=_params(("parallel", "parallel", "parallel"), 48 << 20),
        name="diff_attention",
    )(*args)


def _scan_segments(a, b, h_in, reverse):
    rows, w = a.shape
    seg = rows // SUBLANES
    order = range(seg - 1, -1, -1) if reverse else range(seg)
    h_loc = [None] * seg
    a_cum = [None] * seg
    h = prod = None
    for i in order:
        ai = a[i * SUBLANES:(i + 1) * SUBLANES]
        bi = b[i * SUBLANES:(i + 1) * SUBLANES]
        h = bi if h is None else ai * h + bi
        prod = ai if prod is None else ai * prod
        h_loc[i] = h
        a_cum[i] = prod
    sub = lax.broadcasted_iota(jnp.int32, (SUBLANES, w), 0)
    e, p = h, prod
    for d in (1, 2, 4):
        keep = (sub < SUBLANES - d) if reverse else (sub >= d)
        shift = SUBLANES - d if reverse else d
        p_s = jnp.where(keep, pltpu.roll(p, shift, 0), 1.0)
        e_s = jnp.where(keep, pltpu.roll(e, shift, 0), 0.0)
        e = e + p * e_s
        p = p * p_s
    seg_out = e + p * h_in
    if reverse:
        seg_in = jnp.where(sub < SUBLANES - 1, pltpu.roll(seg_out, SUBLANES - 1, 0), h_in)
        h_out = seg_out[0:1]
    else:
        seg_in = jnp.where(sub >= 1, pltpu.roll(seg_out, 1, 0), h_in)
        h_out = seg_out[SUBLANES - 1:SUBLANES]
    return [h_loc[i] + a_cum[i] * seg_in for i in range(seg)], h_out


def _lru_kernel(cu_ref, cg_ref, h0f_ref, h0b_ref, cw_ref, cb_ref, wg_ref, bg_ref, lam_ref,
                y_ref, hf_ref, hb_ref, xpad_ref, hs_ref, u_ref, *, rows):
    n = cu_ref.shape[0]
    nchunks = n // rows
    w = C_WIDTH
    seg = rows // SUBLANES
    slabs = w // LANES

    def load(ref, idx):
        return jnp.concatenate([ref[j, idx, :] for j in range(slabs)], axis=1)

    def store(ref, idx, val):
        for j in range(slabs):
            ref[j, idx, :] = val[:, j * LANES:(j + 1) * LANES]

    store(xpad_ref, pl.ds(0, LRU_PAD), jnp.zeros((LRU_PAD, w), F32))
    store(xpad_ref, pl.ds(LRU_PAD + n, LRU_PAD), jnp.zeros((LRU_PAD, w), F32))

    def fill(ci, carry):
        s = pl.multiple_of(ci * rows, rows)
        store(xpad_ref, pl.ds(s + LRU_PAD, rows), cu_ref[pl.ds(s, rows), :].astype(F32))
        return carry
    lax.fori_loop(0, nchunks, fill, 0)

    lam = lam_ref[...]
    decay = LRU_C * (jnp.maximum(-lam, 0.0) + jnp.log1p(jnp.exp(-jnp.abs(lam))))
    cw = cw_ref[...]
    cb = cb_ref[...]

    def conv(s):
        first = LRU_PAD - CONV_W // 2
        taps = [load(xpad_ref, pl.ds(s + (first + q), SUBLANES, stride=seg))
                for q in range(seg + CONV_W - 1)]
        groups = []
        for i in range(seg):
            acc = cb
            for k in range(CONV_W):
                acc = acc + cw[k:k + 1] * taps[i + k]
            groups.append(acc)
        return jnp.concatenate(groups, axis=0)

    def coeffs(u, d):
        g = jnp.dot(u.astype(BF16), wg_ref[:, d * 2 * w:(d + 1) * 2 * w],
                    preferred_element_type=F32) + bg_ref[:, d * 2 * w:(d + 1) * 2 * w]
        r = jax.nn.sigmoid(g[:, :w])
        i_gate = jax.nn.sigmoid(g[:, w:])
        z = decay[d:d + 1] * r
        a = jnp.exp(-z)
        v = (1.0 + a * a) * jnp.tanh(z)
        root = jnp.where(v > 0.0, v * lax.rsqrt(v), 0.0)
        return a, root * i_gate * u

    r_i = lax.broadcasted_iota(jnp.int32, (rows, rows), 0)
    c_i = lax.broadcasted_iota(jnp.int32, (rows, rows), 1)
    to_tokens = jnp.where(c_i == (r_i % seg) * SUBLANES + r_i // seg, 1.0, 0.0).astype(BF16)
    to_segments = jnp.where(r_i == (c_i % seg) * SUBLANES + c_i // seg, 1.0, 0.0).astype(BF16)

    def bwd(ci, h):
        s = pl.multiple_of((nchunks - 1 - ci) * rows, rows)
        u = conv(s)
        u_ref[pl.ds(s, rows), :] = u
        a, bx = coeffs(u, 1)
        hs, h = _scan_segments(a, bx, h, True)
        hs_ref[pl.ds(s, rows), :] = jnp.concatenate(hs, axis=0)
        return h
    hb_ref[...] = lax.fori_loop(0, nchunks, bwd, h0b_ref[...])

    def fwd(ci, h):
        s = pl.multiple_of(ci * rows, rows)
        a, bx = coeffs(u_ref[pl.ds(s, rows), :], 0)
        hs, h = _scan_segments(a, bx, h, False)
        gate = jax.nn.gelu(jnp.dot(to_segments, cg_ref[pl.ds(s, rows), :], preferred_element_type=F32))
        y = ((jnp.concatenate(hs, axis=0) + hs_ref[pl.ds(s, rows), :]) * gate).astype(BF16)
        y_ref[pl.ds(s, rows), :] = jnp.dot(to_tokens, y, preferred_element_type=F32).astype(y_ref.dtype)
        return h
    hf_ref[...] = lax.fori_loop(0, nchunks, fwd, h0f_ref[...])


def _bidir_lru(cu, cg, h0f, h0b, conv_w, conv_b, wg_bf, bg, lam, layer, bsz, n):
    w = C_WIDTH
    rows = min(LRU_ROWS, n)
    seq = pl.BlockSpec((n, w), lambda b: (b, 0))
    st = pl.BlockSpec((None, 1, w), lambda b: (b, 0, 0))
    vmem = 6 * n * w * 2 + 2 * n * w * 4 + (n + 2 * LRU_PAD) * w * 4 + (12 << 20)
    return pl.pallas_call(
        functools.partial(_lru_kernel, rows=rows),
        grid=(bsz,),
        in_specs=[seq, seq, st, st, _resident((CONV_W, w)), _resident((1, w)),
                  _layer_resident((w, 4 * w), layer), _resident((1, 4 * w)), _resident((2, w))],
        out_specs=(seq, st, st),
        out_shape=(jax.ShapeDtypeStruct((bsz * n, w), BF16),
                   jax.ShapeDtypeStruct((bsz, 1, w), F32), jax.ShapeDtypeStruct((bsz, 1, w), F32)),
        scratch_shapes=[pltpu.VMEM((w // LANES, n + 2 * LRU_PAD, LANES), F32),
                        pltpu.VMEM((n, w), F32), pltpu.VMEM((n, w), F32)],
        compiler_params=_params(("parallel",), vmem),
        name="bidir_lru",
    )(cu, cg, h0f, h0b, conv_w, conv_b, wg_bf, bg, lam)


def _merge_kernel(ya_ref, yb_ref, yc_ref, gt_ref, x_ref, gx_ref, g_ref, wbr_ref, wout_ref, o_ref):
    acc = None
    for n, y_ref in enumerate((ya_ref, yb_ref, yc_ref)):
        z = jnp.dot(y_ref[...], wbr_ref[n], preferred_element_type=F32)
        gate = jax.nn.sigmoid(gt_ref[:, n * D_MODEL:(n + 1) * D_MODEL].astype(F32))
        acc = gate * z if acc is None else acc + gate * z
    mix = jnp.dot(acc.astype(BF16), wout_ref[...], preferred_element_type=F32)
    o_ref[...] = x_ref[...] + gx_ref[...] * _rms(mix, g_ref[...])


def _merge(ya, yb, yc, gt, x2, gate_x, g1, wbr_bf, wout_bf, layer, tm):
    t, d = x2.shape
    nt = t // tm
    nb = gate_x.shape[0]
    tok = lambda w: pl.BlockSpec((tm, w), lambda i: (i, 0))
    modspec = pl.BlockSpec((None, 1, d), lambda i: (i // (nt // nb), 0, 0))
    vmem = (N_BRANCH * BRANCH_W * d + d * d) * 2 + 2 * tm * (3 * BRANCH_W + 3 * d) * 2 \
        + 4 * tm * d * 4 + 6 * tm * d * 4 + (8 << 20)
    return pl.pallas_call(
        _merge_kernel,
        grid=(nt,),
        in_specs=[tok(BRANCH_W), tok(BRANCH_W), tok(BRANCH_W), tok(N_BRANCH * d), tok(d), modspec,
                  _resident((1, d)), _layer_resident((N_BRANCH, BRANCH_W, d), layer),
                  _layer_resident((d, d), layer)],
        out_specs=tok(d),
        out_shape=jax.ShapeDtypeStruct((t, d), F32),
        compiler_params=_params(("parallel",), vmem),
        name="gated_merge",
    )(ya, yb, yc, gt, x2, gate_x, g1, wbr_bf, wout_bf)


def _ffn_kernel(x_ref, sh_ref, sc_ref, gx_ref, g2_ref, g3_ref, w1_ref, w2_ref, o_ref, *, n_chunks):
    x = x_ref[...]
    h = _rms(x, g2_ref[...])
    h = (h * (1.0 + sc_ref[...]) + sh_ref[...]).astype(BF16)
    tiles = FFN_HIDDEN // MXU_TILE
    bounds = [(-(-tiles * c // n_chunks)) * MXU_TILE for c in range(n_chunks + 1)]
    f = None
    for lo, hi in zip(bounds[:-1], bounds[1:]):
        gate = jnp.dot(h, w1_ref[:, lo:hi], preferred_element_type=F32)
        up = jnp.dot(h, w1_ref[:, FFN_HIDDEN + lo:FFN_HIDDEN + hi], preferred_element_type=F32)
        act = (gate * jax.nn.sigmoid(gate) * up).astype(BF16)
        part = jnp.dot(act, w2_ref[lo:hi, :], preferred_element_type=F32)
        f = part if f is None else f + part
    o_ref[...] = x + gx_ref[...] * _rms(f, g3_ref[...])


def _ffn(x2, shift, scale, gate_x, g2, g3, w1_bf, w2_bf, layer, tm):
    t, d = x2.shape
    nt = t // tm
    nb = shift.shape[0]
    tok = pl.BlockSpec((tm, d), lambda i: (i, 0))
    modspec = pl.BlockSpec((None, 1, d), lambda i: (i // (nt // nb), 0, 0))
    n_chunks = 2
    vmem = 3 * d * FFN_HIDDEN * 2 + 4 * tm * d * 4 + 3 * tm * (FFN_HIDDEN // n_chunks) * 4 \
        + 4 * tm * d * 4 + (8 << 20)
    return pl.pallas_call(
        functools.partial(_ffn_kernel, n_chunks=n_chunks),
        grid=(nt,),
        in_specs=[tok, modspec, modspec, modspec, _resident((1, d)), _resident((1, d)),
                  _layer_resident((d, 2 * FFN_HIDDEN), layer), _layer_resident((FFN_HIDDEN, d), layer)],
        out_specs=tok,
        out_shape=jax.ShapeDtypeStruct((t, d), F32),
        compiler_params=_params(("parallel",), vmem),
        name="swiglu_ffn",
    )(x2, shift, scale, gate_x, g2, g3, w1_bf, w2_bf)


def _rope_tables(n_tokens):
    rd = HEAD_DIM // 4
    t = jnp.arange(n_tokens)
    pos = jnp.stack([t // GRID_W, t % GRID_W], axis=-1).astype(F32)
    inv = 1.0 / (ROPE_THETA ** (jnp.arange(rd, dtype=F32) * 2.0 / (HEAD_DIM // 2)))
    ang = pos[:, :, None] * inv
    cos = jnp.cos(ang)[:, :, None, :]
    sin = jnp.sin(ang)[:, :, None, :]
    zeros = jnp.zeros_like(sin)
    cos64 = jnp.concatenate([cos, cos], axis=2).reshape(n_tokens, HEAD_DIM)
    hi64 = jnp.concatenate([zeros, sin], axis=2).reshape(n_tokens, HEAD_DIM)
    lo64 = jnp.concatenate([-sin, zeros], axis=2).reshape(n_tokens, HEAD_DIM)
    rep = LANES // HEAD_DIM
    return jnp.tile(cos64, (1, rep)), jnp.tile(hi64, (1, rep)), jnp.tile(lo64, (1, rep))


def _identity_tables(rows):
    z = jnp.zeros((rows, LANES), F32)
    return jnp.ones((rows, LANES), F32), z, z


def _gate_weights(lru_w):
    eye = jnp.eye(C_BLOCKS, dtype=lru_w.dtype)
    dense = jnp.einsum("ldgncf,nm->lncdgmf", lru_w, eye)
    return dense.reshape(lru_w.shape[0], C_WIDTH, 4 * C_WIDTH).astype(BF16)


def kernel(x, c, ctx, c_ctx, w_mod, b_mod, norm_g, w_in, attn_sink, diff_lambda, diff_subln, conv_w,
           conv_b, lru_w, lru_b, lru_lambda, w_branch, w_out, w_ffn_in, w_ffn_out):
    bsz, n_tok, d = x.shape
    n_ctx = ctx.shape[1]
    depth = w_mod.shape[0]
    tm = min(512, n_tok)
    tmc = min(256, n_ctx)

    mod_rows = 2 * SUBLANES
    c_rows = jnp.zeros((mod_rows, d), F32).at[:bsz].set(c).at[bsz].set(c_ctx)
    mods = _modulation(c_rows, w_mod, b_mod)

    cos, sin_hi, sin_lo = _rope_tables(n_tok)
    cos_c, sin_hi_c, sin_lo_c = _identity_tables(tmc)

    x2 = x.reshape(bsz * n_tok, d)
    cx2 = ctx.reshape(bsz * n_ctx, d)
    zero_state = jnp.zeros((bsz, 1, C_WIDTH), F32)

    w_in_bf = w_in.astype(BF16)
    wg_bf = _gate_weights(lru_w)
    wbr_bf = w_branch.astype(BF16)
    wout_bf = w_out.astype(BF16)
    w1_bf = w_ffn_in.astype(BF16)
    w2_bf = w_ffn_out.astype(BF16)

    for l in range(depth):
        need_ctx = l < depth - 1
        lam_init = 0.8 - 0.6 * math.exp(-0.3 * l)
        mx = [mods[l, :bsz, k * d:(k + 1) * d].reshape(bsz, 1, d) for k in range(6)]
        mc = [mods[l, bsz:bsz + 1, k * d:(k + 1) * d].reshape(1, 1, d) for k in range(6)]
        g = [norm_g[l, k].reshape(1, d).astype(F32) for k in range(4)]
        bg = lru_b[l].astype(F32).reshape(1, 4 * C_WIDTH)
        cw = conv_w[l].astype(F32)
        cb = conv_b[l].astype(F32).reshape(1, C_WIDTH)
        lam = lru_lambda[l].astype(F32)

        (qa_c, ka_c, va_c, qb_c, kb_c, vb_c, cu_c, cg_c, gt_c) = _in_proj(
            cx2, mc[0], mc[1], g[0], cos_c, sin_hi_c, sin_lo_c, w_in_bf, l, tmc)
        yc_c, hf_c, hb_c = _bidir_lru(cu_c, cg_c, zero_state, zero_state, cw, cb, wg_bf, bg, lam,
                                      l, bsz, n_ctx)

        (qa, ka, va, qb, kb, vb, cu, cg, gt) = _in_proj(
            x2, mx[0], mx[1], g[0], cos, sin_hi, sin_lo, w_in_bf, l, tm)
        ya = _window_attention(qa, ka, va, ka_c, va_c, attn_sink[l], bsz, n_tok, n_ctx)
        yb = _diff_attention(qb, kb, vb, kb_c, vb_c, diff_lambda[l], diff_subln[l], lam_init,
                             bsz, n_tok, n_ctx)
        yc, _, _ = _bidir_lru(cu, cg, hf_c, hb_c, cw, cb, wg_bf, bg, lam, l, bsz, n_tok)
        x2 = _merge(ya, yb, yc, gt, x2, mx[2], g[1], wbr_bf, wout_bf, l, tm)
        x2 = _ffn(x2, mx[3], mx[4], mx[5], g[2], g[3], w1_bf, w2_bf, l, tm)

        if need_ctx:
            ya_c = _context_attention(qa_c, ka_c, va_c, attn_sink[l], bsz, n_ctx)
            yb_c = _diff_attention(qb_c, None, None, kb_c, vb_c, diff_lambda[l], diff_subln[l],
                                   lam_init, bsz, 0, n_ctx)
            cx2 = _merge(ya_c, yb_c, yc_c, gt_c, cx2, mc[2], g[1], wbr_bf, wout_bf, l, tmc)
            cx2 = _ffn(cx2, mc[3], mc[4], mc[5], g[2], g[3], w1_bf, w2_bf, l, tmc)

    return x2.reshape(bsz, n_tok, d)
```

```python
import functools
import math

import jax
import jax.numpy as jnp
from jax import lax
from jax.experimental import pallas as pl
from jax.experimental.pallas import tpu as pltpu

F32 = jnp.float32
BF16 = jnp.bfloat16

D_MODEL = 1024
GRID_W = 64
HEAD_DIM = 64
ROPE_THETA = 10000.0
EPS = 1e-6
A_HEADS = 8
A_KV_HEADS = 2
A_GROUPS = A_HEADS // A_KV_HEADS
WINDOW = 128
B_HEADS = 4
B_VDIM = 2 * HEAD_DIM
C_WIDTH = 512
C_BLOCKS = 8
C_BW = C_WIDTH // C_BLOCKS
CONV_W = 4
LRU_C = 8.0
N_BRANCH = 3
BRANCH_W = 512
FFN_HIDDEN = -(-8 * D_MODEL // (3 * 256)) * 256

O_AQ = 0
O_AK = O_AQ + A_HEADS * HEAD_DIM
O_AV = O_AK + A_KV_HEADS * HEAD_DIM
O_BQ = O_AV + A_KV_HEADS * HEAD_DIM
O_BK = O_BQ + B_HEADS * 2 * HEAD_DIM
O_BV = O_BK + B_HEADS * 2 * HEAD_DIM
O_CU = O_BV + B_HEADS * B_VDIM
O_CG = O_CU + C_WIDTH
O_GT = O_CG + C_WIDTH
IN_WIDTH = O_GT + N_BRANCH * D_MODEL

LANES = 128
SUBLANES = 8
BF16_ROWS = 16
MXU_TILE = 256
VMEM_CAP_BYTES = 60 * 1024 * 1024
NEG_BIG = -1e30
Q_BLOCK = 128
WIN_Q_BLOCKS = 8
DIFF_TQ = 512
DIFF_TK = 512
DIFF_HEADS_PER_STEP = 2
LRU_ROWS = 512
LRU_PAD = 8
LOG2E = math.log2(math.e)
Q_SCALE = HEAD_DIM ** -0.5 * LOG2E


def _params(semantics, vmem_bytes):
    return pltpu.CompilerParams(dimension_semantics=semantics,
                                vmem_limit_bytes=min(int(vmem_bytes), VMEM_CAP_BYTES))


def _resident(shape):
    nd = len(shape)
    return pl.BlockSpec(shape, lambda *_: (0,) * nd, pipeline_mode=pl.Buffered(1))


def _layer_resident(shape, layer):
    nd = len(shape)
    return pl.BlockSpec((None,) + tuple(shape), lambda *_: (layer,) + (0,) * nd,
                        pipeline_mode=pl.Buffered(1))


def _rms(x, g):
    return x * lax.rsqrt(jnp.mean(x * x, axis=-1, keepdims=True) + EPS) * g


def _mod_kernel(c_ref, w_ref, b_ref, o_ref):
    c = c_ref[...]
    s = c * jax.nn.sigmoid(c)
    o_ref[...] = jnp.dot(s, w_ref[...], preferred_element_type=F32,
                         precision=lax.Precision.HIGHEST) + b_ref[...]


def _modulation(c_rows, w_mod, b_mod):
    depth, d, n = w_mod.shape
    rows = c_rows.shape[0]
    tn = 1536
    return pl.pallas_call(
        _mod_kernel,
        grid=(depth, n // tn),
        in_specs=[pl.BlockSpec((rows, d), lambda l, j: (0, 0)),
                  pl.BlockSpec((None, d, tn), lambda l, j: (l, 0, j)),
                  pl.BlockSpec((None, 1, tn), lambda l, j: (l, 0, j))],
        out_specs=pl.BlockSpec((None, rows, tn), lambda l, j: (l, 0, j)),
        out_shape=jax.ShapeDtypeStruct((depth, rows, n), F32),
        compiler_params=_params(("parallel", "parallel"), 32 << 20),
        name="modulation",
    )(c_rows, w_mod, b_mod.reshape(depth, 1, n))


def _in_proj_kernel(x_ref, sh_ref, sc_ref, g_ref, cos_ref, sa_ref, sb_ref, w_ref, *out_refs, queries):
    if queries:
        qa_ref, ka_ref, va_ref, qb_ref, kb_ref, vb_ref, cu_ref, cg_ref, gt_ref = out_refs
    else:
        ka_ref, va_ref, kb_ref, vb_ref, cu_ref, cg_ref = out_refs
    h = _rms(x_ref[...], g_ref[...])
    h = (h * (1.0 + sc_ref[...]) + sh_ref[...]).astype(BF16)

    def proj(lo, width):
        return jnp.dot(h, w_ref[:, lo:lo + width], preferred_element_type=F32)

    cos = cos_ref[...]
    sin_hi = sa_ref[...]
    sin_lo = sb_ref[...]

    def rope(a):
        outs = []
        for k in range(a.shape[1] // LANES):
            blk = a[:, k * LANES:(k + 1) * LANES]
            outs.append(blk * cos + pltpu.roll(blk, HEAD_DIM // 4, 1) * sin_hi
                        + pltpu.roll(blk, LANES - HEAD_DIM // 4, 1) * sin_lo)
        return jnp.concatenate(outs, axis=1)

    kv_a = proj(O_AK, O_BQ - O_AK)
    ka_ref[...] = rope(kv_a[:, :O_AV - O_AK]).astype(BF16)
    va_ref[...] = kv_a[:, O_AV - O_AK:].T.astype(BF16)
    kb_ref[...] = rope(proj(O_BK, O_BV - O_BK)).astype(BF16)
    vb_ref[...] = proj(O_BV, O_CU - O_BV).T.astype(BF16)
    cu_ref[...] = proj(O_CU, C_WIDTH).astype(BF16)
    cg_ref[...] = proj(O_CG, C_WIDTH).astype(BF16)
    if queries:
        qa_ref[...] = (rope(proj(O_AQ, O_AK - O_AQ)) * Q_SCALE).T.astype(BF16)
        qb_ref[...] = (rope(proj(O_BQ, O_BK - O_BQ)) * Q_SCALE).T.astype(BF16)
        for k in range(N_BRANCH):
            gt_ref[:, k * D_MODEL:(k + 1) * D_MODEL] = proj(O_GT + k * D_MODEL, D_MODEL).astype(BF16)


def _in_proj(x2, shift, scale, g0, cos, sin_hi, sin_lo, w_bf, layer, tm, queries=True):
    t, d = x2.shape
    nt = t // tm
    nb = shift.shape[0]
    npos = cos.shape[0] // tm
    tok = lambda w: pl.BlockSpec((tm, w), lambda i: (i, 0))
    feat = lambda w: pl.BlockSpec((w, tm), lambda i: (0, i))
    modspec = pl.BlockSpec((None, 1, d), lambda i: (i // (nt // nb), 0, 0))
    tabspec = pl.BlockSpec((tm, LANES), lambda i: (i % npos, 0))
    wq, wkv = A_HEADS * HEAD_DIM, A_KV_HEADS * HEAD_DIM
    wb = B_HEADS * B_VDIM
    out_shape = (
        jax.ShapeDtypeStruct((wq, t), BF16), jax.ShapeDtypeStruct((t, wkv), BF16),
        jax.ShapeDtypeStruct((wkv, t), BF16), jax.ShapeDtypeStruct((wb, t), BF16),
        jax.ShapeDtypeStruct((t, wb), BF16), jax.ShapeDtypeStruct((wb, t), BF16),
        jax.ShapeDtypeStruct((t, C_WIDTH), BF16), jax.ShapeDtypeStruct((t, C_WIDTH), BF16),
        jax.ShapeDtypeStruct((t, N_BRANCH * D_MODEL), BF16))
    out_specs = (feat(wq), tok(wkv), feat(wkv), feat(wb), tok(wb), feat(wb),
                 tok(C_WIDTH), tok(C_WIDTH), tok(N_BRANCH * D_MODEL))
    if not queries:
        keep = (1, 2, 4, 5, 6, 7)
        out_shape = tuple(out_shape[i] for i in keep)
        out_specs = tuple(out_specs[i] for i in keep)
    vmem = d * IN_WIDTH * 2 + 2 * tm * d * 4 + 2 * tm * IN_WIDTH * 2 + 6 * tm * d * 4 + (8 << 20)
    return pl.pallas_call(
        functools.partial(_in_proj_kernel, queries=queries),
        grid=(nt,),
        in_specs=[tok(d), modspec, modspec, _resident((1, d)), tabspec, tabspec, tabspec,
                  _layer_resident((d, IN_WIDTH), layer)],
        out_specs=out_specs,
        out_shape=out_shape,
        compiler_params=_params(("parallel",), vmem),
        name="in_proj",
    )(x2, shift, scale, g0, cos, sin_hi, sin_lo, w_bf)


def _sink_scores(k_all, qt):
    tq = qt.shape[1]
    zero = jnp.zeros((HEAD_DIM, tq), qt.dtype)
    scores = []
    for j in range(A_KV_HEADS):
        cols = []
        for g in range(A_GROUPS):
            hd = j * A_GROUPS + g
            qh = qt[hd * HEAD_DIM:(hd + 1) * HEAD_DIM, :]
            cols.append(jnp.concatenate([qh, zero] if j == 0 else [zero, qh], axis=0))
        qz = jnp.concatenate(cols, axis=1)
        scores.append(jnp.dot(k_all, qz, preferred_element_type=F32))
    return scores


def _sink_softmax_pv(scores, vt_all, sink_row, bias, n_win):
    tq = scores[0].shape[1] // A_GROUPS
    outs = []
    ones = jnp.ones((BF16_ROWS, vt_all.shape[1]), BF16)
    for j in range(A_KV_HEADS):
        s = scores[j]
        sink = sink_row[:, j * A_GROUPS * tq:(j + 1) * A_GROUPS * tq] * LOG2E
        if n_win:
            sw = s[:n_win] + jnp.concatenate([bias] * A_GROUPS, axis=1)
            sc = s[n_win:]
            m = jnp.maximum(jnp.maximum(jnp.max(sw, axis=0, keepdims=True),
                                        jnp.max(sc, axis=0, keepdims=True)), sink)
            p = jnp.concatenate([jnp.exp2(sw - m), jnp.exp2(sc - m)], axis=0).astype(BF16)
        else:
            m = jnp.maximum(jnp.max(s, axis=0, keepdims=True), sink)
            p = jnp.exp2(s - m).astype(BF16)
        vt_aug = jnp.concatenate([vt_all[j * HEAD_DIM:(j + 1) * HEAD_DIM, :], ones], axis=0)
        o = jnp.dot(vt_aug, p, preferred_element_type=F32)
        l = o[HEAD_DIM:HEAD_DIM + 1] + jnp.exp2(sink - m)
        o = o[:HEAD_DIM] * (1.0 / l)
        for g in range(A_GROUPS):
            outs.append(o[:, g * tq:(g + 1) * tq])
    return jnp.concatenate(outs, axis=0)


def _win_attn_kernel(q_ref, kp_ref, kc_ref, kn_ref, vp_ref, vc_ref, vn_ref, kx_ref, vx_ref,
                     sink_ref, o_ref, *, n_tok):
    tq = Q_BLOCK
    q_blocks = q_ref.shape[1] // tq
    first = pl.program_id(1) * q_blocks
    k_cat = jnp.concatenate([kp_ref[...], kc_ref[...], kn_ref[...]], axis=0)
    vt_cat = jnp.concatenate([vp_ref[...], vc_ref[...], vn_ref[...]], axis=1)
    kx = kx_ref[...]
    vx = vx_ref[...]
    n_win = 3 * tq
    r = lax.broadcasted_iota(jnp.int32, (n_win, tq), 0)
    c = lax.broadcasted_iota(jnp.int32, (n_win, tq), 1)
    band = jnp.where(jnp.abs(r - tq - c) <= WINDOW, 0.0, NEG_BIG).astype(F32)

    def scores(b):
        k_all = jnp.concatenate([k_cat[b * tq:(b + 3) * tq], kx], axis=0)
        return _sink_scores(k_all, q_ref[:, b * tq:(b + 1) * tq])

    s_next = scores(0)
    for b in range(q_blocks):
        s_cur = s_next
        if b + 1 < q_blocks:
            s_next = scores(b + 1)
        kpos = (first + b - 1) * tq + r
        bias = band + jnp.where(kpos >= 0, 0.0, NEG_BIG) + jnp.where(kpos < n_tok, 0.0, NEG_BIG)
        vt_all = jnp.concatenate([vt_cat[:, b * tq:(b + 3) * tq], vx], axis=1)
        yt = _sink_softmax_pv(s_cur, vt_all, sink_ref[...], bias, n_win)
        o_ref[b * tq:(b + 1) * tq, :] = yt.T.astype(o_ref.dtype)


def _ctx_attn_kernel(q_ref, kx_ref, vx_ref, sink_ref, o_ref):
    yt = _sink_softmax_pv(_sink_scores(kx_ref[...], q_ref[...]), vx_ref[...], sink_ref[...], None, 0)
    o_ref[...] = yt.T.astype(o_ref.dtype)


def _sink_row(sink, tq):
    return jnp.repeat(sink.astype(F32), tq).reshape(1, A_HEADS * tq)


def _window_attention(qt, k, vt, kx, vxt, sink, bsz, n_tok, n_ctx):
    tq = Q_BLOCK
    nq = n_tok // tq
    qb = min(WIN_Q_BLOCKS, nq)
    ns = nq // qb
    wq, wkv = A_HEADS * HEAD_DIM, A_KV_HEADS * HEAD_DIM
    prev = lambda b, n: b * nq + jnp.maximum(n * qb - 1, 0)
    cur = lambda b, n: b * ns + n
    nxt = lambda b, n: b * nq + jnp.minimum(n * qb + qb, nq - 1)
    edge_k = lambda f: pl.BlockSpec((tq, wkv), lambda b, n: (f(b, n), 0))
    edge_v = lambda f: pl.BlockSpec((wkv, tq), lambda b, n: (0, f(b, n)))
    return pl.pallas_call(
        functools.partial(_win_attn_kernel, n_tok=n_tok),
        grid=(bsz, ns),
        in_specs=[pl.BlockSpec((wq, qb * tq), lambda b, n: (0, cur(b, n))),
                  edge_k(prev), pl.BlockSpec((qb * tq, wkv), lambda b, n: (cur(b, n), 0)), edge_k(nxt),
                  edge_v(prev), pl.BlockSpec((wkv, qb * tq), lambda b, n: (0, cur(b, n))), edge_v(nxt),
                  pl.BlockSpec((n_ctx, wkv), lambda b, n: (b, 0)),
                  pl.BlockSpec((wkv, n_ctx), lambda b, n: (0, b)),
                  pl.BlockSpec((1, A_HEADS * tq), lambda b, n: (0, 0))],
        out_specs=pl.BlockSpec((qb * tq, wq), lambda b, n: (cur(b, n), 0)),
        out_shape=jax.ShapeDtypeStruct((bsz * n_tok, wq), BF16),
        compiler_params=_params(("parallel", "parallel"), 32 << 20),
        name="window_attention",
    )(qt, k, k, k, vt, vt, vt, kx, vxt, _sink_row(sink, tq))


def _context_attention(qt, kx, vxt, sink, bsz, n_ctx):
    tq = Q_BLOCK
    nq = n_ctx // tq
    wq, wkv = A_HEADS * HEAD_DIM, A_KV_HEADS * HEAD_DIM
    return pl.pallas_call(
        _ctx_attn_kernel,
        grid=(bsz, nq),
        in_specs=[pl.BlockSpec((wq, tq), lambda b, n: (0, b * nq + n)),
                  pl.BlockSpec((n_ctx, wkv), lambda b, n: (b, 0)),
                  pl.BlockSpec((wkv, n_ctx), lambda b, n: (0, b)),
                  pl.BlockSpec((1, A_HEADS * tq), lambda b, n: (0, 0))],
        out_specs=pl.BlockSpec((tq, wq), lambda b, n: (b * nq + n, 0)),
        out_shape=jax.ShapeDtypeStruct((bsz * n_ctx, wq), BF16),
        compiler_params=_params(("parallel", "parallel"), 32 << 20),
        name="context_attention",
    )(qt, kx, vxt, _sink_row(sink, tq))


def _diff_attn_kernel(*refs, n_lat_chunks, tk, lam_init, heads):
    if n_lat_chunks:
        q_ref, k_ref, v_ref, kx_ref, vx_ref, dl_ref, g_ref, o_ref = refs
    else:
        q_ref, kx_ref, vx_ref, dl_ref, g_ref, o_ref = refs
    tq = q_ref.shape[1]
    hw = B_VDIM
    row = lax.broadcasted_iota(jnp.int32, (hw, tq), 0)
    zero = jnp.zeros((hw, tq), q_ref.dtype)
    n = n_lat_chunks + 1

    def head_slice(hd):
        return slice(hd * hw, (hd + 1) * hw)

    qz = []
    for hd in range(heads):
        qt = q_ref[head_slice(hd), :]
        qz.append(jnp.concatenate([jnp.where(row < HEAD_DIM, qt, zero),
                                   jnp.where(row >= HEAD_DIM, qt, zero)], axis=1))

    def keys(hd, i):
        if i < n_lat_chunks:
            return k_ref[i * tk:(i + 1) * tk, head_slice(hd)]
        return kx_ref[:, head_slice(hd)]

    def values(hd, i):
        v = v_ref[head_slice(hd), i * tk:(i + 1) * tk] if i < n_lat_chunks else vx_ref[head_slice(hd), :]
        return jnp.concatenate([v, jnp.ones((BF16_ROWS, v.shape[1]), BF16)], axis=0)

    def scores(hd, i):
        s = jnp.dot(keys(hd, i), qz[hd], preferred_element_type=F32)
        return s, jnp.max(s, axis=0, keepdims=True)

    m = [jnp.full((1, 2 * tq), NEG_BIG, F32) for _ in range(heads)]
    acc = [jnp.zeros((hw + BF16_ROWS, 2 * tq), F32) for _ in range(heads)]
    nxt = [scores(hd, 0) for hd in range(heads)]
    e_prev = [None] * heads
    alpha = [None] * heads
    for c in range(n):
        cur = list(nxt)
        for hd in range(heads):
            if c + 1 < n:
                nxt[hd] = scores(hd, c + 1)
            if c:
                acc[hd] = alpha[hd] * acc[hd] + jnp.dot(values(hd, c - 1), e_prev[hd],
                                                        preferred_element_type=F32)
        for hd in range(heads):
            s, smax = cur[hd]
            m_new = jnp.maximum(m[hd], smax)
            alpha[hd] = jnp.exp2(m[hd] - m_new)
            e_prev[hd] = jnp.exp2(s - m_new).astype(BF16)
            m[hd] = m_new

    dl = dl_ref[...]
    lam = (jnp.exp(jnp.sum(dl[0:1] * dl[1:2], axis=1, keepdims=True))
           - jnp.exp(jnp.sum(dl[2:3] * dl[3:4], axis=1, keepdims=True)) + lam_init)
    for hd in range(heads):
        a = alpha[hd] * acc[hd] + jnp.dot(values(hd, n - 1), e_prev[hd], preferred_element_type=F32)
        o = a[:hw] * (1.0 / a[hw:hw + 1])
        y = (o[:, :tq] - lam * o[:, tq:]).T
        o_ref[:, head_slice(hd)] = (_rms(y, g_ref[...]) * (1.0 - lam_init)).astype(o_ref.dtype)


def _diff_attention(qt, k, vt, kx, vxt, diff_lambda, subln, lam_init, bsz, n_tok, n_ctx):
    nq_tok = n_tok if n_tok else n_ctx
    tq = min(DIFF_TQ, nq_tok)
    nqt = nq_tok // tq
    tk = min(DIFF_TK, n_tok) if n_tok else 0
    heads = DIFF_HEADS_PER_STEP
    hw = heads * B_VDIM
    in_specs = [pl.BlockSpec((hw, tq), lambda b, h, i: (h, b * nqt + i))]
    args = [qt]
    if n_tok:
        in_specs += [pl.BlockSpec((n_tok, hw), lambda b, h, i: (b, h)),
                     pl.BlockSpec((hw, n_tok), lambda b, h, i: (h, b))]
        args += [k, vt]
    in_specs += [pl.BlockSpec((n_ctx, hw), lambda b, h, i: (b, h)),
                 pl.BlockSpec((hw, n_ctx), lambda b, h, i: (h, b)),
                 pl.BlockSpec((4, HEAD_DIM), lambda b, h, i: (0, 0)),
                 pl.BlockSpec((1, B_VDIM), lambda b, h, i: (0, 0))]
    args += [kx, vxt, diff_lambda.astype(F32), subln.astype(F32).reshape(1, B_VDIM)]
    return pl.pallas_call(
        functools.partial(_diff_attn_kernel, n_lat_chunks=(n_tok // tk if n_tok else 0), tk=tk,
                          lam_init=lam_init, heads=heads),
        grid=(bsz, B_HEADS // heads, nqt),
        in_specs=in_specs,
        out_specs=pl.BlockSpec((tq, hw), lambda b, h, i: (b * nqt + i, h)),
        out_shape=jax.ShapeDtypeStruct((bsz * nq_tok, B_HEADS * B_VDIM), BF16),
        compiler_params=_params(("parallel", "parallel", "parallel"), 48 << 20),
        name="diff_attention",
    )(*args)


def _scan_segments(a, b, h_in, reverse):
    rows, w = a.shape
    seg = rows // SUBLANES
    order = range(seg - 1, -1, -1) if reverse else range(seg)
    h_loc = [None] * seg
    a_cum = [None] * seg
    h = prod = None
    for i in order:
        ai = a[i * SUBLANES:(i + 1) * SUBLANES]
        bi = b[i * SUBLANES:(i + 1) * SUBLANES]
        h = bi if h is None else ai * h + bi
        prod = ai if prod is None else ai * prod
        h_loc[i] = h
        a_cum[i] = prod
    sub = lax.broadcasted_iota(jnp.int32, (SUBLANES, w), 0)
    e, p = h, prod
    for d in (1, 2, 4):
        keep = (sub < SUBLANES - d) if reverse else (sub >= d)
        shift = SUBLANES - d if reverse else d
        p_s = jnp.where(keep, pltpu.roll(p, shift, 0), 1.0)
        e_s = jnp.where(keep, pltpu.roll(e, shift, 0), 0.0)
        e = e + p * e_s
        p = p * p_s
    seg_out = e + p * h_in
    if reverse:
        seg_in = jnp.where(sub < SUBLANES - 1, pltpu.roll(seg_out, SUBLANES - 1, 0), h_in)
        h_out = seg_out[0:1]
    else:
        seg_in = jnp.where(sub >= 1, pltpu.roll(seg_out, 1, 0), h_in)
        h_out = seg_out[SUBLANES - 1:SUBLANES]
    return [h_loc[i] + a_cum[i] * seg_in for i in range(seg)], h_out


def _lru_kernel(cu_ref, cg_ref, h0f_ref, h0b_ref, cw_ref, cb_ref, wg_ref, bg_ref, lam_ref,
                y_ref, hf_ref, hb_ref, xpad_ref, hs_ref, u_ref, *, rows):
    n = cu_ref.shape[0]
    nchunks = n // rows
    w = C_WIDTH
    seg = rows // SUBLANES
    slabs = w // LANES

    def load(ref, idx):
        return jnp.concatenate([ref[j, idx, :] for j in range(slabs)], axis=1)

    def store(ref, idx, val):
        for j in range(slabs):
            ref[j, idx, :] = val[:, j * LANES:(j + 1) * LANES]

    store(xpad_ref, pl.ds(0, LRU_PAD), jnp.zeros((LRU_PAD, w), F32))
    store(xpad_ref, pl.ds(LRU_PAD + n, LRU_PAD), jnp.zeros((LRU_PAD, w), F32))

    def fill(ci, carry):
        s = pl.multiple_of(ci * rows, rows)
        store(xpad_ref, pl.ds(s + LRU_PAD, rows), cu_ref[pl.ds(s, rows), :].astype(F32))
        return carry
    lax.fori_loop(0, nchunks, fill, 0)

    lam = lam_ref[...]
    decay = LRU_C * (jnp.maximum(-lam, 0.0) + jnp.log1p(jnp.exp(-jnp.abs(lam))))
    cw = cw_ref[...]
    cb = cb_ref[...]

    def conv(s):
        first = LRU_PAD - CONV_W // 2
        taps = [load(xpad_ref, pl.ds(s + (first + q), SUBLANES, stride=seg))
                for q in range(seg + CONV_W - 1)]
        groups = []
        for i in range(seg):
            acc = cb
            for k in range(CONV_W):
                acc = acc + cw[k:k + 1] * taps[i + k]
            groups.append(acc)
        return jnp.concatenate(groups, axis=0)

    def coeffs(u, d):
        g = jnp.dot(u.astype(BF16), wg_ref[:, d * 2 * w:(d + 1) * 2 * w],
                    preferred_element_type=F32) + bg_ref[:, d * 2 * w:(d + 1) * 2 * w]
        r = jax.nn.sigmoid(g[:, :w])
        i_gate = jax.nn.sigmoid(g[:, w:])
        z = decay[d:d + 1] * r
        a = jnp.exp(-z)
        v = (1.0 + a * a) * jnp.tanh(z)
        root = jnp.where(v > 0.0, v * lax.rsqrt(v), 0.0)
        return a, root * i_gate * u

    r_i = lax.broadcasted_iota(jnp.int32, (rows, rows), 0)
    c_i = lax.broadcasted_iota(jnp.int32, (rows, rows), 1)
    to_tokens = jnp.where(c_i == (r_i % seg) * SUBLANES + r_i // seg, 1.0, 0.0).astype(BF16)
    to_segments = jnp.where(r_i == (c_i % seg) * SUBLANES + c_i // seg, 1.0, 0.0).astype(BF16)

    def bwd(ci, h):
        s = pl.multiple_of((nchunks - 1 - ci) * rows, rows)
        u = conv(s)
        u_ref[pl.ds(s, rows), :] = u
        a, bx = coeffs(u, 1)
        hs, h = _scan_segments(a, bx, h, True)
        hs_ref[pl.ds(s, rows), :] = jnp.concatenate(hs, axis=0)
        return h
    hb_ref[...] = lax.fori_loop(0, nchunks, bwd, h0b_ref[...])

    def fwd(ci, h):
        s = pl.multiple_of(ci * rows, rows)
        a, bx = coeffs(u_ref[pl.ds(s, rows), :], 0)
        hs, h = _scan_segments(a, bx, h, False)
        gate = jax.nn.gelu(jnp.dot(to_segments, cg_ref[pl.ds(s, rows), :], preferred_element_type=F32))
        y = ((jnp.concatenate(hs, axis=0) + hs_ref[pl.ds(s, rows), :]) * gate).astype(BF16)
        y_ref[pl.ds(s, rows), :] = jnp.dot(to_tokens, y, preferred_element_type=F32).astype(y_ref.dtype)
        return h
    hf_ref[...] = lax.fori_loop(0, nchunks, fwd, h0f_ref[...])


def _bidir_lru(cu, cg, h0f, h0b, conv_w, conv_b, wg_bf, bg, lam, layer, bsz, n):
    w = C_WIDTH
    rows = min(LRU_ROWS, n)
    seq = pl.BlockSpec((n, w), lambda b: (b, 0))
    st = pl.BlockSpec((None, 1, w), lambda b: (b, 0, 0))
    vmem = 6 * n * w * 2 + 2 * n * w * 4 + (n + 2 * LRU_PAD) * w * 4 + (12 << 20)
    return pl.pallas_call(
        functools.partial(_lru_kernel, rows=rows),
        grid=(bsz,),
        in_specs=[seq, seq, st, st, _resident((CONV_W, w)), _resident((1, w)),
                  _layer_resident((w, 4 * w), layer), _resident((1, 4 * w)), _resident((2, w))],
        out_specs=(seq, st, st),
        out_shape=(jax.ShapeDtypeStruct((bsz * n, w), BF16),
                   jax.ShapeDtypeStruct((bsz, 1, w), F32), jax.ShapeDtypeStruct((bsz, 1, w), F32)),
        scratch_shapes=[pltpu.VMEM((w // LANES, n + 2 * LRU_PAD, LANES), F32),
                        pltpu.VMEM((n, w), F32), pltpu.VMEM((n, w), F32)],
        compiler_params=_params(("parallel",), vmem),
        name="bidir_lru",
    )(cu, cg, h0f, h0b, conv_w, conv_b, wg_bf, bg, lam)


def _merge_kernel(ya_ref, yb_ref, yc_ref, gt_ref, x_ref, gx_ref, g_ref, wbr_ref, wout_ref, o_ref):
    acc = None
    for n, y_ref in enumerate((ya_ref, yb_ref, yc_ref)):
        z = jnp.dot(y_ref[...], wbr_ref[n], preferred_element_type=F32)
        gate = jax.nn.sigmoid(gt_ref[:, n * D_MODEL:(n + 1) * D_MODEL].astype(F32))
        acc = gate * z if acc is None else acc + gate * z
    mix = jnp.dot(acc.astype(BF16), wout_ref[...], preferred_element_type=F32)
    o_ref[...] = x_ref[...] + gx_ref[...] * _rms(mix, g_ref[...])


def _merge(ya, yb, yc, gt, x2, gate_x, g1, wbr_bf, wout_bf, layer, tm):
    t, d = x2.shape
    nt = t // tm
    nb = gate_x.shape[0]
    tok = lambda w: pl.BlockSpec((tm, w), lambda i: (i, 0))
    modspec = pl.BlockSpec((None, 1, d), lambda i: (i // (nt // nb), 0, 0))
    vmem = (N_BRANCH * BRANCH_W * d + d * d) * 2 + 2 * tm * (3 * BRANCH_W + 3 * d) * 2 \
        + 4 * tm * d * 4 + 6 * tm * d * 4 + (8 << 20)
    return pl.pallas_call(
        _merge_kernel,
        grid=(nt,),
        in_specs=[tok(BRANCH_W), tok(BRANCH_W), tok(BRANCH_W), tok(N_BRANCH * d), tok(d), modspec,
                  _resident((1, d)), _layer_resident((N_BRANCH, BRANCH_W, d), layer),
                  _layer_resident((d, d), layer)],
        out_specs=tok(d),
        out_shape=jax.ShapeDtypeStruct((t, d), F32),
        compiler_params=_params(("parallel",), vmem),
        name="gated_merge",
    )(ya, yb, yc, gt, x2, gate_x, g1, wbr_bf, wout_bf)


def _ffn_kernel(x_ref, sh_ref, sc_ref, gx_ref, g2_ref, g3_ref, w1_ref, w2_ref, o_ref, *, n_chunks):
    x = x_ref[...]
    h = _rms(x, g2_ref[...])
    h = (h * (1.0 + sc_ref[...]) + sh_ref[...]).astype(BF16)
    tiles = FFN_HIDDEN // MXU_TILE
    bounds = [(-(-tiles * c // n_chunks)) * MXU_TILE for c in range(n_chunks + 1)]
    f = None
    for lo, hi in zip(bounds[:-1], bounds[1:]):
        gate = jnp.dot(h, w1_ref[:, lo:hi], preferred_element_type=F32)
        up = jnp.dot(h, w1_ref[:, FFN_HIDDEN + lo:FFN_HIDDEN + hi], preferred_element_type=F32)
        act = (gate * jax.nn.sigmoid(gate) * up).astype(BF16)
        part = jnp.dot(act, w2_ref[lo:hi, :], preferred_element_type=F32)
        f = part if f is None else f + part
    o_ref[...] = x + gx_ref[...] * _rms(f, g3_ref[...])


def _ffn(x2, shift, scale, gate_x, g2, g3, w1_bf, w2_bf, layer, tm):
    t, d = x2.shape
    nt = t // tm
    nb = shift.shape[0]
    tok = pl.BlockSpec((tm, d), lambda i: (i, 0))
    modspec = pl.BlockSpec((None, 1, d), lambda i: (i // (nt // nb), 0, 0))
    n_chunks = 2
    vmem = 3 * d * FFN_HIDDEN * 2 + 4 * tm * d * 4 + 3 * tm * (FFN_HIDDEN // n_chunks) * 4 \
        + 4 * tm * d * 4 + (8 << 20)
    return pl.pallas_call(
        functools.partial(_ffn_kernel, n_chunks=n_chunks),
        grid=(nt,),
        in_specs=[tok, modspec, modspec, modspec, _resident((1, d)), _resident((1, d)),
                  _layer_resident((d, 2 * FFN_HIDDEN), layer), _layer_resident((FFN_HIDDEN, d), layer)],
        out_specs=tok,
        out_shape=jax.ShapeDtypeStruct((t, d), F32),
        compiler_params=_params(("parallel",), vmem),
        name="swiglu_ffn",
    )(x2, shift, scale, gate_x, g2, g3, w1_bf, w2_bf)


def _rope_tables(n_tokens):
    rd = HEAD_DIM // 4
    t = jnp.arange(n_tokens)
    pos = jnp.stack([t // GRID_W, t % GRID_W], axis=-1).astype(F32)
    inv = 1.0 / (ROPE_THETA ** (jnp.arange(rd, dtype=F32) * 2.0 / (HEAD_DIM // 2)))
    ang = pos[:, :, None] * inv
    cos = jnp.cos(ang)[:, :, None, :]
    sin = jnp.sin(ang)[:, :, None, :]
    zeros = jnp.zeros_like(sin)
    cos64 = jnp.concatenate([cos, cos], axis=2).reshape(n_tokens, HEAD_DIM)
    hi64 = jnp.concatenate([zeros, sin], axis=2).reshape(n_tokens, HEAD_DIM)
    lo64 = jnp.concatenate([-sin, zeros], axis=2).reshape(n_tokens, HEAD_DIM)
    rep = LANES // HEAD_DIM
    return jnp.tile(cos64, (1, rep)), jnp.tile(hi64, (1, rep)), jnp.tile(lo64, (1, rep))


def _identity_tables(rows):
    z = jnp.zeros((rows, LANES), F32)
    return jnp.ones((rows, LANES), F32), z, z


def _gate_weights(lru_w):
    eye = jnp.eye(C_BLOCKS, dtype=lru_w.dtype)
    dense = jnp.einsum("ldgncf,nm->lncdgmf", lru_w, eye)
    return dense.reshape(lru_w.shape[0], C_WIDTH, 4 * C_WIDTH).astype(BF16)


def kernel(x, c, ctx, c_ctx, w_mod, b_mod, norm_g, w_in, attn_sink, diff_lambda, diff_subln, conv_w,
           conv_b, lru_w, lru_b, lru_lambda, w_branch, w_out, w_ffn_in, w_ffn_out):
    bsz, n_tok, d = x.shape
    n_ctx = ctx.shape[1]
    depth = w_mod.shape[0]
    tm = min(512, n_tok)
    tmc = min(256, n_ctx)

    mod_rows = 2 * SUBLANES
    c_rows = jnp.zeros((mod_rows, d), F32).at[:bsz].set(c).at[bsz].set(c_ctx)
    mods = _modulation(c_rows, w_mod, b_mod)

    cos, sin_hi, sin_lo = _rope_tables(n_tok)
    cos_c, sin_hi_c, sin_lo_c = _identity_tables(tmc)

    x2 = x.reshape(bsz * n_tok, d)
    cx2 = ctx.reshape(bsz * n_ctx, d)
    zero_state = jnp.zeros((bsz, 1, C_WIDTH), F32)

    w_in_bf = w_in.astype(BF16)
    wg_bf = _gate_weights(lru_w)
    wbr_bf = w_branch.astype(BF16)
    wout_bf = w_out.astype(BF16)
    w1_bf = w_ffn_in.astype(BF16)
    w2_bf = w_ffn_out.astype(BF16)

    for l in range(depth):
        need_ctx = l < depth - 1
        lam_init = 0.8 - 0.6 * math.exp(-0.3 * l)
        mx = [mods[l, :bsz, k * d:(k + 1) * d].reshape(bsz, 1, d) for k in range(6)]
        mc = [mods[l, bsz:bsz + 1, k * d:(k + 1) * d].reshape(1, 1, d) for k in range(6)]
        g = [norm_g[l, k].reshape(1, d).astype(F32) for k in range(4)]
        bg = lru_b[l].astype(F32).reshape(1, 4 * C_WIDTH)
        cw = conv_w[l].astype(F32)
        cb = conv_b[l].astype(F32).reshape(1, C_WIDTH)
        lam = lru_lambda[l].astype(F32)

        ctx_proj = _in_proj(cx2, mc[0], mc[1], g[0], cos_c, sin_hi_c, sin_lo_c, w_in_bf, l, tmc,
                            queries=need_ctx)
        if need_ctx:
            qa_c, ka_c, va_c, qb_c, kb_c, vb_c, cu_c, cg_c, gt_c = ctx_proj
        else:
            ka_c, va_c, kb_c, vb_c, cu_c, cg_c = ctx_proj
        yc_c, hf_c, hb_c = _bidir_lru(cu_c, cg_c, zero_state, zero_state, cw, cb, wg_bf, bg, lam,
                                      l, bsz, n_ctx)

        (qa, ka, va, qb, kb, vb, cu, cg, gt) = _in_proj(
            x2, mx[0], mx[1], g[0], cos, sin_hi, sin_lo, w_in_bf, l, tm)
        ya = _window_attention(qa, ka, va, ka_c, va_c, attn_sink[l], bsz, n_tok, n_ctx)
        yb = _diff_attention(qb, kb, vb, kb_c, vb_c, diff_lambda[l], diff_subln[l], lam_init,
                             bsz, n_tok, n_ctx)
        yc, _, _ = _bidir_lru(cu, cg, hf_c, hb_c, cw, cb, wg_bf, bg, lam, l, bsz, n_tok)
        x2 = _merge(ya, yb, yc, gt, x2, mx[2], g[1], wbr_bf, wout_bf, l, tm)
        x2 = _ffn(x2, mx[3], mx[4], mx[5], g[2], g[3], w1_bf, w2_bf, l, tm)

        if need_ctx:
            ya_c = _context_attention(qa_c, ka_c, va_c, attn_sink[l], bsz, n_ctx)
            yb_c = _diff_attention(qb_c, None, None, kb_c, vb_c, diff_lambda[l], diff_subln[l],
                                   lam_init, bsz, 0, n_ctx)
            cx2 = _merge(ya_c, yb_c, yc_c, gt_c, cx2, mc[2], g[1], wbr_bf, wout_bf, l, tmc)
            cx2 = _ffn(cx2, mc[3], mc[4], mc[5], g[2], g[3], w1_bf, w2_bf, l, tmc)

    return x2.reshape(bsz, n_tok, d)
```

```python
import functools
import math

import jax
import jax.numpy as jnp
from jax import lax
from jax.experimental import pallas as pl
from jax.experimental.pallas import tpu as pltpu

F32 = jnp.float32
BF16 = jnp.bfloat16

D_MODEL = 1024
GRID_W = 64
HEAD_DIM = 64
ROPE_THETA = 10000.0
EPS = 1e-6
A_HEADS = 8
A_KV_HEADS = 2
A_GROUPS = A_HEADS // A_KV_HEADS
WINDOW = 128
B_HEADS = 4
B_VDIM = 2 * HEAD_DIM
C_WIDTH = 512
C_BLOCKS = 8
C_BW = C_WIDTH // C_BLOCKS
CONV_W = 4
LRU_C = 8.0
N_BRANCH = 3
BRANCH_W = 512
FFN_HIDDEN = -(-8 * D_MODEL // (3 * 256)) * 256

O_AQ = 0
O_AK = O_AQ + A_HEADS * HEAD_DIM
O_AV = O_AK + A_KV_HEADS * HEAD_DIM
O_BQ = O_AV + A_KV_HEADS * HEAD_DIM
O_BK = O_BQ + B_HEADS * 2 * HEAD_DIM
O_BV = O_BK + B_HEADS * 2 * HEAD_DIM
O_CU = O_BV + B_HEADS * B_VDIM
O_CG = O_CU + C_WIDTH
O_GT = O_CG + C_WIDTH
IN_WIDTH = O_GT + N_BRANCH * D_MODEL

LANES = 128
SUBLANES = 8
BF16_ROWS = 16
MXU_TILE = 256
VMEM_CAP_BYTES = 60 * 1024 * 1024
NEG_BIG = -1e30
Q_BLOCK = 128
WIN_Q_BLOCKS = 8
DIFF_TK = 512
DIFF_HEADS_PER_STEP = 2
LRU_ROWS = 512
LRU_PAD = 8
LOG2E = math.log2(math.e)
Q_SCALE = HEAD_DIM ** -0.5 * LOG2E


def _params(semantics, vmem_bytes):
    return pltpu.CompilerParams(dimension_semantics=semantics,
                                vmem_limit_bytes=min(int(vmem_bytes), VMEM_CAP_BYTES))


def _resident(shape):
    nd = len(shape)
    return pl.BlockSpec(shape, lambda *_: (0,) * nd, pipeline_mode=pl.Buffered(1))


def _layer_resident(shape, layer):
    nd = len(shape)
    return pl.BlockSpec((None,) + tuple(shape), lambda *_: (layer,) + (0,) * nd,
                        pipeline_mode=pl.Buffered(1))


def _rms(x, g):
    return x * lax.rsqrt(jnp.mean(x * x, axis=-1, keepdims=True) + EPS) * g


def _mod_kernel(c_ref, w_ref, b_ref, o_ref):
    c = c_ref[...]
    s = c * jax.nn.sigmoid(c)
    o_ref[...] = jnp.dot(s, w_ref[...], preferred_element_type=F32,
                         precision=lax.Precision.HIGHEST) + b_ref[...]


def _modulation(c_rows, w_mod, b_mod):
    depth, d, n = w_mod.shape
    rows = c_rows.shape[0]
    tn = 1536
    return pl.pallas_call(
        _mod_kernel,
        grid=(depth, n // tn),
        in_specs=[pl.BlockSpec((rows, d), lambda l, j: (0, 0)),
                  pl.BlockSpec((None, d, tn), lambda l, j: (l, 0, j)),
                  pl.BlockSpec((None, 1, tn), lambda l, j: (l, 0, j))],
        out_specs=pl.BlockSpec((None, rows, tn), lambda l, j: (l, 0, j)),
        out_shape=jax.ShapeDtypeStruct((depth, rows, n), F32),
        compiler_params=_params(("parallel", "parallel"), 32 << 20),
        name="modulation",
    )(c_rows, w_mod, b_mod.reshape(depth, 1, n))


def _in_proj_kernel(x_ref, sh_ref, sc_ref, g_ref, cos_ref, sa_ref, sb_ref, w_ref, *out_refs, queries):
    if queries:
        qa_ref, ka_ref, va_ref, qb_ref, kb_ref, vb_ref, cu_ref, cg_ref, gt_ref = out_refs
    else:
        ka_ref, va_ref, kb_ref, vb_ref, cu_ref, cg_ref = out_refs
    h = _rms(x_ref[...], g_ref[...])
    h = (h * (1.0 + sc_ref[...]) + sh_ref[...]).astype(BF16)

    def proj(lo, width):
        return jnp.dot(h, w_ref[:, lo:lo + width], preferred_element_type=F32)

    cos = cos_ref[...]
    sin_hi = sa_ref[...]
    sin_lo = sb_ref[...]

    def rope(a):
        outs = []
        for k in range(a.shape[1] // LANES):
            blk = a[:, k * LANES:(k + 1) * LANES]
            outs.append(blk * cos + pltpu.roll(blk, HEAD_DIM // 4, 1) * sin_hi
                        + pltpu.roll(blk, LANES - HEAD_DIM // 4, 1) * sin_lo)
        return jnp.concatenate(outs, axis=1)

    kv_a = proj(O_AK, O_BQ - O_AK)
    ka_ref[...] = rope(kv_a[:, :O_AV - O_AK]).astype(BF16)
    va_ref[...] = kv_a[:, O_AV - O_AK:].T.astype(BF16)
    kb_ref[...] = rope(proj(O_BK, O_BV - O_BK)).astype(BF16)
    vb_ref[...] = proj(O_BV, O_CU - O_BV).T.astype(BF16)
    cu_ref[...] = proj(O_CU, C_WIDTH).astype(BF16)
    cg_ref[...] = proj(O_CG, C_WIDTH).astype(BF16)
    if queries:
        qa_ref[...] = (rope(proj(O_AQ, O_AK - O_AQ)) * Q_SCALE).T.astype(BF16)
        qb_ref[...] = (rope(proj(O_BQ, O_BK - O_BQ)) * Q_SCALE).T.astype(BF16)
        for k in range(N_BRANCH):
            gt_ref[:, k * D_MODEL:(k + 1) * D_MODEL] = proj(O_GT + k * D_MODEL, D_MODEL).astype(BF16)


def _in_proj(x2, shift, scale, g0, cos, sin_hi, sin_lo, w_bf, layer, tm, queries=True):
    t, d = x2.shape
    nt = t // tm
    nb = shift.shape[0]
    npos = cos.shape[0] // tm
    tok = lambda w: pl.BlockSpec((tm, w), lambda i: (i, 0))
    feat = lambda w: pl.BlockSpec((w, tm), lambda i: (0, i))
    modspec = pl.BlockSpec((None, 1, d), lambda i: (i // (nt // nb), 0, 0))
    tabspec = pl.BlockSpec((tm, LANES), lambda i: (i % npos, 0))
    wq, wkv = A_HEADS * HEAD_DIM, A_KV_HEADS * HEAD_DIM
    wb = B_HEADS * B_VDIM
    out_shape = (
        jax.ShapeDtypeStruct((wq, t), BF16), jax.ShapeDtypeStruct((t, wkv), BF16),
        jax.ShapeDtypeStruct((wkv, t), BF16), jax.ShapeDtypeStruct((nt, wb, tm), BF16),
        jax.ShapeDtypeStruct((t, wb), BF16), jax.ShapeDtypeStruct((wb, t), BF16),
        jax.ShapeDtypeStruct((t, C_WIDTH), BF16), jax.ShapeDtypeStruct((t, C_WIDTH), BF16),
        jax.ShapeDtypeStruct((t, N_BRANCH * D_MODEL), BF16))
    out_specs = (feat(wq), tok(wkv), feat(wkv), pl.BlockSpec((None, wb, tm), lambda i: (i, 0, 0)), tok(wb), feat(wb),
                 tok(C_WIDTH), tok(C_WIDTH), tok(N_BRANCH * D_MODEL))
    if not queries:
        keep = (1, 2, 4, 5, 6, 7)
        out_shape = tuple(out_shape[i] for i in keep)
        out_specs = tuple(out_specs[i] for i in keep)
    vmem = d * IN_WIDTH * 2 + 2 * tm * d * 4 + 2 * tm * IN_WIDTH * 2 + 6 * tm * d * 4 + (8 << 20)
    return pl.pallas_call(
        functools.partial(_in_proj_kernel, queries=queries),
        grid=(nt,),
        in_specs=[tok(d), modspec, modspec, _resident((1, d)), tabspec, tabspec, tabspec,
                  _layer_resident((d, IN_WIDTH), layer)],
        out_specs=out_specs,
        out_shape=out_shape,
        compiler_params=_params(("parallel",), vmem),
        name="in_proj",
    )(x2, shift, scale, g0, cos, sin_hi, sin_lo, w_bf)


def _sink_scores(k_all, qt):
    tq = qt.shape[1]
    zero = jnp.zeros((HEAD_DIM, tq), qt.dtype)
    scores = []
    for j in range(A_KV_HEADS):
        cols = []
        for g in range(A_GROUPS):
            hd = j * A_GROUPS + g
            qh = qt[hd * HEAD_DIM:(hd + 1) * HEAD_DIM, :]
            cols.append(jnp.concatenate([qh, zero] if j == 0 else [zero, qh], axis=0))
        qz = jnp.concatenate(cols, axis=1)
        scores.append(jnp.dot(k_all, qz, preferred_element_type=F32))
    return scores


def _sink_softmax_pv(scores, vt_all, sink_row, bias, n_win):
    tq = scores[0].shape[1] // A_GROUPS
    outs = []
    ones = jnp.ones((BF16_ROWS, vt_all.shape[1]), BF16)
    for j in range(A_KV_HEADS):
        s = scores[j]
        sink = sink_row[:, j * A_GROUPS * tq:(j + 1) * A_GROUPS * tq] * LOG2E
        if n_win:
            sw = s[:n_win] + jnp.concatenate([bias] * A_GROUPS, axis=1)
            sc = s[n_win:]
            m = jnp.maximum(jnp.maximum(jnp.max(sw, axis=0, keepdims=True),
                                        jnp.max(sc, axis=0, keepdims=True)), sink)
            p = jnp.concatenate([jnp.exp2(sw - m), jnp.exp2(sc - m)], axis=0).astype(BF16)
        else:
            m = jnp.maximum(jnp.max(s, axis=0, keepdims=True), sink)
            p = jnp.exp2(s - m).astype(BF16)
        vt_aug = jnp.concatenate([vt_all[j * HEAD_DIM:(j + 1) * HEAD_DIM, :], ones], axis=0)
        o = jnp.dot(vt_aug, p, preferred_element_type=F32)
        l = o[HEAD_DIM:HEAD_DIM + 1] + jnp.exp2(sink - m)
        o = o[:HEAD_DIM] * (1.0 / l)
        for g in range(A_GROUPS):
            outs.append(o[:, g * tq:(g + 1) * tq])
    return jnp.concatenate(outs, axis=0)


def _win_attn_kernel(q_ref, kp_ref, kc_ref, kn_ref, vp_ref, vc_ref, vn_ref, kx_ref, vx_ref,
                     sink_ref, o_ref, *, n_tok):
    tq = Q_BLOCK
    q_blocks = q_ref.shape[1] // tq
    first = pl.program_id(1) * q_blocks
    k_cat = jnp.concatenate([kp_ref[...], kc_ref[...], kn_ref[...]], axis=0)
    vt_cat = jnp.concatenate([vp_ref[...], vc_ref[...], vn_ref[...]], axis=1)
    kx = kx_ref[...]
    vx = vx_ref[...]
    n_win = 3 * tq
    r = lax.broadcasted_iota(jnp.int32, (n_win, tq), 0)
    c = lax.broadcasted_iota(jnp.int32, (n_win, tq), 1)
    band = jnp.where(jnp.abs(r - tq - c) <= WINDOW, 0.0, NEG_BIG).astype(F32)

    def scores(b):
        k_all = jnp.concatenate([k_cat[b * tq:(b + 3) * tq], kx], axis=0)
        return _sink_scores(k_all, q_ref[:, b * tq:(b + 1) * tq])

    s_next = scores(0)
    for b in range(q_blocks):
        s_cur = s_next
        if b + 1 < q_blocks:
            s_next = scores(b + 1)
        kpos = (first + b - 1) * tq + r
        bias = band + jnp.where(kpos >= 0, 0.0, NEG_BIG) + jnp.where(kpos < n_tok, 0.0, NEG_BIG)
        vt_all = jnp.concatenate([vt_cat[:, b * tq:(b + 3) * tq], vx], axis=1)
        yt = _sink_softmax_pv(s_cur, vt_all, sink_ref[...], bias, n_win)
        o_ref[b * tq:(b + 1) * tq, :] = yt.T.astype(o_ref.dtype)


def _ctx_attn_kernel(q_ref, kx_ref, vx_ref, sink_ref, o_ref):
    yt = _sink_softmax_pv(_sink_scores(kx_ref[...], q_ref[...]), vx_ref[...], sink_ref[...], None, 0)
    o_ref[...] = yt.T.astype(o_ref.dtype)


def _sink_row(sink, tq):
    return jnp.repeat(sink.astype(F32), tq).reshape(1, A_HEADS * tq)


def _window_attention(qt, k, vt, kx, vxt, sink, bsz, n_tok, n_ctx):
    tq = Q_BLOCK
    nq = n_tok // tq
    qb = min(WIN_Q_BLOCKS, nq)
    ns = nq // qb
    wq, wkv = A_HEADS * HEAD_DIM, A_KV_HEADS * HEAD_DIM
    prev = lambda b, n: b * nq + jnp.maximum(n * qb - 1, 0)
    cur = lambda b, n: b * ns + n
    nxt = lambda b, n: b * nq + jnp.minimum(n * qb + qb, nq - 1)
    edge_k = lambda f: pl.BlockSpec((tq, wkv), lambda b, n: (f(b, n), 0))
    edge_v = lambda f: pl.BlockSpec((wkv, tq), lambda b, n: (0, f(b, n)))
    return pl.pallas_call(
        functools.partial(_win_attn_kernel, n_tok=n_tok),
        grid=(bsz, ns),
        in_specs=[pl.BlockSpec((wq, qb * tq), lambda b, n: (0, cur(b, n))),
                  edge_k(prev), pl.BlockSpec((qb * tq, wkv), lambda b, n: (cur(b, n), 0)), edge_k(nxt),
                  edge_v(prev), pl.BlockSpec((wkv, qb * tq), lambda b, n: (0, cur(b, n))), edge_v(nxt),
                  pl.BlockSpec((n_ctx, wkv), lambda b, n: (b, 0)),
                  pl.BlockSpec((wkv, n_ctx), lambda b, n: (0, b)),
                  pl.BlockSpec((1, A_HEADS * tq), lambda b, n: (0, 0))],
        out_specs=pl.BlockSpec((qb * tq, wq), lambda b, n: (cur(b, n), 0)),
        out_shape=jax.ShapeDtypeStruct((bsz * n_tok, wq), BF16),
        compiler_params=_params(("parallel", "parallel"), 32 << 20),
        name="window_attention",
    )(qt, k, k, k, vt, vt, vt, kx, vxt, _sink_row(sink, tq))


def _context_attention(qt, kx, vxt, sink, bsz, n_ctx):
    tq = Q_BLOCK
    nq = n_ctx // tq
    wq, wkv = A_HEADS * HEAD_DIM, A_KV_HEADS * HEAD_DIM
    return pl.pallas_call(
        _ctx_attn_kernel,
        grid=(bsz, nq),
        in_specs=[pl.BlockSpec((wq, tq), lambda b, n: (0, b * nq + n)),
                  pl.BlockSpec((n_ctx, wkv), lambda b, n: (b, 0)),
                  pl.BlockSpec((wkv, n_ctx), lambda b, n: (0, b)),
                  pl.BlockSpec((1, A_HEADS * tq), lambda b, n: (0, 0))],
        out_specs=pl.BlockSpec((tq, wq), lambda b, n: (b * nq + n, 0)),
        out_shape=jax.ShapeDtypeStruct((bsz * n_ctx, wq), BF16),
        compiler_params=_params(("parallel", "parallel"), 32 << 20),
        name="context_attention",
    )(qt, kx, vxt, _sink_row(sink, tq))


def _diff_attn_kernel(*refs, n_lat_chunks, tk, lam_init, heads):
    if n_lat_chunks:
        q_ref, k_ref, v_ref, kx_ref, vx_ref, dl_ref, g_ref, o_ref, acc_ref = refs
    else:
        q_ref, kx_ref, vx_ref, dl_ref, g_ref, o_ref, acc_ref = refs
    n_q_tiles, _, tq = q_ref.shape
    hw = B_VDIM
    row = lax.broadcasted_iota(jnp.int32, (hw, tq), 0)
    zero = jnp.zeros((hw, tq), q_ref.dtype)
    n = n_lat_chunks + 1

    def head_slice(hd):
        return slice(hd * hw, (hd + 1) * hw)

    def keys(hd, i):
        if i < n_lat_chunks:
            return k_ref[i * tk:(i + 1) * tk, head_slice(hd)]
        return kx_ref[:, head_slice(hd)]

    def values(hd, i):
        v = v_ref[head_slice(hd), i * tk:(i + 1) * tk] if i < n_lat_chunks else vx_ref[head_slice(hd), :]
        return jnp.concatenate([v, jnp.ones((BF16_ROWS, v.shape[1]), BF16)], axis=0)

    def attend(qi):
        qz = []
        for hd in range(heads):
            qt = q_ref[qi, head_slice(hd), :]
            qz.append(jnp.concatenate([jnp.where(row < HEAD_DIM, qt, zero),
                                       jnp.where(row >= HEAD_DIM, qt, zero)], axis=1))

        def scores(hd, i):
            s = jnp.dot(keys(hd, i), qz[hd], preferred_element_type=F32)
            return s, jnp.max(s, axis=0, keepdims=True)

        m = [jnp.full((1, 2 * tq), NEG_BIG, F32) for _ in range(heads)]
        acc = [jnp.zeros((hw + BF16_ROWS, 2 * tq), F32) for _ in range(heads)]
        nxt = [scores(hd, 0) for hd in range(heads)]
        e_prev = [None] * heads
        alpha = [None] * heads
        for c in range(n):
            cur = list(nxt)
            for hd in range(heads):
                if c + 1 < n:
                    nxt[hd] = scores(hd, c + 1)
                if c:
                    acc[hd] = alpha[hd] * acc[hd] + jnp.dot(values(hd, c - 1), e_prev[hd],
                                                            preferred_element_type=F32)
            for hd in range(heads):
                s, smax = cur[hd]
                m_new = jnp.maximum(m[hd], smax)
                alpha[hd] = jnp.exp2(m[hd] - m_new)
                e_prev[hd] = jnp.exp2(s - m_new).astype(BF16)
                m[hd] = m_new
        return [alpha[hd] * acc[hd] + jnp.dot(values(hd, n - 1), e_prev[hd], preferred_element_type=F32)
                for hd in range(heads)]

    dl = dl_ref[...]
    lam = (jnp.exp(jnp.sum(dl[0:1] * dl[1:2], axis=1, keepdims=True))
           - jnp.exp(jnp.sum(dl[2:3] * dl[3:4], axis=1, keepdims=True)) + lam_init)

    def finish(qi, accs):
        for hd in range(heads):
            a = accs[hd]
            o = a[:hw] * (1.0 / a[hw:hw + 1])
            y = (o[:, :tq] - lam * o[:, tq:]).T
            y = (_rms(y, g_ref[...]) * (1.0 - lam_init)).astype(o_ref.dtype)
            o_ref[pl.ds(pl.multiple_of(qi * tq, tq), tq), head_slice(hd)] = y

    if n_q_tiles == 1:
        finish(0, attend(0))
        return

    first = attend(0)
    for hd in range(heads):
        acc_ref[hd] = first[hd]

    def body(qi, carry):
        parked = [acc_ref[hd] for hd in range(heads)]
        new = attend(qi)
        finish(qi - 1, parked)
        for hd in range(heads):
            acc_ref[hd] = new[hd]
        return carry
    lax.fori_loop(1, n_q_tiles, body, 0)
    finish(n_q_tiles - 1, [acc_ref[hd] for hd in range(heads)])


def _diff_attention(qt, k, vt, kx, vxt, diff_lambda, subln, lam_init, bsz, n_tok, n_ctx):
    nq_tok = n_tok if n_tok else n_ctx
    tq = qt.shape[2]
    nqt = nq_tok // tq
    tk = min(DIFF_TK, n_tok) if n_tok else 0
    heads = DIFF_HEADS_PER_STEP
    hw = heads * B_VDIM
    in_specs = [pl.BlockSpec((nqt, hw, tq), lambda b, h: (b, h, 0))]
    args = [qt]
    if n_tok:
        in_specs += [pl.BlockSpec((n_tok, hw), lambda b, h: (b, h)),
                     pl.BlockSpec((hw, n_tok), lambda b, h: (h, b))]
        args += [k, vt]
    in_specs += [pl.BlockSpec((n_ctx, hw), lambda b, h: (b, h)),
                 pl.BlockSpec((hw, n_ctx), lambda b, h: (h, b)),
                 pl.BlockSpec((4, HEAD_DIM), lambda b, h: (0, 0)),
                 pl.BlockSpec((1, B_VDIM), lambda b, h: (0, 0))]
    args += [kx, vxt, diff_lambda.astype(F32), subln.astype(F32).reshape(1, B_VDIM)]
    return pl.pallas_call(
        functools.partial(_diff_attn_kernel, n_lat_chunks=(n_tok // tk if n_tok else 0), tk=tk,
                          lam_init=lam_init, heads=heads),
        grid=(bsz, B_HEADS // heads),
        in_specs=in_specs,
        out_specs=pl.BlockSpec((nq_tok, hw), lambda b, h: (b, h)),
        out_shape=jax.ShapeDtypeStruct((bsz * nq_tok, B_HEADS * B_VDIM), BF16),
        scratch_shapes=[pltpu.VMEM((heads, B_VDIM + BF16_ROWS, 2 * tq), F32)],
        compiler_params=_params(("parallel", "parallel"), 48 << 20),
        name="diff_attention",
    )(*args)


def _scan_segments(a, b, h_in, reverse):
    rows, w = a.shape
    seg = rows // SUBLANES
    order = range(seg - 1, -1, -1) if reverse else range(seg)
    h_loc = [None] * seg
    a_cum = [None] * seg
    h = prod = None
    for i in order:
        ai = a[i * SUBLANES:(i + 1) * SUBLANES]
        bi = b[i * SUBLANES:(i + 1) * SUBLANES]
        h = bi if h is None else ai * h + bi
        prod = ai if prod is None else ai * prod
        h_loc[i] = h
        a_cum[i] = prod
    sub = lax.broadcasted_iota(jnp.int32, (SUBLANES, w), 0)
    e, p = h, prod
    for d in (1, 2, 4):
        keep = (sub < SUBLANES - d) if reverse else (sub >= d)
        shift = SUBLANES - d if reverse else d
        p_s = jnp.where(keep, pltpu.roll(p, shift, 0), 1.0)
        e_s = jnp.where(keep, pltpu.roll(e, shift, 0), 0.0)
        e = e + p * e_s
        p = p * p_s
    seg_out = e + p * h_in
    if reverse:
        seg_in = jnp.where(sub < SUBLANES - 1, pltpu.roll(seg_out, SUBLANES - 1, 0), h_in)
        h_out = seg_out[0:1]
    else:
        seg_in = jnp.where(sub >= 1, pltpu.roll(seg_out, 1, 0), h_in)
        h_out = seg_out[SUBLANES - 1:SUBLANES]
    return [h_loc[i] + a_cum[i] * seg_in for i in range(seg)], h_out


def _lru_kernel(cu_ref, cg_ref, h0f_ref, h0b_ref, cw_ref, cb_ref, wg_ref, bg_ref, lam_ref,
                y_ref, hf_ref, hb_ref, xpad_ref, hs_ref, u_ref, *, rows):
    n = cu_ref.shape[0]
    nchunks = n // rows
    w = C_WIDTH
    seg = rows // SUBLANES
    slabs = w // LANES

    def load(ref, idx):
        return jnp.concatenate([ref[j, idx, :] for j in range(slabs)], axis=1)

    def store(ref, idx, val):
        for j in range(slabs):
            ref[j, idx, :] = val[:, j * LANES:(j + 1) * LANES]

    store(xpad_ref, pl.ds(0, LRU_PAD), jnp.zeros((LRU_PAD, w), F32))
    store(xpad_ref, pl.ds(LRU_PAD + n, LRU_PAD), jnp.zeros((LRU_PAD, w), F32))

    def fill(ci, carry):
        s = pl.multiple_of(ci * rows, rows)
        store(xpad_ref, pl.ds(s + LRU_PAD, rows), cu_ref[pl.ds(s, rows), :].astype(F32))
        return carry
    lax.fori_loop(0, nchunks, fill, 0)

    lam = lam_ref[...]
    decay = LRU_C * (jnp.maximum(-lam, 0.0) + jnp.log1p(jnp.exp(-jnp.abs(lam))))
    cw = cw_ref[...]
    cb = cb_ref[...]

    def conv(s):
        first = LRU_PAD - CONV_W // 2
        taps = [load(xpad_ref, pl.ds(s + (first + q), SUBLANES, stride=seg))
                for q in range(seg + CONV_W - 1)]
        groups = []
        for i in range(seg):
            acc = cb
            for k in range(CONV_W):
                acc = acc + cw[k:k + 1] * taps[i + k]
            groups.append(acc)
        return jnp.concatenate(groups, axis=0)

    def coeffs(u, d):
        g = jnp.dot(u.astype(BF16), wg_ref[:, d * 2 * w:(d + 1) * 2 * w],
                    preferred_element_type=F32) + bg_ref[:, d * 2 * w:(d + 1) * 2 * w]
        r = jax.nn.sigmoid(g[:, :w])
        i_gate = jax.nn.sigmoid(g[:, w:])
        z = decay[d:d + 1] * r
        a = jnp.exp(-z)
        v = (1.0 + a * a) * jnp.tanh(z)
        root = jnp.where(v > 0.0, v * lax.rsqrt(v), 0.0)
        return a, root * i_gate * u

    r_i = lax.broadcasted_iota(jnp.int32, (rows, rows), 0)
    c_i = lax.broadcasted_iota(jnp.int32, (rows, rows), 1)
    to_tokens = jnp.where(c_i == (r_i % seg) * SUBLANES + r_i // seg, 1.0, 0.0).astype(BF16)
    to_segments = jnp.where(r_i == (c_i % seg) * SUBLANES + c_i // seg, 1.0, 0.0).astype(BF16)

    def bwd(ci, h):
        s = pl.multiple_of((nchunks - 1 - ci) * rows, rows)
        u = conv(s)
        u_ref[pl.ds(s, rows), :] = u
        a, bx = coeffs(u, 1)
        hs, h = _scan_segments(a, bx, h, True)
        hs_ref[pl.ds(s, rows), :] = jnp.concatenate(hs, axis=0)
        return h
    hb_ref[...] = lax.fori_loop(0, nchunks, bwd, h0b_ref[...])

    def fwd(ci, h):
        s = pl.multiple_of(ci * rows, rows)
        a, bx = coeffs(u_ref[pl.ds(s, rows), :], 0)
        hs, h = _scan_segments(a, bx, h, False)
        gate = jax.nn.gelu(jnp.dot(to_segments, cg_ref[pl.ds(s, rows), :], preferred_element_type=F32))
        y = ((jnp.concatenate(hs, axis=0) + hs_ref[pl.ds(s, rows), :]) * gate).astype(BF16)
        y_ref[pl.ds(s, rows), :] = jnp.dot(to_tokens, y, preferred_element_type=F32).astype(y_ref.dtype)
        return h
    hf_ref[...] = lax.fori_loop(0, nchunks, fwd, h0f_ref[...])


def _bidir_lru(cu, cg, h0f, h0b, conv_w, conv_b, wg_bf, bg, lam, layer, bsz, n):
    w = C_WIDTH
    rows = min(LRU_ROWS, n)
    seq = pl.BlockSpec((n, w), lambda b: (b, 0))
    st = pl.BlockSpec((None, 1, w), lambda b: (b, 0, 0))
    vmem = 6 * n * w * 2 + 2 * n * w * 4 + (n + 2 * LRU_PAD) * w * 4 + (12 << 20)
    return pl.pallas_call(
        functools.partial(_lru_kernel, rows=rows),
        grid=(bsz,),
        in_specs=[seq, seq, st, st, _resident((CONV_W, w)), _resident((1, w)),
                  _layer_resident((w, 4 * w), layer), _resident((1, 4 * w)), _resident((2, w))],
        out_specs=(seq, st, st),
        out_shape=(jax.ShapeDtypeStruct((bsz * n, w), BF16),
                   jax.ShapeDtypeStruct((bsz, 1, w), F32), jax.ShapeDtypeStruct((bsz, 1, w), F32)),
        scratch_shapes=[pltpu.VMEM((w // LANES, n + 2 * LRU_PAD, LANES), F32),
                        pltpu.VMEM((n, w), F32), pltpu.VMEM((n, w), F32)],
        compiler_params=_params(("parallel",), vmem),
        name="bidir_lru",
    )(cu, cg, h0f, h0b, conv_w, conv_b, wg_bf, bg, lam)


def _merge_kernel(ya_ref, yb_ref, yc_ref, gt_ref, x_ref, gx_ref, g_ref, wbr_ref, wout_ref, o_ref):
    acc = None
    for n, y_ref in enumerate((ya_ref, yb_ref, yc_ref)):
        z = jnp.dot(y_ref[...], wbr_ref[n], preferred_element_type=F32)
        gate = jax.nn.sigmoid(gt_ref[:, n * D_MODEL:(n + 1) * D_MODEL].astype(F32))
        acc = gate * z if acc is None else acc + gate * z
    mix = jnp.dot(acc.astype(BF16), wout_ref[...], preferred_element_type=F32)
    o_ref[...] = x_ref[...] + gx_ref[...] * _rms(mix, g_ref[...])


def _merge(ya, yb, yc, gt, x2, gate_x, g1, wbr_bf, wout_bf, layer, tm):
    t, d = x2.shape
    nt = t // tm
    nb = gate_x.shape[0]
    tok = lambda w: pl.BlockSpec((tm, w), lambda i: (i, 0))
    modspec = pl.BlockSpec((None, 1, d), lambda i: (i // (nt // nb), 0, 0))
    vmem = (N_BRANCH * BRANCH_W * d + d * d) * 2 + 2 * tm * (3 * BRANCH_W + 3 * d) * 2 \
        + 4 * tm * d * 4 + 6 * tm * d * 4 + (8 << 20)
    return pl.pallas_call(
        _merge_kernel,
        grid=(nt,),
        in_specs=[tok(BRANCH_W), tok(BRANCH_W), tok(BRANCH_W), tok(N_BRANCH * d), tok(d), modspec,
                  _resident((1, d)), _layer_resident((N_BRANCH, BRANCH_W, d), layer),
                  _layer_resident((d, d), layer)],
        out_specs=tok(d),
        out_shape=jax.ShapeDtypeStruct((t, d), F32),
        compiler_params=_params(("parallel",), vmem),
        name="gated_merge",
    )(ya, yb, yc, gt, x2, gate_x, g1, wbr_bf, wout_bf)


def _ffn_kernel(x_ref, sh_ref, sc_ref, gx_ref, g2_ref, g3_ref, w1_ref, w2_ref, o_ref, *, n_chunks):
    x = x_ref[...]
    h = _rms(x, g2_ref[...])
    h = (h * (1.0 + sc_ref[...]) + sh_ref[...]).astype(BF16)
    tiles = FFN_HIDDEN // MXU_TILE
    bounds = [(-(-tiles * c // n_chunks)) * MXU_TILE for c in range(n_chunks + 1)]
    f = None
    for lo, hi in zip(bounds[:-1], bounds[1:]):
        gate = jnp.dot(h, w1_ref[:, lo:hi], preferred_element_type=F32)
        up = jnp.dot(h, w1_ref[:, FFN_HIDDEN + lo:FFN_HIDDEN + hi], preferred_element_type=F32)
        act = (gate * jax.nn.sigmoid(gate) * up).astype(BF16)
        part = jnp.dot(act, w2_ref[lo:hi, :], preferred_element_type=F32)
        f = part if f is None else f + part
    o_ref[...] = x + gx_ref[...] * _rms(f, g3_ref[...])


def _ffn(x2, shift, scale, gate_x, g2, g3, w1_bf, w2_bf, layer, tm):
    t, d = x2.shape
    nt = t // tm
    nb = shift.shape[0]
    tok = pl.BlockSpec((tm, d), lambda i: (i, 0))
    modspec = pl.BlockSpec((None, 1, d), lambda i: (i // (nt // nb), 0, 0))
    n_chunks = 2
    vmem = 3 * d * FFN_HIDDEN * 2 + 4 * tm * d * 4 + 3 * tm * (FFN_HIDDEN // n_chunks) * 4 \
        + 4 * tm * d * 4 + (8 << 20)
    return pl.pallas_call(
        functools.partial(_ffn_kernel, n_chunks=n_chunks),
        grid=(nt,),
        in_specs=[tok, modspec, modspec, modspec, _resident((1, d)), _resident((1, d)),
                  _layer_resident((d, 2 * FFN_HIDDEN), layer), _layer_resident((FFN_HIDDEN, d), layer)],
        out_specs=tok,
        out_shape=jax.ShapeDtypeStruct((t, d), F32),
        compiler_params=_params(("parallel",), vmem),
        name="swiglu_ffn",
    )(x2, shift, scale, gate_x, g2, g3, w1_bf, w2_bf)


def _rope_tables(n_tokens):
    rd = HEAD_DIM // 4
    t = jnp.arange(n_tokens)
    pos = jnp.stack([t // GRID_W, t % GRID_W], axis=-1).astype(F32)
    inv = 1.0 / (ROPE_THETA ** (jnp.arange(rd, dtype=F32) * 2.0 / (HEAD_DIM // 2)))
    ang = pos[:, :, None] * inv
    cos = jnp.cos(ang)[:, :, None, :]
    sin = jnp.sin(ang)[:, :, None, :]
    zeros = jnp.zeros_like(sin)
    cos64 = jnp.concatenate([cos, cos], axis=2).reshape(n_tokens, HEAD_DIM)
    hi64 = jnp.concatenate([zeros, sin], axis=2).reshape(n_tokens, HEAD_DIM)
    lo64 = jnp.concatenate([-sin, zeros], axis=2).reshape(n_tokens, HEAD_DIM)
    rep = LANES // HEAD_DIM
    return jnp.tile(cos64, (1, rep)), jnp.tile(hi64, (1, rep)), jnp.tile(lo64, (1, rep))


def _identity_tables(rows):
    z = jnp.zeros((rows, LANES), F32)
    return jnp.ones((rows, LANES), F32), z, z


def _gate_weights(lru_w):
    eye = jnp.eye(C_BLOCKS, dtype=lru_w.dtype)
    dense = jnp.einsum("ldgncf,nm->lncdgmf", lru_w, eye)
    return dense.reshape(lru_w.shape[0], C_WIDTH, 4 * C_WIDTH).astype(BF16)


def kernel(x, c, ctx, c_ctx, w_mod, b_mod, norm_g, w_in, attn_sink, diff_lambda, diff_subln, conv_w,
           conv_b, lru_w, lru_b, lru_lambda, w_branch, w_out, w_ffn_in, w_ffn_out):
    bsz, n_tok, d = x.shape
    n_ctx = ctx.shape[1]
    depth = w_mod.shape[0]
    tm = min(512, n_tok)
    tmc = min(256, n_ctx)

    mod_rows = 2 * SUBLANES
    c_rows = jnp.zeros((mod_rows, d), F32).at[:bsz].set(c).at[bsz].set(c_ctx)
    mods = _modulation(c_rows, w_mod, b_mod)

    cos, sin_hi, sin_lo = _rope_tables(n_tok)
    cos_c, sin_hi_c, sin_lo_c = _identity_tables(tmc)

    x2 = x.reshape(bsz * n_tok, d)
    cx2 = ctx.reshape(bsz * n_ctx, d)
    zero_state = jnp.zeros((bsz, 1, C_WIDTH), F32)

    w_in_bf = w_in.astype(BF16)
    wg_bf = _gate_weights(lru_w)
    wbr_bf = w_branch.astype(BF16)
    wout_bf = w_out.astype(BF16)
    w1_bf = w_ffn_in.astype(BF16)
    w2_bf = w_ffn_out.astype(BF16)

    for l in range(depth):
        need_ctx = l < depth - 1
        lam_init = 0.8 - 0.6 * math.exp(-0.3 * l)
        mx = [mods[l, :bsz, k * d:(k + 1) * d].reshape(bsz, 1, d) for k in range(6)]
        mc = [mods[l, bsz:bsz + 1, k * d:(k + 1) * d].reshape(1, 1, d) for k in range(6)]
        g = [norm_g[l, k].reshape(1, d).astype(F32) for k in range(4)]
        bg = lru_b[l].astype(F32).reshape(1, 4 * C_WIDTH)
        cw = conv_w[l].astype(F32)
        cb = conv_b[l].astype(F32).reshape(1, C_WIDTH)
        lam = lru_lambda[l].astype(F32)

        ctx_proj = _in_proj(cx2, mc[0], mc[1], g[0], cos_c, sin_hi_c, sin_lo_c, w_in_bf, l, tmc,
                            queries=need_ctx)
        if need_ctx:
            qa_c, ka_c, va_c, qb_c, kb_c, vb_c, cu_c, cg_c, gt_c = ctx_proj
        else:
            ka_c, va_c, kb_c, vb_c, cu_c, cg_c = ctx_proj
        yc_c, hf_c, hb_c = _bidir_lru(cu_c, cg_c, zero_state, zero_state, cw, cb, wg_bf, bg, lam,
                                      l, bsz, n_ctx)

        (qa, ka, va, qb, kb, vb, cu, cg, gt) = _in_proj(
            x2, mx[0], mx[1], g[0], cos, sin_hi, sin_lo, w_in_bf, l, tm)
        ya = _window_attention(qa, ka, va, ka_c, va_c, attn_sink[l], bsz, n_tok, n_ctx)
        yb = _diff_attention(qb, kb, vb, kb_c, vb_c, diff_lambda[l], diff_subln[l], lam_init,
                             bsz, n_tok, n_ctx)
        yc, _, _ = _bidir_lru(cu, cg, hf_c, hb_c, cw, cb, wg_bf, bg, lam, l, bsz, n_tok)
        x2 = _merge(ya, yb, yc, gt, x2, mx[2], g[1], wbr_bf, wout_bf, l, tm)
        x2 = _ffn(x2, mx[3], mx[4], mx[5], g[2], g[3], w1_bf, w2_bf, l, tm)

        if need_ctx:
            ya_c = _context_attention(qa_c, ka_c, va_c, attn_sink[l], bsz, n_ctx)
            yb_c = _diff_attention(qb_c, None, None, kb_c, vb_c, diff_lambda[l], diff_subln[l],
                                   lam_init, bsz, 0, n_ctx)
            cx2 = _merge(ya_c, yb_c, yc_c, gt_c, cx2, mc[2], g[1], wbr_bf, wout_bf, l, tmc)
            cx2 = _ffn(cx2, mc[3], mc[4], mc[5], g[2], g[3], w1_bf, w2_bf, l, tmc)

    return x2.reshape(bsz, n_tok, d)
```

```python
import functools
import math

import jax
import jax.numpy as jnp
from jax import lax
from jax.experimental import pallas as pl
from jax.experimental.pallas import tpu as pltpu

F32 = jnp.float32
BF16 = jnp.bfloat16

D_MODEL = 1024
GRID_W = 64
HEAD_DIM = 64
ROPE_THETA = 10000.0
EPS = 1e-6
A_HEADS = 8
A_KV_HEADS = 2
A_GROUPS = A_HEADS // A_KV_HEADS
WINDOW = 128
B_HEADS = 4
B_VDIM = 2 * HEAD_DIM
C_WIDTH = 512
C_BLOCKS = 8
C_BW = C_WIDTH // C_BLOCKS
CONV_W = 4
LRU_C = 8.0
N_BRANCH = 3
BRANCH_W = 512
FFN_HIDDEN = -(-8 * D_MODEL // (3 * 256)) * 256

O_AQ = 0
O_AK = O_AQ + A_HEADS * HEAD_DIM
O_AV = O_AK + A_KV_HEADS * HEAD_DIM
O_BQ = O_AV + A_KV_HEADS * HEAD_DIM
O_BK = O_BQ + B_HEADS * 2 * HEAD_DIM
O_BV = O_BK + B_HEADS * 2 * HEAD_DIM
O_CU = O_BV + B_HEADS * B_VDIM
O_CG = O_CU + C_WIDTH
O_GT = O_CG + C_WIDTH
IN_WIDTH = O_GT + N_BRANCH * D_MODEL

LANES = 128
SUBLANES = 8
BF16_ROWS = 16
MXU_TILE = 256
VMEM_CAP_BYTES = 60 * 1024 * 1024
NEG_BIG = -1e30
Q_BLOCK = 128
WIN_Q_BLOCKS = 8
DIFF_TK = 512
DIFF_HEADS_PER_STEP = 2
ROW_SUBTILE = 512
LRU_ROWS = 512
LRU_PAD = 8
LOG2E = math.log2(math.e)
Q_SCALE = HEAD_DIM ** -0.5 * LOG2E


def _params(semantics, vmem_bytes):
    return pltpu.CompilerParams(dimension_semantics=semantics,
                                vmem_limit_bytes=min(int(vmem_bytes), VMEM_CAP_BYTES))


def _resident(shape):
    nd = len(shape)
    return pl.BlockSpec(shape, lambda *_: (0,) * nd, pipeline_mode=pl.Buffered(1))


def _layer_resident(shape, layer):
    nd = len(shape)
    return pl.BlockSpec((None,) + tuple(shape), lambda *_: (layer,) + (0,) * nd,
                        pipeline_mode=pl.Buffered(1))


def _rms(x, g):
    return x * lax.rsqrt(jnp.mean(x * x, axis=-1, keepdims=True) + EPS) * g


def _mod_kernel(c_ref, w_ref, b_ref, o_ref):
    c = c_ref[...]
    s = c * jax.nn.sigmoid(c)
    o_ref[...] = jnp.dot(s, w_ref[...], preferred_element_type=F32,
                         precision=lax.Precision.HIGHEST) + b_ref[...]


def _modulation(c_rows, w_mod, b_mod):
    depth, d, n = w_mod.shape
    rows = c_rows.shape[0]
    tn = 1536
    return pl.pallas_call(
        _mod_kernel,
        grid=(depth, n // tn),
        in_specs=[pl.BlockSpec((rows, d), lambda l, j: (0, 0)),
                  pl.BlockSpec((None, d, tn), lambda l, j: (l, 0, j)),
                  pl.BlockSpec((None, 1, tn), lambda l, j: (l, 0, j))],
        out_specs=pl.BlockSpec((None, rows, tn), lambda l, j: (l, 0, j)),
        out_shape=jax.ShapeDtypeStruct((depth, rows, n), F32),
        compiler_params=_params(("parallel", "parallel"), 32 << 20),
        name="modulation",
    )(c_rows, w_mod, b_mod.reshape(depth, 1, n))


def _in_proj_kernel(x_ref, sh_ref, sc_ref, g_ref, cos_ref, sa_ref, sb_ref, w_ref, *out_refs, queries):
    if queries:
        qa_ref, ka_ref, va_ref, qb_ref, kb_ref, vb_ref, cu_ref, cg_ref, gt_ref = out_refs
    else:
        ka_ref, va_ref, kb_ref, vb_ref, cu_ref, cg_ref = out_refs
    h = _rms(x_ref[...], g_ref[...])
    h = (h * (1.0 + sc_ref[...]) + sh_ref[...]).astype(BF16)

    def proj(lo, width):
        return jnp.dot(h, w_ref[:, lo:lo + width], preferred_element_type=F32)

    cos = cos_ref[...]
    sin_hi = sa_ref[...]
    sin_lo = sb_ref[...]

    def rope(a):
        outs = []
        for k in range(a.shape[1] // LANES):
            blk = a[:, k * LANES:(k + 1) * LANES]
            outs.append(blk * cos + pltpu.roll(blk, HEAD_DIM // 4, 1) * sin_hi
                        + pltpu.roll(blk, LANES - HEAD_DIM // 4, 1) * sin_lo)
        return jnp.concatenate(outs, axis=1)

    kv_a = proj(O_AK, O_BQ - O_AK)
    ka_ref[...] = rope(kv_a[:, :O_AV - O_AK]).astype(BF16)
    va_ref[...] = kv_a[:, O_AV - O_AK:].T.astype(BF16)
    kb_ref[...] = rope(proj(O_BK, O_BV - O_BK)).astype(BF16)
    vb_ref[...] = proj(O_BV, O_CU - O_BV).T.astype(BF16)
    cu_ref[...] = proj(O_CU, C_WIDTH).astype(BF16)
    cg_ref[...] = proj(O_CG, C_WIDTH).astype(BF16)
    if queries:
        qa_ref[...] = (rope(proj(O_AQ, O_AK - O_AQ)) * Q_SCALE).T.astype(BF16)
        qb_ref[...] = (rope(proj(O_BQ, O_BK - O_BQ)) * Q_SCALE).T.astype(BF16)
        for k in range(N_BRANCH):
            gt_ref[:, k * D_MODEL:(k + 1) * D_MODEL] = proj(O_GT + k * D_MODEL, D_MODEL).astype(BF16)


def _in_proj(x2, shift, scale, g0, cos, sin_hi, sin_lo, w_bf, layer, tm, queries=True):
    t, d = x2.shape
    nt = t // tm
    nb = shift.shape[0]
    npos = cos.shape[0] // tm
    tok = lambda w: pl.BlockSpec((tm, w), lambda i: (i, 0))
    feat = lambda w: pl.BlockSpec((w, tm), lambda i: (0, i))
    modspec = pl.BlockSpec((None, 1, d), lambda i: (i // (nt // nb), 0, 0))
    tabspec = pl.BlockSpec((tm, LANES), lambda i: (i % npos, 0))
    wq, wkv = A_HEADS * HEAD_DIM, A_KV_HEADS * HEAD_DIM
    wb = B_HEADS * B_VDIM
    out_shape = (
        jax.ShapeDtypeStruct((wq, t), BF16), jax.ShapeDtypeStruct((t, wkv), BF16),
        jax.ShapeDtypeStruct((wkv, t), BF16), jax.ShapeDtypeStruct((nt, wb, tm), BF16),
        jax.ShapeDtypeStruct((t, wb), BF16), jax.ShapeDtypeStruct((wb, t), BF16),
        jax.ShapeDtypeStruct((t, C_WIDTH), BF16), jax.ShapeDtypeStruct((t, C_WIDTH), BF16),
        jax.ShapeDtypeStruct((t, N_BRANCH * D_MODEL), BF16))
    out_specs = (feat(wq), tok(wkv), feat(wkv), pl.BlockSpec((None, wb, tm), lambda i: (i, 0, 0)), tok(wb), feat(wb),
                 tok(C_WIDTH), tok(C_WIDTH), tok(N_BRANCH * D_MODEL))
    if not queries:
        keep = (1, 2, 4, 5, 6, 7)
        out_shape = tuple(out_shape[i] for i in keep)
        out_specs = tuple(out_specs[i] for i in keep)
    vmem = d * IN_WIDTH * 2 + 2 * tm * d * 4 + 2 * tm * IN_WIDTH * 2 + 6 * tm * d * 4 + (8 << 20)
    return pl.pallas_call(
        functools.partial(_in_proj_kernel, queries=queries),
        grid=(nt,),
        in_specs=[tok(d), modspec, modspec, _resident((1, d)), tabspec, tabspec, tabspec,
                  _layer_resident((d, IN_WIDTH), layer)],
        out_specs=out_specs,
        out_shape=out_shape,
        compiler_params=_params(("parallel",), vmem),
        name="in_proj",
    )(x2, shift, scale, g0, cos, sin_hi, sin_lo, w_bf)


def _sink_scores(k_all, qt):
    tq = qt.shape[1]
    zero = jnp.zeros((HEAD_DIM, tq), qt.dtype)
    scores = []
    for j in range(A_KV_HEADS):
        cols = []
        for g in range(A_GROUPS):
            hd = j * A_GROUPS + g
            qh = qt[hd * HEAD_DIM:(hd + 1) * HEAD_DIM, :]
            cols.append(jnp.concatenate([qh, zero] if j == 0 else [zero, qh], axis=0))
        qz = jnp.concatenate(cols, axis=1)
        scores.append(jnp.dot(k_all, qz, preferred_element_type=F32))
    return scores


def _sink_softmax_pv(scores, vt_all, sink_row, bias, n_win):
    tq = scores[0].shape[1] // A_GROUPS
    outs = []
    ones = jnp.ones((BF16_ROWS, vt_all.shape[1]), BF16)
    for j in range(A_KV_HEADS):
        s = scores[j]
        sink = sink_row[:, j * A_GROUPS * tq:(j + 1) * A_GROUPS * tq] * LOG2E
        if n_win:
            sw = s[:n_win] + jnp.concatenate([bias] * A_GROUPS, axis=1)
            sc = s[n_win:]
            m = jnp.maximum(jnp.maximum(jnp.max(sw, axis=0, keepdims=True),
                                        jnp.max(sc, axis=0, keepdims=True)), sink)
            p = jnp.concatenate([jnp.exp2(sw - m), jnp.exp2(sc - m)], axis=0).astype(BF16)
        else:
            m = jnp.maximum(jnp.max(s, axis=0, keepdims=True), sink)
            p = jnp.exp2(s - m).astype(BF16)
        vt_aug = jnp.concatenate([vt_all[j * HEAD_DIM:(j + 1) * HEAD_DIM, :], ones], axis=0)
        o = jnp.dot(vt_aug, p, preferred_element_type=F32)
        l = o[HEAD_DIM:HEAD_DIM + 1] + jnp.exp2(sink - m)
        o = o[:HEAD_DIM] * (1.0 / l)
        for g in range(A_GROUPS):
            outs.append(o[:, g * tq:(g + 1) * tq])
    return jnp.concatenate(outs, axis=0)


def _win_attn_kernel(q_ref, kp_ref, kc_ref, kn_ref, vp_ref, vc_ref, vn_ref, kx_ref, vx_ref,
                     sink_ref, o_ref, *, n_tok):
    tq = Q_BLOCK
    q_blocks = q_ref.shape[1] // tq
    first = pl.program_id(1) * q_blocks
    k_cat = jnp.concatenate([kp_ref[...], kc_ref[...], kn_ref[...]], axis=0)
    vt_cat = jnp.concatenate([vp_ref[...], vc_ref[...], vn_ref[...]], axis=1)
    kx = kx_ref[...]
    vx = vx_ref[...]
    n_win = 3 * tq
    r = lax.broadcasted_iota(jnp.int32, (n_win, tq), 0)
    c = lax.broadcasted_iota(jnp.int32, (n_win, tq), 1)
    band = jnp.where(jnp.abs(r - tq - c) <= WINDOW, 0.0, NEG_BIG).astype(F32)

    def scores(b):
        k_all = jnp.concatenate([k_cat[b * tq:(b + 3) * tq], kx], axis=0)
        return _sink_scores(k_all, q_ref[:, b * tq:(b + 1) * tq])

    s_next = scores(0)
    for b in range(q_blocks):
        s_cur = s_next
        if b + 1 < q_blocks:
            s_next = scores(b + 1)
        kpos = (first + b - 1) * tq + r
        bias = band + jnp.where(kpos >= 0, 0.0, NEG_BIG) + jnp.where(kpos < n_tok, 0.0, NEG_BIG)
        vt_all = jnp.concatenate([vt_cat[:, b * tq:(b + 3) * tq], vx], axis=1)
        yt = _sink_softmax_pv(s_cur, vt_all, sink_ref[...], bias, n_win)
        o_ref[b * tq:(b + 1) * tq, :] = yt.T.astype(o_ref.dtype)


def _ctx_attn_kernel(q_ref, kx_ref, vx_ref, sink_ref, o_ref):
    yt = _sink_softmax_pv(_sink_scores(kx_ref[...], q_ref[...]), vx_ref[...], sink_ref[...], None, 0)
    o_ref[...] = yt.T.astype(o_ref.dtype)


def _sink_row(sink, tq):
    return jnp.repeat(sink.astype(F32), tq).reshape(1, A_HEADS * tq)


def _window_attention(qt, k, vt, kx, vxt, sink, bsz, n_tok, n_ctx):
    tq = Q_BLOCK
    nq = n_tok // tq
    qb = min(WIN_Q_BLOCKS, nq)
    ns = nq // qb
    wq, wkv = A_HEADS * HEAD_DIM, A_KV_HEADS * HEAD_DIM
    prev = lambda b, n: b * nq + jnp.maximum(n * qb - 1, 0)
    cur = lambda b, n: b * ns + n
    nxt = lambda b, n: b * nq + jnp.minimum(n * qb + qb, nq - 1)
    edge_k = lambda f: pl.BlockSpec((tq, wkv), lambda b, n: (f(b, n), 0))
    edge_v = lambda f: pl.BlockSpec((wkv, tq), lambda b, n: (0, f(b, n)))
    return pl.pallas_call(
        functools.partial(_win_attn_kernel, n_tok=n_tok),
        grid=(bsz, ns),
        in_specs=[pl.BlockSpec((wq, qb * tq), lambda b, n: (0, cur(b, n))),
                  edge_k(prev), pl.BlockSpec((qb * tq, wkv), lambda b, n: (cur(b, n), 0)), edge_k(nxt),
                  edge_v(prev), pl.BlockSpec((wkv, qb * tq), lambda b, n: (0, cur(b, n))), edge_v(nxt),
                  pl.BlockSpec((n_ctx, wkv), lambda b, n: (b, 0)),
                  pl.BlockSpec((wkv, n_ctx), lambda b, n: (0, b)),
                  pl.BlockSpec((1, A_HEADS * tq), lambda b, n: (0, 0))],
        out_specs=pl.BlockSpec((qb * tq, wq), lambda b, n: (cur(b, n), 0)),
        out_shape=jax.ShapeDtypeStruct((bsz * n_tok, wq), BF16),
        compiler_params=_params(("parallel", "parallel"), 32 << 20),
        name="window_attention",
    )(qt, k, k, k, vt, vt, vt, kx, vxt, _sink_row(sink, tq))


def _context_attention(qt, kx, vxt, sink, bsz, n_ctx):
    tq = Q_BLOCK
    nq = n_ctx // tq
    wq, wkv = A_HEADS * HEAD_DIM, A_KV_HEADS * HEAD_DIM
    return pl.pallas_call(
        _ctx_attn_kernel,
        grid=(bsz, nq),
        in_specs=[pl.BlockSpec((wq, tq), lambda b, n: (0, b * nq + n)),
                  pl.BlockSpec((n_ctx, wkv), lambda b, n: (b, 0)),
                  pl.BlockSpec((wkv, n_ctx), lambda b, n: (0, b)),
                  pl.BlockSpec((1, A_HEADS * tq), lambda b, n: (0, 0))],
        out_specs=pl.BlockSpec((tq, wq), lambda b, n: (b * nq + n, 0)),
        out_shape=jax.ShapeDtypeStruct((bsz * n_ctx, wq), BF16),
        compiler_params=_params(("parallel", "parallel"), 32 << 20),
        name="context_attention",
    )(qt, kx, vxt, _sink_row(sink, tq))


def _diff_attn_kernel(*refs, n_lat_chunks, tk, lam_init, heads):
    if n_lat_chunks:
        q_ref, k_ref, v_ref, kx_ref, vx_ref, dl_ref, g_ref, o_ref, acc_ref = refs
    else:
        q_ref, kx_ref, vx_ref, dl_ref, g_ref, o_ref, acc_ref = refs
    n_q_tiles, _, tq = q_ref.shape
    hw = B_VDIM
    row = lax.broadcasted_iota(jnp.int32, (hw, tq), 0)
    zero = jnp.zeros((hw, tq), q_ref.dtype)
    n = n_lat_chunks + 1

    def head_slice(hd):
        return slice(hd * hw, (hd + 1) * hw)

    def keys(hd, i):
        if i < n_lat_chunks:
            return k_ref[i * tk:(i + 1) * tk, head_slice(hd)]
        return kx_ref[:, head_slice(hd)]

    def values(hd, i):
        v = v_ref[head_slice(hd), i * tk:(i + 1) * tk] if i < n_lat_chunks else vx_ref[head_slice(hd), :]
        return jnp.concatenate([v, jnp.ones((BF16_ROWS, v.shape[1]), BF16)], axis=0)

    def attend(qi):
        qz = []
        for hd in range(heads):
            qt = q_ref[qi, head_slice(hd), :]
            qz.append(jnp.concatenate([jnp.where(row < HEAD_DIM, qt, zero),
                                       jnp.where(row >= HEAD_DIM, qt, zero)], axis=1))

        def scores(hd, i):
            s = jnp.dot(keys(hd, i), qz[hd], preferred_element_type=F32)
            return s, jnp.max(s, axis=0, keepdims=True)

        m = [jnp.full((1, 2 * tq), NEG_BIG, F32) for _ in range(heads)]
        acc = [jnp.zeros((hw + BF16_ROWS, 2 * tq), F32) for _ in range(heads)]
        nxt = [scores(hd, 0) for hd in range(heads)]
        e_prev = [None] * heads
        alpha = [None] * heads
        for c in range(n):
            cur = list(nxt)
            for hd in range(heads):
                if c + 1 < n:
                    nxt[hd] = scores(hd, c + 1)
                if c:
                    acc[hd] = alpha[hd] * acc[hd] + jnp.dot(values(hd, c - 1), e_prev[hd],
                                                            preferred_element_type=F32)
            for hd in range(heads):
                s, smax = cur[hd]
                m_new = jnp.maximum(m[hd], smax)
                alpha[hd] = jnp.exp2(m[hd] - m_new)
                e_prev[hd] = jnp.exp2(s - m_new).astype(BF16)
                m[hd] = m_new
        return [alpha[hd] * acc[hd] + jnp.dot(values(hd, n - 1), e_prev[hd], preferred_element_type=F32)
                for hd in range(heads)]

    dl = dl_ref[...]
    lam = (jnp.exp(jnp.sum(dl[0:1] * dl[1:2], axis=1, keepdims=True))
           - jnp.exp(jnp.sum(dl[2:3] * dl[3:4], axis=1, keepdims=True)) + lam_init)

    def finish(qi, accs):
        for hd in range(heads):
            a = accs[hd]
            o = a[:hw] * (1.0 / a[hw:hw + 1])
            y = (o[:, :tq] - lam * o[:, tq:]).T
            y = (_rms(y, g_ref[...]) * (1.0 - lam_init)).astype(o_ref.dtype)
            o_ref[pl.ds(pl.multiple_of(qi * tq, tq), tq), head_slice(hd)] = y

    if n_q_tiles == 1:
        finish(0, attend(0))
        return

    first = attend(0)
    for hd in range(heads):
        acc_ref[hd] = first[hd]

    def body(qi, carry):
        parked = [acc_ref[hd] for hd in range(heads)]
        new = attend(qi)
        finish(qi - 1, parked)
        for hd in range(heads):
            acc_ref[hd] = new[hd]
        return carry
    lax.fori_loop(1, n_q_tiles, body, 0)
    finish(n_q_tiles - 1, [acc_ref[hd] for hd in range(heads)])


def _diff_attention(qt, k, vt, kx, vxt, diff_lambda, subln, lam_init, bsz, n_tok, n_ctx):
    nq_tok = n_tok if n_tok else n_ctx
    tq = qt.shape[2]
    nqt = nq_tok // tq
    tk = min(DIFF_TK, n_tok) if n_tok else 0
    heads = DIFF_HEADS_PER_STEP
    hw = heads * B_VDIM
    in_specs = [pl.BlockSpec((nqt, hw, tq), lambda b, h: (b, h, 0))]
    args = [qt]
    if n_tok:
        in_specs += [pl.BlockSpec((n_tok, hw), lambda b, h: (b, h)),
                     pl.BlockSpec((hw, n_tok), lambda b, h: (h, b))]
        args += [k, vt]
    in_specs += [pl.BlockSpec((n_ctx, hw), lambda b, h: (b, h)),
                 pl.BlockSpec((hw, n_ctx), lambda b, h: (h, b)),
                 pl.BlockSpec((4, HEAD_DIM), lambda b, h: (0, 0)),
                 pl.BlockSpec((1, B_VDIM), lambda b, h: (0, 0))]
    args += [kx, vxt, diff_lambda.astype(F32), subln.astype(F32).reshape(1, B_VDIM)]
    return pl.pallas_call(
        functools.partial(_diff_attn_kernel, n_lat_chunks=(n_tok // tk if n_tok else 0), tk=tk,
                          lam_init=lam_init, heads=heads),
        grid=(bsz, B_HEADS // heads),
        in_specs=in_specs,
        out_specs=pl.BlockSpec((nq_tok, hw), lambda b, h: (b, h)),
        out_shape=jax.ShapeDtypeStruct((bsz * nq_tok, B_HEADS * B_VDIM), BF16),
        scratch_shapes=[pltpu.VMEM((heads, B_VDIM + BF16_ROWS, 2 * tq), F32)],
        compiler_params=_params(("parallel", "parallel"), 48 << 20),
        name="diff_attention",
    )(*args)


def _scan_segments(a, b, h_in, reverse):
    rows, w = a.shape
    seg = rows // SUBLANES
    order = range(seg - 1, -1, -1) if reverse else range(seg)
    h_loc = [None] * seg
    a_cum = [None] * seg
    h = prod = None
    for i in order:
        ai = a[i * SUBLANES:(i + 1) * SUBLANES]
        bi = b[i * SUBLANES:(i + 1) * SUBLANES]
        h = bi if h is None else ai * h + bi
        prod = ai if prod is None else ai * prod
        h_loc[i] = h
        a_cum[i] = prod
    sub = lax.broadcasted_iota(jnp.int32, (SUBLANES, w), 0)
    e, p = h, prod
    for d in (1, 2, 4):
        keep = (sub < SUBLANES - d) if reverse else (sub >= d)
        shift = SUBLANES - d if reverse else d
        p_s = jnp.where(keep, pltpu.roll(p, shift, 0), 1.0)
        e_s = jnp.where(keep, pltpu.roll(e, shift, 0), 0.0)
        e = e + p * e_s
        p = p * p_s
    seg_out = e + p * h_in
    if reverse:
        seg_in = jnp.where(sub < SUBLANES - 1, pltpu.roll(seg_out, SUBLANES - 1, 0), h_in)
        h_out = seg_out[0:1]
    else:
        seg_in = jnp.where(sub >= 1, pltpu.roll(seg_out, 1, 0), h_in)
        h_out = seg_out[SUBLANES - 1:SUBLANES]
    return [h_loc[i] + a_cum[i] * seg_in for i in range(seg)], h_out


def _lru_kernel(cu_ref, cg_ref, h0f_ref, h0b_ref, cw_ref, cb_ref, wg_ref, bg_ref, lam_ref,
                y_ref, hf_ref, hb_ref, xpad_ref, hs_ref, u_ref, *, rows):
    n = cu_ref.shape[0]
    nchunks = n // rows
    w = C_WIDTH
    seg = rows // SUBLANES
    slabs = w // LANES

    def load(ref, idx):
        return jnp.concatenate([ref[j, idx, :] for j in range(slabs)], axis=1)

    def store(ref, idx, val):
        for j in range(slabs):
            ref[j, idx, :] = val[:, j * LANES:(j + 1) * LANES]

    store(xpad_ref, pl.ds(0, LRU_PAD), jnp.zeros((LRU_PAD, w), F32))
    store(xpad_ref, pl.ds(LRU_PAD + n, LRU_PAD), jnp.zeros((LRU_PAD, w), F32))

    def fill(ci, carry):
        s = pl.multiple_of(ci * rows, rows)
        store(xpad_ref, pl.ds(s + LRU_PAD, rows), cu_ref[pl.ds(s, rows), :].astype(F32))
        return carry
    lax.fori_loop(0, nchunks, fill, 0)

    lam = lam_ref[...]
    decay = LRU_C * (jnp.maximum(-lam, 0.0) + jnp.log1p(jnp.exp(-jnp.abs(lam))))
    cw = cw_ref[...]
    cb = cb_ref[...]

    def conv(s):
        first = LRU_PAD - CONV_W // 2
        taps = [load(xpad_ref, pl.ds(s + (first + q), SUBLANES, stride=seg))
                for q in range(seg + CONV_W - 1)]
        groups = []
        for i in range(seg):
            acc = cb
            for k in range(CONV_W):
                acc = acc + cw[k:k + 1] * taps[i + k]
            groups.append(acc)
        return jnp.concatenate(groups, axis=0)

    def coeffs(u, d):
        g = jnp.dot(u.astype(BF16), wg_ref[:, d * 2 * w:(d + 1) * 2 * w],
                    preferred_element_type=F32) + bg_ref[:, d * 2 * w:(d + 1) * 2 * w]
        r = jax.nn.sigmoid(g[:, :w])
        i_gate = jax.nn.sigmoid(g[:, w:])
        z = decay[d:d + 1] * r
        a = jnp.exp(-z)
        v = (1.0 + a * a) * jnp.tanh(z)
        root = jnp.where(v > 0.0, v * lax.rsqrt(v), 0.0)
        return a, root * i_gate * u

    r_i = lax.broadcasted_iota(jnp.int32, (rows, rows), 0)
    c_i = lax.broadcasted_iota(jnp.int32, (rows, rows), 1)
    to_tokens = jnp.where(c_i == (r_i % seg) * SUBLANES + r_i // seg, 1.0, 0.0).astype(BF16)
    to_segments = jnp.where(r_i == (c_i % seg) * SUBLANES + c_i // seg, 1.0, 0.0).astype(BF16)

    def bwd(ci, h):
        s = pl.multiple_of((nchunks - 1 - ci) * rows, rows)
        u = conv(s)
        u_ref[pl.ds(s, rows), :] = u
        a, bx = coeffs(u, 1)
        hs, h = _scan_segments(a, bx, h, True)
        hs_ref[pl.ds(s, rows), :] = jnp.concatenate(hs, axis=0)
        return h
    hb_ref[...] = lax.fori_loop(0, nchunks, bwd, h0b_ref[...])

    def fwd(ci, h):
        s = pl.multiple_of(ci * rows, rows)
        a, bx = coeffs(u_ref[pl.ds(s, rows), :], 0)
        hs, h = _scan_segments(a, bx, h, False)
        gate = jax.nn.gelu(jnp.dot(to_segments, cg_ref[pl.ds(s, rows), :], preferred_element_type=F32))
        y = ((jnp.concatenate(hs, axis=0) + hs_ref[pl.ds(s, rows), :]) * gate).astype(BF16)
        y_ref[pl.ds(s, rows), :] = jnp.dot(to_tokens, y, preferred_element_type=F32).astype(y_ref.dtype)
        return h
    hf_ref[...] = lax.fori_loop(0, nchunks, fwd, h0f_ref[...])


def _bidir_lru(cu, cg, h0f, h0b, conv_w, conv_b, wg_bf, bg, lam, layer, bsz, n):
    w = C_WIDTH
    rows = min(LRU_ROWS, n)
    seq = pl.BlockSpec((n, w), lambda b: (b, 0))
    st = pl.BlockSpec((None, 1, w), lambda b: (b, 0, 0))
    vmem = 6 * n * w * 2 + 2 * n * w * 4 + (n + 2 * LRU_PAD) * w * 4 + (12 << 20)
    return pl.pallas_call(
        functools.partial(_lru_kernel, rows=rows),
        grid=(bsz,),
        in_specs=[seq, seq, st, st, _resident((CONV_W, w)), _resident((1, w)),
                  _layer_resident((w, 4 * w), layer), _resident((1, 4 * w)), _resident((2, w))],
        out_specs=(seq, st, st),
        out_shape=(jax.ShapeDtypeStruct((bsz * n, w), BF16),
                   jax.ShapeDtypeStruct((bsz, 1, w), F32), jax.ShapeDtypeStruct((bsz, 1, w), F32)),
        scratch_shapes=[pltpu.VMEM((w // LANES, n + 2 * LRU_PAD, LANES), F32),
                        pltpu.VMEM((n, w), F32), pltpu.VMEM((n, w), F32)],
        compiler_params=_params(("parallel",), vmem),
        name="bidir_lru",
    )(cu, cg, h0f, h0b, conv_w, conv_b, wg_bf, bg, lam)


def _merge_kernel(ya_ref, yb_ref, yc_ref, gt_ref, x_ref, gx_ref, g_ref, wbr_ref, wout_ref, o_ref):
    sub = min(ROW_SUBTILE, x_ref.shape[0])
    for r in range(x_ref.shape[0] // sub):
        rows = slice(r * sub, (r + 1) * sub)
        acc = None
        for n, y_ref in enumerate((ya_ref, yb_ref, yc_ref)):
            z = jnp.dot(y_ref[rows, :], wbr_ref[n], preferred_element_type=F32)
            gate = jax.nn.sigmoid(gt_ref[rows, n * D_MODEL:(n + 1) * D_MODEL].astype(F32))
            acc = gate * z if acc is None else acc + gate * z
        mix = jnp.dot(acc.astype(BF16), wout_ref[...], preferred_element_type=F32)
        o_ref[rows, :] = x_ref[rows, :] + gx_ref[...] * _rms(mix, g_ref[...])


def _merge(ya, yb, yc, gt, x2, gate_x, g1, wbr_bf, wout_bf, layer, tm):
    t, d = x2.shape
    nt = t // tm
    nb = gate_x.shape[0]
    tok = lambda w: pl.BlockSpec((tm, w), lambda i: (i, 0))
    modspec = pl.BlockSpec((None, 1, d), lambda i: (i // (nt // nb), 0, 0))
    vmem = (N_BRANCH * BRANCH_W * d + d * d) * 2 + 2 * tm * (3 * BRANCH_W + 3 * d) * 2 \
        + 4 * tm * d * 4 + 6 * tm * d * 4 + (8 << 20)
    return pl.pallas_call(
        _merge_kernel,
        grid=(nt,),
        in_specs=[tok(BRANCH_W), tok(BRANCH_W), tok(BRANCH_W), tok(N_BRANCH * d), tok(d), modspec,
                  _resident((1, d)), _layer_resident((N_BRANCH, BRANCH_W, d), layer),
                  _layer_resident((d, d), layer)],
        out_specs=tok(d),
        out_shape=jax.ShapeDtypeStruct((t, d), F32),
        compiler_params=_params(("parallel",), vmem),
        name="gated_merge",
    )(ya, yb, yc, gt, x2, gate_x, g1, wbr_bf, wout_bf)


def _ffn_kernel(x_ref, sh_ref, sc_ref, gx_ref, g2_ref, g3_ref, w1_ref, w2_ref, o_ref, *, n_chunks):
    tiles = FFN_HIDDEN // MXU_TILE
    bounds = [(-(-tiles * c // n_chunks)) * MXU_TILE for c in range(n_chunks + 1)]
    sub = min(ROW_SUBTILE, x_ref.shape[0])
    for r in range(x_ref.shape[0] // sub):
        rows = slice(r * sub, (r + 1) * sub)
        x = x_ref[rows, :]
        h = _rms(x, g2_ref[...])
        h = (h * (1.0 + sc_ref[...]) + sh_ref[...]).astype(BF16)
        f = None
        for lo, hi in zip(bounds[:-1], bounds[1:]):
            gate = jnp.dot(h, w1_ref[:, lo:hi], preferred_element_type=F32)
            up = jnp.dot(h, w1_ref[:, FFN_HIDDEN + lo:FFN_HIDDEN + hi], preferred_element_type=F32)
            act = (gate * jax.nn.sigmoid(gate) * up).astype(BF16)
            part = jnp.dot(act, w2_ref[lo:hi, :], preferred_element_type=F32)
            f = part if f is None else f + part
        o_ref[rows, :] = x + gx_ref[...] * _rms(f, g3_ref[...])


def _ffn(x2, shift, scale, gate_x, g2, g3, w1_bf, w2_bf, layer, tm):
    t, d = x2.shape
    nt = t // tm
    nb = shift.shape[0]
    tok = pl.BlockSpec((tm, d), lambda i: (i, 0))
    modspec = pl.BlockSpec((None, 1, d), lambda i: (i // (nt // nb), 0, 0))
    n_chunks = 2
    vmem = 3 * d * FFN_HIDDEN * 2 + 4 * tm * d * 4 + 3 * tm * (FFN_HIDDEN // n_chunks) * 4 \
        + 4 * tm * d * 4 + (8 << 20)
    return pl.pallas_call(
        functools.partial(_ffn_kernel, n_chunks=n_chunks),
        grid=(nt,),
        in_specs=[tok, modspec, modspec, modspec, _resident((1, d)), _resident((1, d)),
                  _layer_resident((d, 2 * FFN_HIDDEN), layer), _layer_resident((FFN_HIDDEN, d), layer)],
        out_specs=tok,
        out_shape=jax.ShapeDtypeStruct((t, d), F32),
        compiler_params=_params(("parallel",), vmem),
        name="swiglu_ffn",
    )(x2, shift, scale, gate_x, g2, g3, w1_bf, w2_bf)


def _rope_tables(n_tokens):
    rd = HEAD_DIM // 4
    t = jnp.arange(n_tokens)
    pos = jnp.stack([t // GRID_W, t % GRID_W], axis=-1).astype(F32)
    inv = 1.0 / (ROPE_THETA ** (jnp.arange(rd, dtype=F32) * 2.0 / (HEAD_DIM // 2)))
    ang = pos[:, :, None] * inv
    cos = jnp.cos(ang)[:, :, None, :]
    sin = jnp.sin(ang)[:, :, None, :]
    zeros = jnp.zeros_like(sin)
    cos64 = jnp.concatenate([cos, cos], axis=2).reshape(n_tokens, HEAD_DIM)
    hi64 = jnp.concatenate([zeros, sin], axis=2).reshape(n_tokens, HEAD_DIM)
    lo64 = jnp.concatenate([-sin, zeros], axis=2).reshape(n_tokens, HEAD_DIM)
    rep = LANES // HEAD_DIM
    return jnp.tile(cos64, (1, rep)), jnp.tile(hi64, (1, rep)), jnp.tile(lo64, (1, rep))


def _identity_tables(rows):
    z = jnp.zeros((rows, LANES), F32)
    return jnp.ones((rows, LANES), F32), z, z


def _gate_weights(lru_w):
    eye = jnp.eye(C_BLOCKS, dtype=lru_w.dtype)
    dense = jnp.einsum("ldgncf,nm->lncdgmf", lru_w, eye)
    return dense.reshape(lru_w.shape[0], C_WIDTH, 4 * C_WIDTH).astype(BF16)


def kernel(x, c, ctx, c_ctx, w_mod, b_mod, norm_g, w_in, attn_sink, diff_lambda, diff_subln, conv_w,
           conv_b, lru_w, lru_b, lru_lambda, w_branch, w_out, w_ffn_in, w_ffn_out):
    bsz, n_tok, d = x.shape
    n_ctx = ctx.shape[1]
    depth = w_mod.shape[0]
    tm = min(512, n_tok)
    tm2 = min(2 * ROW_SUBTILE, n_tok)
    tmc = min(256, n_ctx)

    mod_rows = 2 * SUBLANES
    c_rows = jnp.zeros((mod_rows, d), F32).at[:bsz].set(c).at[bsz].set(c_ctx)
    mods = _modulation(c_rows, w_mod, b_mod)

    cos, sin_hi, sin_lo = _rope_tables(n_tok)
    cos_c, sin_hi_c, sin_lo_c = _identity_tables(tmc)

    x2 = x.reshape(bsz * n_tok, d)
    cx2 = ctx.reshape(bsz * n_ctx, d)
    zero_state = jnp.zeros((bsz, 1, C_WIDTH), F32)

    w_in_bf = w_in.astype(BF16)
    wg_bf = _gate_weights(lru_w)
    wbr_bf = w_branch.astype(BF16)
    wout_bf = w_out.astype(BF16)
    w1_bf = w_ffn_in.astype(BF16)
    w2_bf = w_ffn_out.astype(BF16)

    for l in range(depth):
        need_ctx = l < depth - 1
        lam_init = 0.8 - 0.6 * math.exp(-0.3 * l)
        mx = [mods[l, :bsz, k * d:(k + 1) * d].reshape(bsz, 1, d) for k in range(6)]
        mc = [mods[l, bsz:bsz + 1, k * d:(k + 1) * d].reshape(1, 1, d) for k in range(6)]
        g = [norm_g[l, k].reshape(1, d).astype(F32) for k in range(4)]
        bg = lru_b[l].astype(F32).reshape(1, 4 * C_WIDTH)
        cw = conv_w[l].astype(F32)
        cb = conv_b[l].astype(F32).reshape(1, C_WIDTH)
        lam = lru_lambda[l].astype(F32)

        ctx_proj = _in_proj(cx2, mc[0], mc[1], g[0], cos_c, sin_hi_c, sin_lo_c, w_in_bf, l, tmc,
                            queries=need_ctx)
        if need_ctx:
            qa_c, ka_c, va_c, qb_c, kb_c, vb_c, cu_c, cg_c, gt_c = ctx_proj
        else:
            ka_c, va_c, kb_c, vb_c, cu_c, cg_c = ctx_proj
        yc_c, hf_c, hb_c = _bidir_lru(cu_c, cg_c, zero_state, zero_state, cw, cb, wg_bf, bg, lam,
                                      l, bsz, n_ctx)

        (qa, ka, va, qb, kb, vb, cu, cg, gt) = _in_proj(
            x2, mx[0], mx[1], g[0], cos, sin_hi, sin_lo, w_in_bf, l, tm)
        ya = _window_attention(qa, ka, va, ka_c, va_c, attn_sink[l], bsz, n_tok, n_ctx)
        yb = _diff_attention(qb, kb, vb, kb_c, vb_c, diff_lambda[l], diff_subln[l], lam_init,
                             bsz, n_tok, n_ctx)
        yc, _, _ = _bidir_lru(cu, cg, hf_c, hb_c, cw, cb, wg_bf, bg, lam, l, bsz, n_tok)
        x2 = _merge(ya, yb, yc, gt, x2, mx[2], g[1], wbr_bf, wout_bf, l, tm2)
        x2 = _ffn(x2, mx[3], mx[4], mx[5], g[2], g[3], w1_bf, w2_bf, l, tm2)

        if need_ctx:
            ya_c = _context_attention(qa_c, ka_c, va_c, attn_sink[l], bsz, n_ctx)
            yb_c = _diff_attention(qb_c, None, None, kb_c, vb_c, diff_lambda[l], diff_subln[l],
                                   lam_init, bsz, 0, n_ctx)
            cx2 = _merge(ya_c, yb_c, yc_c, gt_c, cx2, mc[2], g[1], wbr_bf, wout_bf, l, tmc)
            cx2 = _ffn(cx2, mc[3], mc[4], mc[5], g[2], g[3], w1_bf, w2_bf, l, tmc)

    return x2.reshape(bsz, n_tok, d)
```

```python
import functools
import math

import jax
import jax.numpy as jnp
from jax import lax
from jax.experimental import pallas as pl
from jax.experimental.pallas import tpu as pltpu

F32 = jnp.float32
BF16 = jnp.bfloat16

D_MODEL = 1024
GRID_W = 64
HEAD_DIM = 64
ROPE_THETA = 10000.0
EPS = 1e-6
A_HEADS = 8
A_KV_HEADS = 2
A_GROUPS = A_HEADS // A_KV_HEADS
WINDOW = 128
B_HEADS = 4
B_VDIM = 2 * HEAD_DIM
C_WIDTH = 512
C_BLOCKS = 8
C_BW = C_WIDTH // C_BLOCKS
CONV_W = 4
LRU_C = 8.0
N_BRANCH = 3
BRANCH_W = 512
FFN_HIDDEN = -(-8 * D_MODEL // (3 * 256)) * 256

O_AQ = 0
O_AK = O_AQ + A_HEADS * HEAD_DIM
O_AV = O_AK + A_KV_HEADS * HEAD_DIM
O_BQ = O_AV + A_KV_HEADS * HEAD_DIM
O_BK = O_BQ + B_HEADS * 2 * HEAD_DIM
O_BV = O_BK + B_HEADS * 2 * HEAD_DIM
O_CU = O_BV + B_HEADS * B_VDIM
O_CG = O_CU + C_WIDTH
O_GT = O_CG + C_WIDTH
IN_WIDTH = O_GT + N_BRANCH * D_MODEL

LANES = 128
SUBLANES = 8
BF16_ROWS = 16
MXU_TILE = 256
VMEM_CAP_BYTES = 60 * 1024 * 1024
NEG_BIG = -1e30
Q_BLOCK = 128
WIN_Q_BLOCKS = 8
DIFF_TK = 512
DIFF_HEADS_PER_STEP = 2
ROW_SUBTILE = 512
LRU_ROWS = 512
LRU_PAD = 8
LOG2E = math.log2(math.e)
Q_SCALE = HEAD_DIM ** -0.5 * LOG2E


def _params(semantics, vmem_bytes):
    return pltpu.CompilerParams(dimension_semantics=semantics,
                                vmem_limit_bytes=min(int(vmem_bytes), VMEM_CAP_BYTES))


def _resident(shape):
    nd = len(shape)
    return pl.BlockSpec(shape, lambda *_: (0,) * nd, pipeline_mode=pl.Buffered(1))


def _layer_resident(shape, layer):
    nd = len(shape)
    return pl.BlockSpec((None,) + tuple(shape), lambda *_: (layer,) + (0,) * nd,
                        pipeline_mode=pl.Buffered(1))


def _rms(x, g):
    return x * lax.rsqrt(jnp.mean(x * x, axis=-1, keepdims=True) + EPS) * g


def _mod_kernel(c_ref, w_ref, b_ref, o_ref):
    c = c_ref[...]
    s = c * jax.nn.sigmoid(c)
    o_ref[...] = jnp.dot(s, w_ref[...], preferred_element_type=F32,
                         precision=lax.Precision.HIGHEST) + b_ref[...]


def _modulation(c_rows, w_mod, b_mod):
    depth, d, n = w_mod.shape
    rows = c_rows.shape[0]
    tn = 1536
    return pl.pallas_call(
        _mod_kernel,
        grid=(depth, n // tn),
        in_specs=[pl.BlockSpec((rows, d), lambda l, j: (0, 0)),
                  pl.BlockSpec((None, d, tn), lambda l, j: (l, 0, j)),
                  pl.BlockSpec((None, 1, tn), lambda l, j: (l, 0, j))],
        out_specs=pl.BlockSpec((None, rows, tn), lambda l, j: (l, 0, j)),
        out_shape=jax.ShapeDtypeStruct((depth, rows, n), F32),
        compiler_params=_params(("parallel", "parallel"), 32 << 20),
        name="modulation",
    )(c_rows, w_mod, b_mod.reshape(depth, 1, n))


def _in_proj_kernel(x_ref, sh_ref, sc_ref, g_ref, cos_ref, sa_ref, sb_ref, w_ref, *out_refs, queries):
    if queries:
        qa_ref, ka_ref, va_ref, qb_ref, kb_ref, vb_ref, cu_ref, cg_ref = out_refs
    else:
        ka_ref, va_ref, kb_ref, vb_ref, cu_ref, cg_ref = out_refs
    h = _rms(x_ref[...], g_ref[...])
    h = (h * (1.0 + sc_ref[...]) + sh_ref[...]).astype(BF16)

    def proj(lo, width):
        return jnp.dot(h, w_ref[:, lo:lo + width], preferred_element_type=F32)

    cos = cos_ref[...]
    sin_hi = sa_ref[...]
    sin_lo = sb_ref[...]

    def rope(a):
        outs = []
        for k in range(a.shape[1] // LANES):
            blk = a[:, k * LANES:(k + 1) * LANES]
            outs.append(blk * cos + pltpu.roll(blk, HEAD_DIM // 4, 1) * sin_hi
                        + pltpu.roll(blk, LANES - HEAD_DIM // 4, 1) * sin_lo)
        return jnp.concatenate(outs, axis=1)

    kv_a = proj(O_AK, O_BQ - O_AK)
    ka_ref[...] = rope(kv_a[:, :O_AV - O_AK]).astype(BF16)
    va_ref[...] = kv_a[:, O_AV - O_AK:].T.astype(BF16)
    kb_ref[...] = rope(proj(O_BK, O_BV - O_BK)).astype(BF16)
    vb_ref[...] = proj(O_BV, O_CU - O_BV).T.astype(BF16)
    cu_ref[...] = proj(O_CU, C_WIDTH).astype(BF16)
    cg_ref[...] = proj(O_CG, C_WIDTH).astype(BF16)
    if queries:
        qa_ref[...] = (rope(proj(O_AQ, O_AK - O_AQ)) * Q_SCALE).T.astype(BF16)
        qb_ref[...] = (rope(proj(O_BQ, O_BK - O_BQ)) * Q_SCALE).T.astype(BF16)


def _in_proj(x2, shift, scale, g0, cos, sin_hi, sin_lo, w_bf, layer, tm, queries=True):
    t, d = x2.shape
    nt = t // tm
    nb = shift.shape[0]
    npos = cos.shape[0] // tm
    tok = lambda w: pl.BlockSpec((tm, w), lambda i: (i, 0))
    feat = lambda w: pl.BlockSpec((w, tm), lambda i: (0, i))
    modspec = pl.BlockSpec((None, 1, d), lambda i: (i // (nt // nb), 0, 0))
    tabspec = pl.BlockSpec((tm, LANES), lambda i: (i % npos, 0))
    wq, wkv = A_HEADS * HEAD_DIM, A_KV_HEADS * HEAD_DIM
    wb = B_HEADS * B_VDIM
    out_shape = (
        jax.ShapeDtypeStruct((wq, t), BF16), jax.ShapeDtypeStruct((t, wkv), BF16),
        jax.ShapeDtypeStruct((wkv, t), BF16), jax.ShapeDtypeStruct((nt, wb, tm), BF16),
        jax.ShapeDtypeStruct((t, wb), BF16), jax.ShapeDtypeStruct((wb, t), BF16),
        jax.ShapeDtypeStruct((t, C_WIDTH), BF16), jax.ShapeDtypeStruct((t, C_WIDTH), BF16))
    out_specs = (feat(wq), tok(wkv), feat(wkv), pl.BlockSpec((None, wb, tm), lambda i: (i, 0, 0)), tok(wb), feat(wb),
                 tok(C_WIDTH), tok(C_WIDTH))
    if not queries:
        keep = (1, 2, 4, 5, 6, 7)
        out_shape = tuple(out_shape[i] for i in keep)
        out_specs = tuple(out_specs[i] for i in keep)
    vmem = d * IN_WIDTH * 2 + 2 * tm * d * 4 + 2 * tm * IN_WIDTH * 2 + 6 * tm * d * 4 + (8 << 20)
    return pl.pallas_call(
        functools.partial(_in_proj_kernel, queries=queries),
        grid=(nt,),
        in_specs=[tok(d), modspec, modspec, _resident((1, d)), tabspec, tabspec, tabspec,
                  _layer_resident((d, IN_WIDTH), layer)],
        out_specs=out_specs,
        out_shape=out_shape,
        compiler_params=_params(("parallel",), vmem),
        name="in_proj",
    )(x2, shift, scale, g0, cos, sin_hi, sin_lo, w_bf)


def _sink_scores(k_all, qt):
    tq = qt.shape[1]
    zero = jnp.zeros((HEAD_DIM, tq), qt.dtype)
    scores = []
    for j in range(A_KV_HEADS):
        cols = []
        for g in range(A_GROUPS):
            hd = j * A_GROUPS + g
            qh = qt[hd * HEAD_DIM:(hd + 1) * HEAD_DIM, :]
            cols.append(jnp.concatenate([qh, zero] if j == 0 else [zero, qh], axis=0))
        qz = jnp.concatenate(cols, axis=1)
        scores.append(jnp.dot(k_all, qz, preferred_element_type=F32))
    return scores


def _sink_softmax_pv(scores, vt_all, sink_row, bias, n_win):
    tq = scores[0].shape[1] // A_GROUPS
    outs = []
    ones = jnp.ones((BF16_ROWS, vt_all.shape[1]), BF16)
    for j in range(A_KV_HEADS):
        s = scores[j]
        sink = sink_row[:, j * A_GROUPS * tq:(j + 1) * A_GROUPS * tq] * LOG2E
        if n_win:
            sw = s[:n_win] + jnp.concatenate([bias] * A_GROUPS, axis=1)
            sc = s[n_win:]
            m = jnp.maximum(jnp.maximum(jnp.max(sw, axis=0, keepdims=True),
                                        jnp.max(sc, axis=0, keepdims=True)), sink)
            p = jnp.concatenate([jnp.exp2(sw - m), jnp.exp2(sc - m)], axis=0).astype(BF16)
        else:
            m = jnp.maximum(jnp.max(s, axis=0, keepdims=True), sink)
            p = jnp.exp2(s - m).astype(BF16)
        vt_aug = jnp.concatenate([vt_all[j * HEAD_DIM:(j + 1) * HEAD_DIM, :], ones], axis=0)
        o = jnp.dot(vt_aug, p, preferred_element_type=F32)
        l = o[HEAD_DIM:HEAD_DIM + 1] + jnp.exp2(sink - m)
        o = o[:HEAD_DIM] * (1.0 / l)
        for g in range(A_GROUPS):
            outs.append(o[:, g * tq:(g + 1) * tq])
    return jnp.concatenate(outs, axis=0)


def _win_attn_kernel(q_ref, kp_ref, kc_ref, kn_ref, vp_ref, vc_ref, vn_ref, kx_ref, vx_ref,
                     sink_ref, o_ref, *, n_tok):
    tq = Q_BLOCK
    q_blocks = q_ref.shape[1] // tq
    first = pl.program_id(1) * q_blocks
    k_cat = jnp.concatenate([kp_ref[...], kc_ref[...], kn_ref[...]], axis=0)
    vt_cat = jnp.concatenate([vp_ref[...], vc_ref[...], vn_ref[...]], axis=1)
    kx = kx_ref[...]
    vx = vx_ref[...]
    n_win = 3 * tq
    r = lax.broadcasted_iota(jnp.int32, (n_win, tq), 0)
    c = lax.broadcasted_iota(jnp.int32, (n_win, tq), 1)
    band = jnp.where(jnp.abs(r - tq - c) <= WINDOW, 0.0, NEG_BIG).astype(F32)

    def scores(b):
        k_all = jnp.concatenate([k_cat[b * tq:(b + 3) * tq], kx], axis=0)
        return _sink_scores(k_all, q_ref[:, b * tq:(b + 1) * tq])

    s_next = scores(0)
    for b in range(q_blocks):
        s_cur = s_next
        if b + 1 < q_blocks:
            s_next = scores(b + 1)
        kpos = (first + b - 1) * tq + r
        bias = band + jnp.where(kpos >= 0, 0.0, NEG_BIG) + jnp.where(kpos < n_tok, 0.0, NEG_BIG)
        vt_all = jnp.concatenate([vt_cat[:, b * tq:(b + 3) * tq], vx], axis=1)
        yt = _sink_softmax_pv(s_cur, vt_all, sink_ref[...], bias, n_win)
        o_ref[b * tq:(b + 1) * tq, :] = yt.T.astype(o_ref.dtype)


def _ctx_attn_kernel(q_ref, kx_ref, vx_ref, sink_ref, o_ref):
    yt = _sink_softmax_pv(_sink_scores(kx_ref[...], q_ref[...]), vx_ref[...], sink_ref[...], None, 0)
    o_ref[...] = yt.T.astype(o_ref.dtype)


def _sink_row(sink, tq):
    return jnp.repeat(sink.astype(F32), tq).reshape(1, A_HEADS * tq)


def _window_attention(qt, k, vt, kx, vxt, sink, bsz, n_tok, n_ctx):
    tq = Q_BLOCK
    nq = n_tok // tq
    qb = min(WIN_Q_BLOCKS, nq)
    ns = nq // qb
    wq, wkv = A_HEADS * HEAD_DIM, A_KV_HEADS * HEAD_DIM
    prev = lambda b, n: b * nq + jnp.maximum(n * qb - 1, 0)
    cur = lambda b, n: b * ns + n
    nxt = lambda b, n: b * nq + jnp.minimum(n * qb + qb, nq - 1)
    edge_k = lambda f: pl.BlockSpec((tq, wkv), lambda b, n: (f(b, n), 0))
    edge_v = lambda f: pl.BlockSpec((wkv, tq), lambda b, n: (0, f(b, n)))
    return pl.pallas_call(
        functools.partial(_win_attn_kernel, n_tok=n_tok),
        grid=(bsz, ns),
        in_specs=[pl.BlockSpec((wq, qb * tq), lambda b, n: (0, cur(b, n))),
                  edge_k(prev), pl.BlockSpec((qb * tq, wkv), lambda b, n: (cur(b, n), 0)), edge_k(nxt),
                  edge_v(prev), pl.BlockSpec((wkv, qb * tq), lambda b, n: (0, cur(b, n))), edge_v(nxt),
                  pl.BlockSpec((n_ctx, wkv), lambda b, n: (b, 0)),
                  pl.BlockSpec((wkv, n_ctx), lambda b, n: (0, b)),
                  pl.BlockSpec((1, A_HEADS * tq), lambda b, n: (0, 0))],
        out_specs=pl.BlockSpec((qb * tq, wq), lambda b, n: (cur(b, n), 0)),
        out_shape=jax.ShapeDtypeStruct((bsz * n_tok, wq), BF16),
        compiler_params=_params(("parallel", "parallel"), 32 << 20),
        name="window_attention",
    )(qt, k, k, k, vt, vt, vt, kx, vxt, _sink_row(sink, tq))


def _context_attention(qt, kx, vxt, sink, bsz, n_ctx):
    tq = Q_BLOCK
    nq = n_ctx // tq
    wq, wkv = A_HEADS * HEAD_DIM, A_KV_HEADS * HEAD_DIM
    return pl.pallas_call(
        _ctx_attn_kernel,
        grid=(bsz, nq),
        in_specs=[pl.BlockSpec((wq, tq), lambda b, n: (0, b * nq + n)),
                  pl.BlockSpec((n_ctx, wkv), lambda b, n: (b, 0)),
                  pl.BlockSpec((wkv, n_ctx), lambda b, n: (0, b)),
                  pl.BlockSpec((1, A_HEADS * tq), lambda b, n: (0, 0))],
        out_specs=pl.BlockSpec((tq, wq), lambda b, n: (b * nq + n, 0)),
        out_shape=jax.ShapeDtypeStruct((bsz * n_ctx, wq), BF16),
        compiler_params=_params(("parallel", "parallel"), 32 << 20),
        name="context_attention",
    )(qt, kx, vxt, _sink_row(sink, tq))


def _diff_attn_kernel(*refs, n_lat_chunks, tk, lam_init, heads):
    if n_lat_chunks:
        q_ref, k_ref, v_ref, kx_ref, vx_ref, dl_ref, g_ref, o_ref, acc_ref = refs
    else:
        q_ref, kx_ref, vx_ref, dl_ref, g_ref, o_ref, acc_ref = refs
    n_q_tiles, _, tq = q_ref.shape
    hw = B_VDIM
    row = lax.broadcasted_iota(jnp.int32, (hw, tq), 0)
    zero = jnp.zeros((hw, tq), q_ref.dtype)
    n = n_lat_chunks + 1

    def head_slice(hd):
        return slice(hd * hw, (hd + 1) * hw)

    def keys(hd, i):
        if i < n_lat_chunks:
            return k_ref[i * tk:(i + 1) * tk, head_slice(hd)]
        return kx_ref[:, head_slice(hd)]

    def values(hd, i):
        v = v_ref[head_slice(hd), i * tk:(i + 1) * tk] if i < n_lat_chunks else vx_ref[head_slice(hd), :]
        return jnp.concatenate([v, jnp.ones((BF16_ROWS, v.shape[1]), BF16)], axis=0)

    def attend(qi):
        qz = []
        for hd in range(heads):
            qt = q_ref[qi, head_slice(hd), :]
            qz.append(jnp.concatenate([jnp.where(row < HEAD_DIM, qt, zero),
                                       jnp.where(row >= HEAD_DIM, qt, zero)], axis=1))

        def scores(hd, i):
            s = jnp.dot(keys(hd, i), qz[hd], preferred_element_type=F32)
            return s, jnp.max(s, axis=0, keepdims=True)

        m = [jnp.full((1, 2 * tq), NEG_BIG, F32) for _ in range(heads)]
        acc = [jnp.zeros((hw + BF16_ROWS, 2 * tq), F32) for _ in range(heads)]
        nxt = [scores(hd, 0) for hd in range(heads)]
        e_prev = [None] * heads
        alpha = [None] * heads
        for c in range(n):
            cur = list(nxt)
            for hd in range(heads):
                if c + 1 < n:
                    nxt[hd] = scores(hd, c + 1)
                if c:
                    acc[hd] = alpha[hd] * acc[hd] + jnp.dot(values(hd, c - 1), e_prev[hd],
                                                            preferred_element_type=F32)
            for hd in range(heads):
                s, smax = cur[hd]
                m_new = jnp.maximum(m[hd], smax)
                alpha[hd] = jnp.exp2(m[hd] - m_new)
                e_prev[hd] = jnp.exp2(s - m_new).astype(BF16)
                m[hd] = m_new
        return [alpha[hd] * acc[hd] + jnp.dot(values(hd, n - 1), e_prev[hd], preferred_element_type=F32)
                for hd in range(heads)]

    dl = dl_ref[...]
    lam = (jnp.exp(jnp.sum(dl[0:1] * dl[1:2], axis=1, keepdims=True))
           - jnp.exp(jnp.sum(dl[2:3] * dl[3:4], axis=1, keepdims=True)) + lam_init)

    def finish(qi, accs):
        for hd in range(heads):
            a = accs[hd]
            o = a[:hw] * (1.0 / a[hw:hw + 1])
            y = (o[:, :tq] - lam * o[:, tq:]).T
            y = (_rms(y, g_ref[...]) * (1.0 - lam_init)).astype(o_ref.dtype)
            o_ref[pl.ds(pl.multiple_of(qi * tq, tq), tq), head_slice(hd)] = y

    if n_q_tiles == 1:
        finish(0, attend(0))
        return

    first = attend(0)
    for hd in range(heads):
        acc_ref[hd] = first[hd]

    def body(qi, carry):
        parked = [acc_ref[hd] for hd in range(heads)]
        new = attend(qi)
        finish(qi - 1, parked)
        for hd in range(heads):
            acc_ref[hd] = new[hd]
        return carry
    lax.fori_loop(1, n_q_tiles, body, 0)
    finish(n_q_tiles - 1, [acc_ref[hd] for hd in range(heads)])


def _diff_attention(qt, k, vt, kx, vxt, diff_lambda, subln, lam_init, bsz, n_tok, n_ctx):
    nq_tok = n_tok if n_tok else n_ctx
    tq = qt.shape[2]
    nqt = nq_tok // tq
    tk = min(DIFF_TK, n_tok) if n_tok else 0
    heads = DIFF_HEADS_PER_STEP
    hw = heads * B_VDIM
    in_specs = [pl.BlockSpec((nqt, hw, tq), lambda b, h: (b, h, 0))]
    args = [qt]
    if n_tok:
        in_specs += [pl.BlockSpec((n_tok, hw), lambda b, h: (b, h)),
                     pl.BlockSpec((hw, n_tok), lambda b, h: (h, b))]
        args += [k, vt]
    in_specs += [pl.BlockSpec((n_ctx, hw), lambda b, h: (b, h)),
                 pl.BlockSpec((hw, n_ctx), lambda b, h: (h, b)),
                 pl.BlockSpec((4, HEAD_DIM), lambda b, h: (0, 0)),
                 pl.BlockSpec((1, B_VDIM), lambda b, h: (0, 0))]
    args += [kx, vxt, diff_lambda.astype(F32), subln.astype(F32).reshape(1, B_VDIM)]
    return pl.pallas_call(
        functools.partial(_diff_attn_kernel, n_lat_chunks=(n_tok // tk if n_tok else 0), tk=tk,
                          lam_init=lam_init, heads=heads),
        grid=(bsz, B_HEADS // heads),
        in_specs=in_specs,
        out_specs=pl.BlockSpec((nq_tok, hw), lambda b, h: (b, h)),
        out_shape=jax.ShapeDtypeStruct((bsz * nq_tok, B_HEADS * B_VDIM), BF16),
        scratch_shapes=[pltpu.VMEM((heads, B_VDIM + BF16_ROWS, 2 * tq), F32)],
        compiler_params=_params(("parallel", "parallel"), 48 << 20),
        name="diff_attention",
    )(*args)


def _scan_segments(a, b, h_in, reverse):
    rows, w = a.shape
    seg = rows // SUBLANES
    order = range(seg - 1, -1, -1) if reverse else range(seg)
    h_loc = [None] * seg
    a_cum = [None] * seg
    h = prod = None
    for i in order:
        ai = a[i * SUBLANES:(i + 1) * SUBLANES]
        bi = b[i * SUBLANES:(i + 1) * SUBLANES]
        h = bi if h is None else ai * h + bi
        prod = ai if prod is None else ai * prod
        h_loc[i] = h
        a_cum[i] = prod
    sub = lax.broadcasted_iota(jnp.int32, (SUBLANES, w), 0)
    e, p = h, prod
    for d in (1, 2, 4):
        keep = (sub < SUBLANES - d) if reverse else (sub >= d)
        shift = SUBLANES - d if reverse else d
        p_s = jnp.where(keep, pltpu.roll(p, shift, 0), 1.0)
        e_s = jnp.where(keep, pltpu.roll(e, shift, 0), 0.0)
        e = e + p * e_s
        p = p * p_s
    seg_out = e + p * h_in
    if reverse:
        seg_in = jnp.where(sub < SUBLANES - 1, pltpu.roll(seg_out, SUBLANES - 1, 0), h_in)
        h_out = seg_out[0:1]
    else:
        seg_in = jnp.where(sub >= 1, pltpu.roll(seg_out, 1, 0), h_in)
        h_out = seg_out[SUBLANES - 1:SUBLANES]
    return [h_loc[i] + a_cum[i] * seg_in for i in range(seg)], h_out


def _lru_kernel(cu_ref, cg_ref, h0f_ref, h0b_ref, cw_ref, cb_ref, wg_ref, bg_ref, lam_ref,
                y_ref, hf_ref, hb_ref, xpad_ref, hs_ref, u_ref, *, rows):
    n = cu_ref.shape[0]
    nchunks = n // rows
    w = C_WIDTH
    seg = rows // SUBLANES
    slabs = w // LANES

    def load(ref, idx):
        return jnp.concatenate([ref[j, idx, :] for j in range(slabs)], axis=1)

    def store(ref, idx, val):
        for j in range(slabs):
            ref[j, idx, :] = val[:, j * LANES:(j + 1) * LANES]

    store(xpad_ref, pl.ds(0, LRU_PAD), jnp.zeros((LRU_PAD, w), F32))
    store(xpad_ref, pl.ds(LRU_PAD + n, LRU_PAD), jnp.zeros((LRU_PAD, w), F32))

    def fill(ci, carry):
        s = pl.multiple_of(ci * rows, rows)
        store(xpad_ref, pl.ds(s + LRU_PAD, rows), cu_ref[pl.ds(s, rows), :].astype(F32))
        return carry
    lax.fori_loop(0, nchunks, fill, 0)

    lam = lam_ref[...]
    decay = LRU_C * (jnp.maximum(-lam, 0.0) + jnp.log1p(jnp.exp(-jnp.abs(lam))))
    cw = cw_ref[...]
    cb = cb_ref[...]

    def conv(s):
        first = LRU_PAD - CONV_W // 2
        taps = [load(xpad_ref, pl.ds(s + (first + q), SUBLANES, stride=seg))
                for q in range(seg + CONV_W - 1)]
        groups = []
        for i in range(seg):
            acc = cb
            for k in range(CONV_W):
                acc = acc + cw[k:k + 1] * taps[i + k]
            groups.append(acc)
        return jnp.concatenate(groups, axis=0)

    def coeffs(u, d):
        g = jnp.dot(u.astype(BF16), wg_ref[:, d * 2 * w:(d + 1) * 2 * w],
                    preferred_element_type=F32) + bg_ref[:, d * 2 * w:(d + 1) * 2 * w]
        r = jax.nn.sigmoid(g[:, :w])
        i_gate = jax.nn.sigmoid(g[:, w:])
        z = decay[d:d + 1] * r
        a = jnp.exp(-z)
        v = (1.0 + a * a) * jnp.tanh(z)
        root = jnp.where(v > 0.0, v * lax.rsqrt(v), 0.0)
        return a, root * i_gate * u

    r_i = lax.broadcasted_iota(jnp.int32, (rows, rows), 0)
    c_i = lax.broadcasted_iota(jnp.int32, (rows, rows), 1)
    to_tokens = jnp.where(c_i == (r_i % seg) * SUBLANES + r_i // seg, 1.0, 0.0).astype(BF16)
    to_segments = jnp.where(r_i == (c_i % seg) * SUBLANES + c_i // seg, 1.0, 0.0).astype(BF16)

    def bwd(ci, h):
        s = pl.multiple_of((nchunks - 1 - ci) * rows, rows)
        u = conv(s)
        u_ref[pl.ds(s, rows), :] = u
        a, bx = coeffs(u, 1)
        hs, h = _scan_segments(a, bx, h, True)
        hs_ref[pl.ds(s, rows), :] = jnp.concatenate(hs, axis=0)
        return h
    hb_ref[...] = lax.fori_loop(0, nchunks, bwd, h0b_ref[...])

    def fwd(ci, h):
        s = pl.multiple_of(ci * rows, rows)
        a, bx = coeffs(u_ref[pl.ds(s, rows), :], 0)
        hs, h = _scan_segments(a, bx, h, False)
        gate = jax.nn.gelu(jnp.dot(to_segments, cg_ref[pl.ds(s, rows), :], preferred_element_type=F32))
        y = ((jnp.concatenate(hs, axis=0) + hs_ref[pl.ds(s, rows), :]) * gate).astype(BF16)
        y_ref[pl.ds(s, rows), :] = jnp.dot(to_tokens, y, preferred_element_type=F32).astype(y_ref.dtype)
        return h
    hf_ref[...] = lax.fori_loop(0, nchunks, fwd, h0f_ref[...])


def _bidir_lru(cu, cg, h0f, h0b, conv_w, conv_b, wg_bf, bg, lam, layer, bsz, n):
    w = C_WIDTH
    rows = min(LRU_ROWS, n)
    seq = pl.BlockSpec((n, w), lambda b: (b, 0))
    st = pl.BlockSpec((None, 1, w), lambda b: (b, 0, 0))
    vmem = 6 * n * w * 2 + 2 * n * w * 4 + (n + 2 * LRU_PAD) * w * 4 + (12 << 20)
    return pl.pallas_call(
        functools.partial(_lru_kernel, rows=rows),
        grid=(bsz,),
        in_specs=[seq, seq, st, st, _resident((CONV_W, w)), _resident((1, w)),
                  _layer_resident((w, 4 * w), layer), _resident((1, 4 * w)), _resident((2, w))],
        out_specs=(seq, st, st),
        out_shape=(jax.ShapeDtypeStruct((bsz * n, w), BF16),
                   jax.ShapeDtypeStruct((bsz, 1, w), F32), jax.ShapeDtypeStruct((bsz, 1, w), F32)),
        scratch_shapes=[pltpu.VMEM((w // LANES, n + 2 * LRU_PAD, LANES), F32),
                        pltpu.VMEM((n, w), F32), pltpu.VMEM((n, w), F32)],
        compiler_params=_params(("parallel",), vmem),
        name="bidir_lru",
    )(cu, cg, h0f, h0b, conv_w, conv_b, wg_bf, bg, lam)


def _merge_kernel(ya_ref, yb_ref, yc_ref, x_ref, sh_ref, sc_ref, gx_ref, g0_ref, g_ref, wgt_ref, wbr_ref,
                  wout_ref, o_ref):
    sub = min(ROW_SUBTILE, x_ref.shape[0])
    for r in range(x_ref.shape[0] // sub):
        rows = slice(r * sub, (r + 1) * sub)
        x = x_ref[rows, :]
        h = (_rms(x, g0_ref[...]) * (1.0 + sc_ref[...]) + sh_ref[...]).astype(BF16)
        acc = None
        for n, y_ref in enumerate((ya_ref, yb_ref, yc_ref)):
            z = jnp.dot(y_ref[rows, :], wbr_ref[n], preferred_element_type=F32)
            gate = jax.nn.sigmoid(jnp.dot(h, wgt_ref[:, n * D_MODEL:(n + 1) * D_MODEL],
                                          preferred_element_type=F32))
            acc = gate * z if acc is None else acc + gate * z
        mix = jnp.dot(acc.astype(BF16), wout_ref[...], preferred_element_type=F32)
        o_ref[rows, :] = x + gx_ref[...] * _rms(mix, g_ref[...])


def _merge(ya, yb, yc, x2, shift, scale, gate_x, g0, g1, wgt_bf, wbr_bf, wout_bf, layer, tm):
    t, d = x2.shape
    nt = t // tm
    nb = gate_x.shape[0]
    tok = lambda w: pl.BlockSpec((tm, w), lambda i: (i, 0))
    modspec = pl.BlockSpec((None, 1, d), lambda i: (i // (nt // nb), 0, 0))
    vmem = (N_BRANCH * BRANCH_W * d + d * d + N_BRANCH * d * d) * 2 + 2 * tm * 3 * BRANCH_W * 2 \
        + 4 * tm * d * 4 + 8 * tm * d * 4 + (8 << 20)
    return pl.pallas_call(
        _merge_kernel,
        grid=(nt,),
        in_specs=[tok(BRANCH_W), tok(BRANCH_W), tok(BRANCH_W), tok(d), modspec, modspec, modspec,
                  _resident((1, d)), _resident((1, d)), _layer_resident((d, N_BRANCH * d), layer),
                  _layer_resident((N_BRANCH, BRANCH_W, d), layer), _layer_resident((d, d), layer)],
        out_specs=tok(d),
        out_shape=jax.ShapeDtypeStruct((t, d), F32),
        compiler_params=_params(("parallel",), vmem),
        name="gated_merge",
    )(ya, yb, yc, x2, shift, scale, gate_x, g0, g1, wgt_bf, wbr_bf, wout_bf)


def _ffn_kernel(x_ref, sh_ref, sc_ref, gx_ref, g2_ref, g3_ref, w1_ref, w2_ref, o_ref, *, n_chunks):
    tiles = FFN_HIDDEN // MXU_TILE
    bounds = [(-(-tiles * c // n_chunks)) * MXU_TILE for c in range(n_chunks + 1)]
    sub = min(ROW_SUBTILE, x_ref.shape[0])
    for r in range(x_ref.shape[0] // sub):
        rows = slice(r * sub, (r + 1) * sub)
        x = x_ref[rows, :]
        h = _rms(x, g2_ref[...])
        h = (h * (1.0 + sc_ref[...]) + sh_ref[...]).astype(BF16)
        f = None
        for lo, hi in zip(bounds[:-1], bounds[1:]):
            gate = jnp.dot(h, w1_ref[:, lo:hi], preferred_element_type=F32)
            up = jnp.dot(h, w1_ref[:, FFN_HIDDEN + lo:FFN_HIDDEN + hi], preferred_element_type=F32)
            act = (gate * jax.nn.sigmoid(gate) * up).astype(BF16)
            part = jnp.dot(act, w2_ref[lo:hi, :], preferred_element_type=F32)
            f = part if f is None else f + part
        o_ref[rows, :] = x + gx_ref[...] * _rms(f, g3_ref[...])


def _ffn(x2, shift, scale, gate_x, g2, g3, w1_bf, w2_bf, layer, tm):
    t, d = x2.shape
    nt = t // tm
    nb = shift.shape[0]
    tok = pl.BlockSpec((tm, d), lambda i: (i, 0))
    modspec = pl.BlockSpec((None, 1, d), lambda i: (i // (nt // nb), 0, 0))
    n_chunks = 2
    vmem = 3 * d * FFN_HIDDEN * 2 + 4 * tm * d * 4 + 3 * tm * (FFN_HIDDEN // n_chunks) * 4 \
        + 4 * tm * d * 4 + (8 << 20)
    return pl.pallas_call(
        functools.partial(_ffn_kernel, n_chunks=n_chunks),
        grid=(nt,),
        in_specs=[tok, modspec, modspec, modspec, _resident((1, d)), _resident((1, d)),
                  _layer_resident((d, 2 * FFN_HIDDEN), layer), _layer_resident((FFN_HIDDEN, d), layer)],
        out_specs=tok,
        out_shape=jax.ShapeDtypeStruct((t, d), F32),
        compiler_params=_params(("parallel",), vmem),
        name="swiglu_ffn",
    )(x2, shift, scale, gate_x, g2, g3, w1_bf, w2_bf)


def _rope_tables(n_tokens):
    rd = HEAD_DIM // 4
    t = jnp.arange(n_tokens)
    pos = jnp.stack([t // GRID_W, t % GRID_W], axis=-1).astype(F32)
    inv = 1.0 / (ROPE_THETA ** (jnp.arange(rd, dtype=F32) * 2.0 / (HEAD_DIM // 2)))
    ang = pos[:, :, None] * inv
    cos = jnp.cos(ang)[:, :, None, :]
    sin = jnp.sin(ang)[:, :, None, :]
    zeros = jnp.zeros_like(sin)
    cos64 = jnp.concatenate([cos, cos], axis=2).reshape(n_tokens, HEAD_DIM)
    hi64 = jnp.concatenate([zeros, sin], axis=2).reshape(n_tokens, HEAD_DIM)
    lo64 = jnp.concatenate([-sin, zeros], axis=2).reshape(n_tokens, HEAD_DIM)
    rep = LANES // HEAD_DIM
    return jnp.tile(cos64, (1, rep)), jnp.tile(hi64, (1, rep)), jnp.tile(lo64, (1, rep))


def _identity_tables(rows):
    z = jnp.zeros((rows, LANES), F32)
    return jnp.ones((rows, LANES), F32), z, z


def _gate_weights(lru_w):
    eye = jnp.eye(C_BLOCKS, dtype=lru_w.dtype)
    dense = jnp.einsum("ldgncf,nm->lncdgmf", lru_w, eye)
    return dense.reshape(lru_w.shape[0], C_WIDTH, 4 * C_WIDTH).astype(BF16)


def kernel(x, c, ctx, c_ctx, w_mod, b_mod, norm_g, w_in, attn_sink, diff_lambda, diff_subln, conv_w,
           conv_b, lru_w, lru_b, lru_lambda, w_branch, w_out, w_ffn_in, w_ffn_out):
    bsz, n_tok, d = x.shape
    n_ctx = ctx.shape[1]
    depth = w_mod.shape[0]
    tm = min(512, n_tok)
    tm2 = min(2 * ROW_SUBTILE, n_tok)
    tmc = min(256, n_ctx)

    mod_rows = 2 * SUBLANES
    c_rows = jnp.zeros((mod_rows, d), F32).at[:bsz].set(c).at[bsz].set(c_ctx)
    mods = _modulation(c_rows, w_mod, b_mod)

    cos, sin_hi, sin_lo = _rope_tables(n_tok)
    cos_c, sin_hi_c, sin_lo_c = _identity_tables(tmc)

    x2 = x.reshape(bsz * n_tok, d)
    cx2 = ctx.reshape(bsz * n_ctx, d)
    zero_state = jnp.zeros((bsz, 1, C_WIDTH), F32)

    w_in_bf = w_in.astype(BF16)
    wgt_bf = w_in[:, :, O_GT:].astype(BF16)
    wg_bf = _gate_weights(lru_w)
    wbr_bf = w_branch.astype(BF16)
    wout_bf = w_out.astype(BF16)
    w1_bf = w_ffn_in.astype(BF16)
    w2_bf = w_ffn_out.astype(BF16)

    for l in range(depth):
        need_ctx = l < depth - 1
        lam_init = 0.8 - 0.6 * math.exp(-0.3 * l)
        mx = [mods[l, :bsz, k * d:(k + 1) * d].reshape(bsz, 1, d) for k in range(6)]
        mc = [mods[l, bsz:bsz + 1, k * d:(k + 1) * d].reshape(1, 1, d) for k in range(6)]
        g = [norm_g[l, k].reshape(1, d).astype(F32) for k in range(4)]
        bg = lru_b[l].astype(F32).reshape(1, 4 * C_WIDTH)
        cw = conv_w[l].astype(F32)
        cb = conv_b[l].astype(F32).reshape(1, C_WIDTH)
        lam = lru_lambda[l].astype(F32)

        ctx_proj = _in_proj(cx2, mc[0], mc[1], g[0], cos_c, sin_hi_c, sin_lo_c, w_in_bf, l, tmc,
                            queries=need_ctx)
        if need_ctx:
            qa_c, ka_c, va_c, qb_c, kb_c, vb_c, cu_c, cg_c = ctx_proj
        else:
            ka_c, va_c, kb_c, vb_c, cu_c, cg_c = ctx_proj
        yc_c, hf_c, hb_c = _bidir_lru(cu_c, cg_c, zero_state, zero_state, cw, cb, wg_bf, bg, lam,
                                      l, bsz, n_ctx)

        (qa, ka, va, qb, kb, vb, cu, cg) = _in_proj(
            x2, mx[0], mx[1], g[0], cos, sin_hi, sin_lo, w_in_bf, l, tm)
        ya = _window_attention(qa, ka, va, ka_c, va_c, attn_sink[l], bsz, n_tok, n_ctx)
        yb = _diff_attention(qb, kb, vb, kb_c, vb_c, diff_lambda[l], diff_subln[l], lam_init,
                             bsz, n_tok, n_ctx)
        yc, _, _ = _bidir_lru(cu, cg, hf_c, hb_c, cw, cb, wg_bf, bg, lam, l, bsz, n_tok)
        x2 = _merge(ya, yb, yc, x2, mx[0], mx[1], mx[2], g[0], g[1], wgt_bf, wbr_bf, wout_bf, l, tm2)
        x2 = _ffn(x2, mx[3], mx[4], mx[5], g[2], g[3], w1_bf, w2_bf, l, tm2)

        if need_ctx:
            ya_c = _context_attention(qa_c, ka_c, va_c, attn_sink[l], bsz, n_ctx)
            yb_c = _diff_attention(qb_c, None, None, kb_c, vb_c, diff_lambda[l], diff_subln[l],
                                   lam_init, bsz, 0, n_ctx)
            cx2 = _merge(ya_c, yb_c, yc_c, cx2, mc[0], mc[1], mc[2], g[0], g[1], wgt_bf, wbr_bf, wout_bf,
                         l, tmc)
            cx2 = _ffn(cx2, mc[3], mc[4], mc[5], g[2], g[3], w1_bf, w2_bf, l, tmc)

    return x2.reshape(bsz, n_tok, d)
```

```python
import functools
import math

import jax
import jax.numpy as jnp
from jax import lax
from jax.experimental import pallas as pl
from jax.experimental.pallas import tpu as pltpu

F32 = jnp.float32
BF16 = jnp.bfloat16

D_MODEL = 1024
GRID_W = 64
HEAD_DIM = 64
ROPE_THETA = 10000.0
EPS = 1e-6
A_HEADS = 8
A_KV_HEADS = 2
A_GROUPS = A_HEADS // A_KV_HEADS
WINDOW = 128
B_HEADS = 4
B_VDIM = 2 * HEAD_DIM
C_WIDTH = 512
C_BLOCKS = 8
C_BW = C_WIDTH // C_BLOCKS
CONV_W = 4
LRU_C = 8.0
N_BRANCH = 3
BRANCH_W = 512
FFN_HIDDEN = -(-8 * D_MODEL // (3 * 256)) * 256

O_AQ = 0
O_AK = O_AQ + A_HEADS * HEAD_DIM
O_AV = O_AK + A_KV_HEADS * HEAD_DIM
O_BQ = O_AV + A_KV_HEADS * HEAD_DIM
O_BK = O_BQ + B_HEADS * 2 * HEAD_DIM
O_BV = O_BK + B_HEADS * 2 * HEAD_DIM
O_CU = O_BV + B_HEADS * B_VDIM
O_CG = O_CU + C_WIDTH
O_GT = O_CG + C_WIDTH
IN_WIDTH = O_GT + N_BRANCH * D_MODEL

LANES = 128
SUBLANES = 8
BF16_ROWS = 16
MXU_TILE = 256
VMEM_CAP_BYTES = 60 * 1024 * 1024
NEG_BIG = -1e30
Q_BLOCK = 128
WIN_Q_BLOCKS = 8
DIFF_TK = 512
DIFF_HEADS_PER_STEP = 2
ROW_SUBTILE = 512
LRU_ROWS = 512
LRU_PAD = 8
LOG2E = math.log2(math.e)
Q_SCALE = HEAD_DIM ** -0.5 * LOG2E


def _params(semantics, vmem_bytes):
    return pltpu.CompilerParams(dimension_semantics=semantics,
                                vmem_limit_bytes=min(int(vmem_bytes), VMEM_CAP_BYTES))


def _resident(shape):
    nd = len(shape)
    return pl.BlockSpec(shape, lambda *_: (0,) * nd, pipeline_mode=pl.Buffered(1))


def _layer_resident(shape, layer):
    nd = len(shape)
    return pl.BlockSpec((None,) + tuple(shape), lambda *_: (layer,) + (0,) * nd,
                        pipeline_mode=pl.Buffered(1))


def _rms(x, g):
    return x * lax.rsqrt(jnp.mean(x * x, axis=-1, keepdims=True) + EPS) * g


def _mod_kernel(c_ref, w_ref, b_ref, o_ref):
    c = c_ref[...]
    s = c * jax.nn.sigmoid(c)
    o_ref[...] = jnp.dot(s, w_ref[...], preferred_element_type=F32,
                         precision=lax.Precision.HIGHEST) + b_ref[...]


def _modulation(c_rows, w_mod, b_mod):
    depth, d, n = w_mod.shape
    rows = c_rows.shape[0]
    tn = 1536
    return pl.pallas_call(
        _mod_kernel,
        grid=(depth, n // tn),
        in_specs=[pl.BlockSpec((rows, d), lambda l, j: (0, 0)),
                  pl.BlockSpec((None, d, tn), lambda l, j: (l, 0, j)),
                  pl.BlockSpec((None, 1, tn), lambda l, j: (l, 0, j))],
        out_specs=pl.BlockSpec((None, rows, tn), lambda l, j: (l, 0, j)),
        out_shape=jax.ShapeDtypeStruct((depth, rows, n), F32),
        compiler_params=_params(("parallel", "parallel"), 32 << 20),
        name="modulation",
    )(c_rows, w_mod, b_mod.reshape(depth, 1, n))


def _in_proj_kernel(x_ref, sh_ref, sc_ref, g_ref, cos_ref, sa_ref, sb_ref, w_ref, *out_refs, queries):
    if queries:
        qa_ref, ka_ref, va_ref, qb_ref, kb_ref, vb_ref, cu_ref, cg_ref, gt_ref = out_refs
    else:
        ka_ref, va_ref, kb_ref, vb_ref, cu_ref, cg_ref = out_refs
    h = _rms(x_ref[...], g_ref[...])
    h = (h * (1.0 + sc_ref[...]) + sh_ref[...]).astype(BF16)

    def proj(lo, width):
        return jnp.dot(h, w_ref[:, lo:lo + width], preferred_element_type=F32)

    cos = cos_ref[...]
    sin_hi = sa_ref[...]
    sin_lo = sb_ref[...]

    def rope(a):
        outs = []
        for k in range(a.shape[1] // LANES):
            blk = a[:, k * LANES:(k + 1) * LANES]
            outs.append(blk * cos + pltpu.roll(blk, HEAD_DIM // 4, 1) * sin_hi
                        + pltpu.roll(blk, LANES - HEAD_DIM // 4, 1) * sin_lo)
        return jnp.concatenate(outs, axis=1)

    kv_a = proj(O_AK, O_BQ - O_AK)
    ka_ref[...] = rope(kv_a[:, :O_AV - O_AK]).astype(BF16)
    va_ref[...] = kv_a[:, O_AV - O_AK:].T.astype(BF16)
    kb_ref[...] = rope(proj(O_BK, O_BV - O_BK)).astype(BF16)
    vb_ref[...] = proj(O_BV, O_CU - O_BV).T.astype(BF16)
    cu_ref[...] = proj(O_CU, C_WIDTH).astype(BF16)
    cg_ref[...] = proj(O_CG, C_WIDTH).astype(BF16)
    if queries:
        qa_ref[...] = (rope(proj(O_AQ, O_AK - O_AQ)) * Q_SCALE).T.astype(BF16)
        qb_ref[...] = (rope(proj(O_BQ, O_BK - O_BQ)) * Q_SCALE).T.astype(BF16)
        for k in range(N_BRANCH):
            gt_ref[:, k * D_MODEL:(k + 1) * D_MODEL] = jax.nn.sigmoid(
                proj(O_GT + k * D_MODEL, D_MODEL)).astype(BF16)


def _in_proj(x2, shift, scale, g0, cos, sin_hi, sin_lo, w_bf, layer, tm, queries=True):
    t, d = x2.shape
    nt = t // tm
    nb = shift.shape[0]
    npos = cos.shape[0] // tm
    tok = lambda w: pl.BlockSpec((tm, w), lambda i: (i, 0))
    feat = lambda w: pl.BlockSpec((w, tm), lambda i: (0, i))
    modspec = pl.BlockSpec((None, 1, d), lambda i: (i // (nt // nb), 0, 0))
    tabspec = pl.BlockSpec((tm, LANES), lambda i: (i % npos, 0))
    wq, wkv = A_HEADS * HEAD_DIM, A_KV_HEADS * HEAD_DIM
    wb = B_HEADS * B_VDIM
    out_shape = (
        jax.ShapeDtypeStruct((wq, t), BF16), jax.ShapeDtypeStruct((t, wkv), BF16),
        jax.ShapeDtypeStruct((wkv, t), BF16), jax.ShapeDtypeStruct((nt, wb, tm), BF16),
        jax.ShapeDtypeStruct((t, wb), BF16), jax.ShapeDtypeStruct((wb, t), BF16),
        jax.ShapeDtypeStruct((t, C_WIDTH), BF16), jax.ShapeDtypeStruct((t, C_WIDTH), BF16),
        jax.ShapeDtypeStruct((t, N_BRANCH * D_MODEL), BF16))
    out_specs = (feat(wq), tok(wkv), feat(wkv), pl.BlockSpec((None, wb, tm), lambda i: (i, 0, 0)), tok(wb), feat(wb),
                 tok(C_WIDTH), tok(C_WIDTH), tok(N_BRANCH * D_MODEL))
    if not queries:
        keep = (1, 2, 4, 5, 6, 7)
        out_shape = tuple(out_shape[i] for i in keep)
        out_specs = tuple(out_specs[i] for i in keep)
    vmem = d * IN_WIDTH * 2 + 2 * tm * d * 4 + 2 * tm * IN_WIDTH * 2 + 6 * tm * d * 4 + (8 << 20)
    return pl.pallas_call(
        functools.partial(_in_proj_kernel, queries=queries),
        grid=(nt,),
        in_specs=[tok(d), modspec, modspec, _resident((1, d)), tabspec, tabspec, tabspec,
                  _layer_resident((d, IN_WIDTH), layer)],
        out_specs=out_specs,
        out_shape=out_shape,
        compiler_params=_params(("parallel",), vmem),
        name="in_proj",
    )(x2, shift, scale, g0, cos, sin_hi, sin_lo, w_bf)


def _sink_scores(k_all, qt):
    tq = qt.shape[1]
    zero = jnp.zeros((HEAD_DIM, tq), qt.dtype)
    scores = []
    for j in range(A_KV_HEADS):
        cols = []
        for g in range(A_GROUPS):
            hd = j * A_GROUPS + g
            qh = qt[hd * HEAD_DIM:(hd + 1) * HEAD_DIM, :]
            cols.append(jnp.concatenate([qh, zero] if j == 0 else [zero, qh], axis=0))
        qz = jnp.concatenate(cols, axis=1)
        scores.append(jnp.dot(k_all, qz, preferred_element_type=F32))
    return scores


def _sink_softmax_pv(scores, vt_all, sink_row, bias, n_win):
    tq = scores[0].shape[1] // A_GROUPS
    outs = []
    ones = jnp.ones((BF16_ROWS, vt_all.shape[1]), BF16)
    for j in range(A_KV_HEADS):
        s = scores[j]
        sink = sink_row[:, j * A_GROUPS * tq:(j + 1) * A_GROUPS * tq] * LOG2E
        if n_win:
            sw = s[:n_win] + jnp.concatenate([bias] * A_GROUPS, axis=1)
            sc = s[n_win:]
            m = jnp.maximum(jnp.maximum(jnp.max(sw, axis=0, keepdims=True),
                                        jnp.max(sc, axis=0, keepdims=True)), sink)
            p = jnp.concatenate([jnp.exp2(sw - m), jnp.exp2(sc - m)], axis=0).astype(BF16)
        else:
            m = jnp.maximum(jnp.max(s, axis=0, keepdims=True), sink)
            p = jnp.exp2(s - m).astype(BF16)
        vt_aug = jnp.concatenate([vt_all[j * HEAD_DIM:(j + 1) * HEAD_DIM, :], ones], axis=0)
        o = jnp.dot(vt_aug, p, preferred_element_type=F32)
        l = o[HEAD_DIM:HEAD_DIM + 1] + jnp.exp2(sink - m)
        o = o[:HEAD_DIM] * (1.0 / l)
        for g in range(A_GROUPS):
            outs.append(o[:, g * tq:(g + 1) * tq])
    return jnp.concatenate(outs, axis=0)


def _win_attn_kernel(q_ref, kp_ref, kc_ref, kn_ref, vp_ref, vc_ref, vn_ref, kx_ref, vx_ref,
                     sink_ref, o_ref, *, n_tok):
    tq = Q_BLOCK
    q_blocks = q_ref.shape[1] // tq
    first = pl.program_id(1) * q_blocks
    k_cat = jnp.concatenate([kp_ref[...], kc_ref[...], kn_ref[...]], axis=0)
    vt_cat = jnp.concatenate([vp_ref[...], vc_ref[...], vn_ref[...]], axis=1)
    kx = kx_ref[...]
    vx = vx_ref[...]
    n_win = 3 * tq
    r = lax.broadcasted_iota(jnp.int32, (n_win, tq), 0)
    c = lax.broadcasted_iota(jnp.int32, (n_win, tq), 1)
    band = jnp.where(jnp.abs(r - tq - c) <= WINDOW, 0.0, NEG_BIG).astype(F32)

    def scores(b):
        k_all = jnp.concatenate([k_cat[b * tq:(b + 3) * tq], kx], axis=0)
        return _sink_scores(k_all, q_ref[:, b * tq:(b + 1) * tq])

    s_next = scores(0)
    for b in range(q_blocks):
        s_cur = s_next
        if b + 1 < q_blocks:
            s_next = scores(b + 1)
        kpos = (first + b - 1) * tq + r
        bias = band + jnp.where(kpos >= 0, 0.0, NEG_BIG) + jnp.where(kpos < n_tok, 0.0, NEG_BIG)
        vt_all = jnp.concatenate([vt_cat[:, b * tq:(b + 3) * tq], vx], axis=1)
        yt = _sink_softmax_pv(s_cur, vt_all, sink_ref[...], bias, n_win)
        o_ref[b * tq:(b + 1) * tq, :] = yt.T.astype(o_ref.dtype)


def _ctx_attn_kernel(q_ref, kx_ref, vx_ref, sink_ref, o_ref):
    yt = _sink_softmax_pv(_sink_scores(kx_ref[...], q_ref[...]), vx_ref[...], sink_ref[...], None, 0)
    o_ref[...] = yt.T.astype(o_ref.dtype)


def _sink_row(sink, tq):
    return jnp.repeat(sink.astype(F32), tq).reshape(1, A_HEADS * tq)


def _window_attention(qt, k, vt, kx, vxt, sink, bsz, n_tok, n_ctx):
    tq = Q_BLOCK
    nq = n_tok // tq
    qb = min(WIN_Q_BLOCKS, nq)
    ns = nq // qb
    wq, wkv = A_HEADS * HEAD_DIM, A_KV_HEADS * HEAD_DIM
    prev = lambda b, n: b * nq + jnp.maximum(n * qb - 1, 0)
    cur = lambda b, n: b * ns + n
    nxt = lambda b, n: b * nq + jnp.minimum(n * qb + qb, nq - 1)
    edge_k = lambda f: pl.BlockSpec((tq, wkv), lambda b, n: (f(b, n), 0))
    edge_v = lambda f: pl.BlockSpec((wkv, tq), lambda b, n: (0, f(b, n)))
    return pl.pallas_call(
        functools.partial(_win_attn_kernel, n_tok=n_tok),
        grid=(bsz, ns),
        in_specs=[pl.BlockSpec((wq, qb * tq), lambda b, n: (0, cur(b, n))),
                  edge_k(prev), pl.BlockSpec((qb * tq, wkv), lambda b, n: (cur(b, n), 0)), edge_k(nxt),
                  edge_v(prev), pl.BlockSpec((wkv, qb * tq), lambda b, n: (0, cur(b, n))), edge_v(nxt),
                  pl.BlockSpec((n_ctx, wkv), lambda b, n: (b, 0)),
                  pl.BlockSpec((wkv, n_ctx), lambda b, n: (0, b)),
                  pl.BlockSpec((1, A_HEADS * tq), lambda b, n: (0, 0))],
        out_specs=pl.BlockSpec((qb * tq, wq), lambda b, n: (cur(b, n), 0)),
        out_shape=jax.ShapeDtypeStruct((bsz * n_tok, wq), BF16),
        compiler_params=_params(("parallel", "parallel"), 32 << 20),
        name="window_attention",
    )(qt, k, k, k, vt, vt, vt, kx, vxt, _sink_row(sink, tq))


def _context_attention(qt, kx, vxt, sink, bsz, n_ctx):
    tq = Q_BLOCK
    nq = n_ctx // tq
    wq, wkv = A_HEADS * HEAD_DIM, A_KV_HEADS * HEAD_DIM
    return pl.pallas_call(
        _ctx_attn_kernel,
        grid=(bsz, nq),
        in_specs=[pl.BlockSpec((wq, tq), lambda b, n: (0, b * nq + n)),
                  pl.BlockSpec((n_ctx, wkv), lambda b, n: (b, 0)),
                  pl.BlockSpec((wkv, n_ctx), lambda b, n: (0, b)),
                  pl.BlockSpec((1, A_HEADS * tq), lambda b, n: (0, 0))],
        out_specs=pl.BlockSpec((tq, wq), lambda b, n: (b * nq + n, 0)),
        out_shape=jax.ShapeDtypeStruct((bsz * n_ctx, wq), BF16),
        compiler_params=_params(("parallel", "parallel"), 32 << 20),
        name="context_attention",
    )(qt, kx, vxt, _sink_row(sink, tq))


def _diff_attn_kernel(*refs, n_lat_chunks, tk, lam_init, heads):
    if n_lat_chunks:
        q_ref, k_ref, v_ref, kx_ref, vx_ref, dl_ref, g_ref, o_ref, acc_ref = refs
    else:
        q_ref, kx_ref, vx_ref, dl_ref, g_ref, o_ref, acc_ref = refs
    n_q_tiles, _, tq = q_ref.shape
    hw = B_VDIM
    row = lax.broadcasted_iota(jnp.int32, (hw, tq), 0)
    zero = jnp.zeros((hw, tq), q_ref.dtype)
    n = n_lat_chunks + 1

    def head_slice(hd):
        return slice(hd * hw, (hd + 1) * hw)

    def keys(hd, i):
        if i < n_lat_chunks:
            return k_ref[i * tk:(i + 1) * tk, head_slice(hd)]
        return kx_ref[:, head_slice(hd)]

    def values(hd, i):
        v = v_ref[head_slice(hd), i * tk:(i + 1) * tk] if i < n_lat_chunks else vx_ref[head_slice(hd), :]
        return jnp.concatenate([v, jnp.ones((BF16_ROWS, v.shape[1]), BF16)], axis=0)

    def attend(qi):
        qz = []
        for hd in range(heads):
            qt = q_ref[qi, head_slice(hd), :]
            qz.append(jnp.concatenate([jnp.where(row < HEAD_DIM, qt, zero),
                                       jnp.where(row >= HEAD_DIM, qt, zero)], axis=1))

        def scores(hd, i):
            s = jnp.dot(keys(hd, i), qz[hd], preferred_element_type=F32)
            return s, jnp.max(s, axis=0, keepdims=True)

        m = [jnp.full((1, 2 * tq), NEG_BIG, F32) for _ in range(heads)]
        acc = [jnp.zeros((hw + BF16_ROWS, 2 * tq), F32) for _ in range(heads)]
        nxt = [scores(hd, 0) for hd in range(heads)]
        e_prev = [None] * heads
        alpha = [None] * heads
        for c in range(n):
            cur = list(nxt)
            for hd in range(heads):
                if c + 1 < n:
                    nxt[hd] = scores(hd, c + 1)
                if c:
                    acc[hd] = alpha[hd] * acc[hd] + jnp.dot(values(hd, c - 1), e_prev[hd],
                                                            preferred_element_type=F32)
            for hd in range(heads):
                s, smax = cur[hd]
                m_new = jnp.maximum(m[hd], smax)
                alpha[hd] = jnp.exp2(m[hd] - m_new)
                e_prev[hd] = jnp.exp2(s - m_new).astype(BF16)
                m[hd] = m_new
        return [alpha[hd] * acc[hd] + jnp.dot(values(hd, n - 1), e_prev[hd], preferred_element_type=F32)
                for hd in range(heads)]

    dl = dl_ref[...]
    lam = (jnp.exp(jnp.sum(dl[0:1] * dl[1:2], axis=1, keepdims=True))
           - jnp.exp(jnp.sum(dl[2:3] * dl[3:4], axis=1, keepdims=True)) + lam_init)

    def finish(qi, accs):
        for hd in range(heads):
            a = accs[hd]
            o = a[:hw] * (1.0 / a[hw:hw + 1])
            y = (o[:, :tq] - lam * o[:, tq:]).T
            y = (_rms(y, g_ref[...]) * (1.0 - lam_init)).astype(o_ref.dtype)
            o_ref[pl.ds(pl.multiple_of(qi * tq, tq), tq), head_slice(hd)] = y

    if n_q_tiles == 1:
        finish(0, attend(0))
        return

    first = attend(0)
    for hd in range(heads):
        acc_ref[hd] = first[hd]

    def body(qi, carry):
        parked = [acc_ref[hd] for hd in range(heads)]
        new = attend(qi)
        finish(qi - 1, parked)
        for hd in range(heads):
            acc_ref[hd] = new[hd]
        return carry
    lax.fori_loop(1, n_q_tiles, body, 0)
    finish(n_q_tiles - 1, [acc_ref[hd] for hd in range(heads)])


def _diff_attention(qt, k, vt, kx, vxt, diff_lambda, subln, lam_init, bsz, n_tok, n_ctx):
    nq_tok = n_tok if n_tok else n_ctx
    tq = qt.shape[2]
    nqt = nq_tok // tq
    tk = min(DIFF_TK, n_tok) if n_tok else 0
    heads = DIFF_HEADS_PER_STEP
    hw = heads * B_VDIM
    in_specs = [pl.BlockSpec((nqt, hw, tq), lambda b, h: (b, h, 0))]
    args = [qt]
    if n_tok:
        in_specs += [pl.BlockSpec((n_tok, hw), lambda b, h: (b, h)),
                     pl.BlockSpec((hw, n_tok), lambda b, h: (h, b))]
        args += [k, vt]
    in_specs += [pl.BlockSpec((n_ctx, hw), lambda b, h: (b, h)),
                 pl.BlockSpec((hw, n_ctx), lambda b, h: (h, b)),
                 pl.BlockSpec((4, HEAD_DIM), lambda b, h: (0, 0)),
                 pl.BlockSpec((1, B_VDIM), lambda b, h: (0, 0))]
    args += [kx, vxt, diff_lambda.astype(F32), subln.astype(F32).reshape(1, B_VDIM)]
    return pl.pallas_call(
        functools.partial(_diff_attn_kernel, n_lat_chunks=(n_tok // tk if n_tok else 0), tk=tk,
                          lam_init=lam_init, heads=heads),
        grid=(bsz, B_HEADS // heads),
        in_specs=in_specs,
        out_specs=pl.BlockSpec((nq_tok, hw), lambda b, h: (b, h)),
        out_shape=jax.ShapeDtypeStruct((bsz * nq_tok, B_HEADS * B_VDIM), BF16),
        scratch_shapes=[pltpu.VMEM((heads, B_VDIM + BF16_ROWS, 2 * tq), F32)],
        compiler_params=_params(("parallel", "parallel"), 48 << 20),
        name="diff_attention",
    )(*args)


def _scan_segments(a, b, h_in, reverse):
    rows, w = a.shape
    seg = rows // SUBLANES
    order = range(seg - 1, -1, -1) if reverse else range(seg)
    h_loc = [None] * seg
    a_cum = [None] * seg
    h = prod = None
    for i in order:
        ai = a[i * SUBLANES:(i + 1) * SUBLANES]
        bi = b[i * SUBLANES:(i + 1) * SUBLANES]
        h = bi if h is None else ai * h + bi
        prod = ai if prod is None else ai * prod
        h_loc[i] = h
        a_cum[i] = prod
    sub = lax.broadcasted_iota(jnp.int32, (SUBLANES, w), 0)
    e, p = h, prod
    for d in (1, 2, 4):
        keep = (sub < SUBLANES - d) if reverse else (sub >= d)
        shift = SUBLANES - d if reverse else d
        p_s = jnp.where(keep, pltpu.roll(p, shift, 0), 1.0)
        e_s = jnp.where(keep, pltpu.roll(e, shift, 0), 0.0)
        e = e + p * e_s
        p = p * p_s
    seg_out = e + p * h_in
    if reverse:
        seg_in = jnp.where(sub < SUBLANES - 1, pltpu.roll(seg_out, SUBLANES - 1, 0), h_in)
        h_out = seg_out[0:1]
    else:
        seg_in = jnp.where(sub >= 1, pltpu.roll(seg_out, 1, 0), h_in)
        h_out = seg_out[SUBLANES - 1:SUBLANES]
    return [h_loc[i] + a_cum[i] * seg_in for i in range(seg)], h_out


def _lru_kernel(cu_ref, cg_ref, h0f_ref, h0b_ref, cw_ref, cb_ref, wg_ref, bg_ref, lam_ref,
                y_ref, hf_ref, hb_ref, xpad_ref, hs_ref, u_ref, *, rows):
    n = cu_ref.shape[0]
    nchunks = n // rows
    w = C_WIDTH
    seg = rows // SUBLANES
    slabs = w // LANES

    def load(ref, idx):
        return jnp.concatenate([ref[j, idx, :] for j in range(slabs)], axis=1)

    def store(ref, idx, val):
        for j in range(slabs):
            ref[j, idx, :] = val[:, j * LANES:(j + 1) * LANES]

    store(xpad_ref, pl.ds(0, LRU_PAD), jnp.zeros((LRU_PAD, w), F32))
    store(xpad_ref, pl.ds(LRU_PAD + n, LRU_PAD), jnp.zeros((LRU_PAD, w), F32))

    def fill(ci, carry):
        s = pl.multiple_of(ci * rows, rows)
        store(xpad_ref, pl.ds(s + LRU_PAD, rows), cu_ref[pl.ds(s, rows), :].astype(F32))
        return carry
    lax.fori_loop(0, nchunks, fill, 0)

    lam = lam_ref[...]
    decay = LRU_C * (jnp.maximum(-lam, 0.0) + jnp.log1p(jnp.exp(-jnp.abs(lam))))
    cw = cw_ref[...]
    cb = cb_ref[...]

    def conv(s):
        first = LRU_PAD - CONV_W // 2
        taps = [load(xpad_ref, pl.ds(s + (first + q), SUBLANES, stride=seg))
                for q in range(seg + CONV_W - 1)]
        groups = []
        for i in range(seg):
            acc = cb
            for k in range(CONV_W):
                acc = acc + cw[k:k + 1] * taps[i + k]
            groups.append(acc)
        return jnp.concatenate(groups, axis=0)

    def coeffs(u, d):
        g = jnp.dot(u.astype(BF16), wg_ref[:, d * 2 * w:(d + 1) * 2 * w],
                    preferred_element_type=F32) + bg_ref[:, d * 2 * w:(d + 1) * 2 * w]
        r = jax.nn.sigmoid(g[:, :w])
        i_gate = jax.nn.sigmoid(g[:, w:])
        z = decay[d:d + 1] * r
        a = jnp.exp(-z)
        v = (1.0 + a * a) * jnp.tanh(z)
        root = jnp.where(v > 0.0, v * lax.rsqrt(v), 0.0)
        return a, root * i_gate * u

    r_i = lax.broadcasted_iota(jnp.int32, (rows, rows), 0)
    c_i = lax.broadcasted_iota(jnp.int32, (rows, rows), 1)
    to_tokens = jnp.where(c_i == (r_i % seg) * SUBLANES + r_i // seg, 1.0, 0.0).astype(BF16)
    to_segments = jnp.where(r_i == (c_i % seg) * SUBLANES + c_i // seg, 1.0, 0.0).astype(BF16)

    def bwd(ci, h):
        s = pl.multiple_of((nchunks - 1 - ci) * rows, rows)
        u = conv(s)
        u_ref[pl.ds(s, rows), :] = u
        a, bx = coeffs(u, 1)
        hs, h = _scan_segments(a, bx, h, True)
        hs_ref[pl.ds(s, rows), :] = jnp.concatenate(hs, axis=0)
        return h
    hb_ref[...] = lax.fori_loop(0, nchunks, bwd, h0b_ref[...])

    def fwd(ci, h):
        s = pl.multiple_of(ci * rows, rows)
        a, bx = coeffs(u_ref[pl.ds(s, rows), :], 0)
        hs, h = _scan_segments(a, bx, h, False)
        gate = jax.nn.gelu(jnp.dot(to_segments, cg_ref[pl.ds(s, rows), :], preferred_element_type=F32))
        y = ((jnp.concatenate(hs, axis=0) + hs_ref[pl.ds(s, rows), :]) * gate).astype(BF16)
        y_ref[pl.ds(s, rows), :] = jnp.dot(to_tokens, y, preferred_element_type=F32).astype(y_ref.dtype)
        return h
    hf_ref[...] = lax.fori_loop(0, nchunks, fwd, h0f_ref[...])


def _bidir_lru(cu, cg, h0f, h0b, conv_w, conv_b, wg_bf, bg, lam, layer, bsz, n):
    w = C_WIDTH
    rows = min(LRU_ROWS, n)
    seq = pl.BlockSpec((n, w), lambda b: (b, 0))
    st = pl.BlockSpec((None, 1, w), lambda b: (b, 0, 0))
    vmem = 6 * n * w * 2 + 2 * n * w * 4 + (n + 2 * LRU_PAD) * w * 4 + (12 << 20)
    return pl.pallas_call(
        functools.partial(_lru_kernel, rows=rows),
        grid=(bsz,),
        in_specs=[seq, seq, st, st, _resident((CONV_W, w)), _resident((1, w)),
                  _layer_resident((w, 4 * w), layer), _resident((1, 4 * w)), _resident((2, w))],
        out_specs=(seq, st, st),
        out_shape=(jax.ShapeDtypeStruct((bsz * n, w), BF16),
                   jax.ShapeDtypeStruct((bsz, 1, w), F32), jax.ShapeDtypeStruct((bsz, 1, w), F32)),
        scratch_shapes=[pltpu.VMEM((w // LANES, n + 2 * LRU_PAD, LANES), F32),
                        pltpu.VMEM((n, w), F32), pltpu.VMEM((n, w), F32)],
        compiler_params=_params(("parallel",), vmem),
        name="bidir_lru",
    )(cu, cg, h0f, h0b, conv_w, conv_b, wg_bf, bg, lam)


def _merge_kernel(ya_ref, yb_ref, yc_ref, gt_ref, x_ref, gx_ref, g_ref, wbr_ref, wout_ref, o_ref):
    sub = min(ROW_SUBTILE, x_ref.shape[0])
    for r in range(x_ref.shape[0] // sub):
        rows = slice(r * sub, (r + 1) * sub)
        acc = None
        for n, y_ref in enumerate((ya_ref, yb_ref, yc_ref)):
            z = jnp.dot(y_ref[rows, :], wbr_ref[n], preferred_element_type=F32)
            gate = gt_ref[rows, n * D_MODEL:(n + 1) * D_MODEL].astype(F32)
            acc = gate * z if acc is None else acc + gate * z
        mix = jnp.dot(acc.astype(BF16), wout_ref[...], preferred_element_type=F32)
        o_ref[rows, :] = x_ref[rows, :] + gx_ref[...] * _rms(mix, g_ref[...])


def _merge(ya, yb, yc, gt, x2, gate_x, g1, wbr_bf, wout_bf, layer, tm):
    t, d = x2.shape
    nt = t // tm
    nb = gate_x.shape[0]
    tok = lambda w: pl.BlockSpec((tm, w), lambda i: (i, 0))
    modspec = pl.BlockSpec((None, 1, d), lambda i: (i // (nt // nb), 0, 0))
    vmem = (N_BRANCH * BRANCH_W * d + d * d) * 2 + 2 * tm * (3 * BRANCH_W + 3 * d) * 2 \
        + 4 * tm * d * 4 + 6 * tm * d * 4 + (8 << 20)
    return pl.pallas_call(
        _merge_kernel,
        grid=(nt,),
        in_specs=[tok(BRANCH_W), tok(BRANCH_W), tok(BRANCH_W), tok(N_BRANCH * d), tok(d), modspec,
                  _resident((1, d)), _layer_resident((N_BRANCH, BRANCH_W, d), layer),
                  _layer_resident((d, d), layer)],
        out_specs=tok(d),
        out_shape=jax.ShapeDtypeStruct((t, d), F32),
        compiler_params=_params(("parallel",), vmem),
        name="gated_merge",
    )(ya, yb, yc, gt, x2, gate_x, g1, wbr_bf, wout_bf)


def _ffn_kernel(x_ref, sh_ref, sc_ref, gx_ref, g2_ref, g3_ref, w1_ref, w2_ref, o_ref, *, n_chunks):
    tiles = FFN_HIDDEN // MXU_TILE
    bounds = [(-(-tiles * c // n_chunks)) * MXU_TILE for c in range(n_chunks + 1)]
    sub = min(ROW_SUBTILE, x_ref.shape[0])
    for r in range(x_ref.shape[0] // sub):
        rows = slice(r * sub, (r + 1) * sub)
        x = x_ref[rows, :]
        h = _rms(x, g2_ref[...])
        h = (h * (1.0 + sc_ref[...]) + sh_ref[...]).astype(BF16)
        f = None
        for lo, hi in zip(bounds[:-1], bounds[1:]):
            gate = jnp.dot(h, w1_ref[:, lo:hi], preferred_element_type=F32)
            up = jnp.dot(h, w1_ref[:, FFN_HIDDEN + lo:FFN_HIDDEN + hi], preferred_element_type=F32)
            act = (gate * jax.nn.sigmoid(gate) * up).astype(BF16)
            part = jnp.dot(act, w2_ref[lo:hi, :], preferred_element_type=F32)
            f = part if f is None else f + part
        o_ref[rows, :] = x + gx_ref[...] * _rms(f, g3_ref[...])


def _ffn(x2, shift, scale, gate_x, g2, g3, w1_bf, w2_bf, layer, tm):
    t, d = x2.shape
    nt = t // tm
    nb = shift.shape[0]
    tok = pl.BlockSpec((tm, d), lambda i: (i, 0))
    modspec = pl.BlockSpec((None, 1, d), lambda i: (i // (nt // nb), 0, 0))
    n_chunks = 2
    vmem = 3 * d * FFN_HIDDEN * 2 + 4 * tm * d * 4 + 3 * tm * (FFN_HIDDEN // n_chunks) * 4 \
        + 4 * tm * d * 4 + (8 << 20)
    return pl.pallas_call(
        functools.partial(_ffn_kernel, n_chunks=n_chunks),
        grid=(nt,),
        in_specs=[tok, modspec, modspec, modspec, _resident((1, d)), _resident((1, d)),
                  _layer_resident((d, 2 * FFN_HIDDEN), layer), _layer_resident((FFN_HIDDEN, d), layer)],
        out_specs=tok,
        out_shape=jax.ShapeDtypeStruct((t, d), F32),
        compiler_params=_params(("parallel",), vmem),
        name="swiglu_ffn",
    )(x2, shift, scale, gate_x, g2, g3, w1_bf, w2_bf)


def _rope_tables(n_tokens):
    rd = HEAD_DIM // 4
    t = jnp.arange(n_tokens)
    pos = jnp.stack([t // GRID_W, t % GRID_W], axis=-1).astype(F32)
    inv = 1.0 / (ROPE_THETA ** (jnp.arange(rd, dtype=F32) * 2.0 / (HEAD_DIM // 2)))
    ang = pos[:, :, None] * inv
    cos = jnp.cos(ang)[:, :, None, :]
    sin = jnp.sin(ang)[:, :, None, :]
    zeros = jnp.zeros_like(sin)
    cos64 = jnp.concatenate([cos, cos], axis=2).reshape(n_tokens, HEAD_DIM)
    hi64 = jnp.concatenate([zeros, sin], axis=2).reshape(n_tokens, HEAD_DIM)
    lo64 = jnp.concatenate([-sin, zeros], axis=2).reshape(n_tokens, HEAD_DIM)
    rep = LANES // HEAD_DIM
    return jnp.tile(cos64, (1, rep)), jnp.tile(hi64, (1, rep)), jnp.tile(lo64, (1, rep))


def _identity_tables(rows):
    z = jnp.zeros((rows, LANES), F32)
    return jnp.ones((rows, LANES), F32), z, z


def _gate_weights(lru_w):
    eye = jnp.eye(C_BLOCKS, dtype=lru_w.dtype)
    dense = jnp.einsum("ldgncf,nm->lncdgmf", lru_w, eye)
    return dense.reshape(lru_w.shape[0], C_WIDTH, 4 * C_WIDTH).astype(BF16)


def kernel(x, c, ctx, c_ctx, w_mod, b_mod, norm_g, w_in, attn_sink, diff_lambda, diff_subln, conv_w,
           conv_b, lru_w, lru_b, lru_lambda, w_branch, w_out, w_ffn_in, w_ffn_out):
    bsz, n_tok, d = x.shape
    n_ctx = ctx.shape[1]
    depth = w_mod.shape[0]
    tm = min(512, n_tok)
    tm2 = min(2 * ROW_SUBTILE, n_tok)
    tmc = min(256, n_ctx)

    mod_rows = 2 * SUBLANES
    c_rows = jnp.zeros((mod_rows, d), F32).at[:bsz].set(c).at[bsz].set(c_ctx)
    mods = _modulation(c_rows, w_mod, b_mod)

    cos, sin_hi, sin_lo = _rope_tables(n_tok)
    cos_c, sin_hi_c, sin_lo_c = _identity_tables(tmc)

    x2 = x.reshape(bsz * n_tok, d)
    cx2 = ctx.reshape(bsz * n_ctx, d)
    zero_state = jnp.zeros((bsz, 1, C_WIDTH), F32)

    w_in_bf = w_in.astype(BF16)
    wg_bf = _gate_weights(lru_w)
    wbr_bf = w_branch.astype(BF16)
    wout_bf = w_out.astype(BF16)
    w1_bf = w_ffn_in.astype(BF16)
    w2_bf = w_ffn_out.astype(BF16)

    for l in range(depth):
        need_ctx = l < depth - 1
        lam_init = 0.8 - 0.6 * math.exp(-0.3 * l)
        mx = [mods[l, :bsz, k * d:(k + 1) * d].reshape(bsz, 1, d) for k in range(6)]
        mc = [mods[l, bsz:bsz + 1, k * d:(k + 1) * d].reshape(1, 1, d) for k in range(6)]
        g = [norm_g[l, k].reshape(1, d).astype(F32) for k in range(4)]
        bg = lru_b[l].astype(F32).reshape(1, 4 * C_WIDTH)
        cw = conv_w[l].astype(F32)
        cb = conv_b[l].astype(F32).reshape(1, C_WIDTH)
        lam = lru_lambda[l].astype(F32)

        ctx_proj = _in_proj(cx2, mc[0], mc[1], g[0], cos_c, sin_hi_c, sin_lo_c, w_in_bf, l, tmc,
                            queries=need_ctx)
        if need_ctx:
            qa_c, ka_c, va_c, qb_c, kb_c, vb_c, cu_c, cg_c, gt_c = ctx_proj
        else:
            ka_c, va_c, kb_c, vb_c, cu_c, cg_c = ctx_proj
        yc_c, hf_c, hb_c = _bidir_lru(cu_c, cg_c, zero_state, zero_state, cw, cb, wg_bf, bg, lam,
                                      l, bsz, n_ctx)

        (qa, ka, va, qb, kb, vb, cu, cg, gt) = _in_proj(
            x2, mx[0], mx[1], g[0], cos, sin_hi, sin_lo, w_in_bf, l, tm)
        ya = _window_attention(qa, ka, va, ka_c, va_c, attn_sink[l], bsz, n_tok, n_ctx)
        yb = _diff_attention(qb, kb, vb, kb_c, vb_c, diff_lambda[l], diff_subln[l], lam_init,
                             bsz, n_tok, n_ctx)
        yc, _, _ = _bidir_lru(cu, cg, hf_c, hb_c, cw, cb, wg_bf, bg, lam, l, bsz, n_tok)
        x2 = _merge(ya, yb, yc, gt, x2, mx[2], g[1], wbr_bf, wout_bf, l, tm2)
        x2 = _ffn(x2, mx[3], mx[4], mx[5], g[2], g[3], w1_bf, w2_bf, l, tm2)

        if need_ctx:
            ya_c = _context_attention(qa_c, ka_c, va_c, attn_sink[l], bsz, n_ctx)
            yb_c = _diff_attention(qb_c, None, None, kb_c, vb_c, diff_lambda[l], diff_subln[l],
                                   lam_init, bsz, 0, n_ctx)
            cx2 = _merge(ya_c, yb_c, yc_c, gt_c, cx2, mc[2], g[1], wbr_bf, wout_bf, l, tmc)
            cx2 = _ffn(cx2, mc[3], mc[4], mc[5], g[2], g[3], w1_bf, w2_bf, l, tmc)

    return x2.reshape(bsz, n_tok, d)
```
